```python
import math
import jax, jax.numpy as jnp
from jax import lax
import numpy as np

D_MODEL = 1024
BATCH = 2
SEQ = 16384
DEPTH = 4
DEC_BATCH = 8
DEC_SEQ = 2048
PAST_LEN = 128

HEAD_DIM = 64
BLOCK = 128
WINDOW = 128
GRID_W = 64
ROPE_THETA = 10000.0
NORM_EPS = 1e-6

A_HEADS = 8
A_KV_HEADS = 2
B_HEADS = 8
B_KV_HEADS = 2

MLA_HEADS = 8
MLA_Q_RANK = 256
MLA_KV_RANK = 128
MLA_NOPE_DIM = 64
MLA_ROPE_DIM = 32
MLA_V_DIM = 64
MLA_QK_DIM = MLA_NOPE_DIM + MLA_ROPE_DIM

DENSE_FF = 3584
N_EXPERTS = 8
TOP_K = 2
EXPERT_FF = 3584

N_EVEN = (DEPTH + 1) // 2
N_ODD = DEPTH // 2
EVEN_SPLITS = [A_HEADS * HEAD_DIM, A_KV_HEADS * HEAD_DIM, A_KV_HEADS * HEAD_DIM,
               B_HEADS * HEAD_DIM, B_KV_HEADS * HEAD_DIM, B_KV_HEADS * HEAD_DIM]
EVEN_IN = sum(EVEN_SPLITS)
EVEN_MIX = (A_HEADS + B_HEADS) * HEAD_DIM
ODD_IN = MLA_Q_RANK + MLA_KV_RANK + MLA_ROPE_DIM
ODD_MIX = MLA_HEADS * MLA_V_DIM

kernel_name = "hybrid_bidir_encoder_window_axial_mla_moe"


def rms_norm(x, g):
    xf = x.astype(jnp.float32)
    y = xf * lax.rsqrt(jnp.mean(xf * xf, axis=-1, keepdims=True) + NORM_EPS)
    return (y * g.astype(jnp.float32)).astype(x.dtype)


def rope_angles(pos, dim):
    inv = ROPE_THETA ** (-jnp.arange(0, dim, 2, dtype=jnp.float32) / dim)
    return pos.astype(jnp.float32)[:, None] * inv[None, :]


def apply_rope(x, ang):
    half = x.shape[-1] // 2
    cos = jnp.cos(ang)[:, None, :]
    sin = jnp.sin(ang)[:, None, :]
    x1 = x[..., :half].astype(jnp.float32)
    x2 = x[..., half:].astype(jnp.float32)
    return jnp.concatenate([x1 * cos - x2 * sin, x2 * cos + x1 * sin], axis=-1).astype(x.dtype)


def alibi_slopes(n):
    return jnp.asarray(2.0 ** (-8.0 * np.arange(1, n + 1) / n), dtype=jnp.float32)


def window_attention(q, k, v, sink):
    bsz, seq, hk, grp, d = q.shape
    nb = seq // BLOCK
    qb = q.reshape(bsz, nb, BLOCK, hk, grp, d)

    def band(t):
        tp = jnp.pad(t, ((0, 0), (BLOCK, BLOCK), (0, 0), (0, 0))).reshape(bsz, nb + 2, BLOCK, hk, t.shape[-1])
        return jnp.concatenate([tp[:, :-2], tp[:, 1:-1], tp[:, 2:]], axis=2)

    kb, vb = band(k), band(v)
    qi = jnp.arange(BLOCK)[:, None]
    kj = jnp.arange(3 * BLOCK)[None, :]
    dist = jnp.abs(kj - BLOCK - qi)
    kabs = jnp.arange(nb)[:, None, None] * BLOCK - BLOCK + kj[None]
    valid = (dist[None] <= WINDOW) & (kabs >= 0) & (kabs < seq)
    slopes = alibi_slopes(hk * grp).reshape(hk, grp)[:, :, None, None]
    s = jnp.einsum('bnqhgd,bnkhd->bnhgqk', qb, kb).astype(jnp.float32) * (d ** -0.5)
    s = s - slopes * dist.astype(jnp.float32)
    s = jnp.where(valid[None, :, None, None], s, -jnp.inf)
    sk = sink.astype(jnp.float32).reshape(hk, grp)[:, :, None, None]
    m = jnp.maximum(jnp.max(s, axis=-1, keepdims=True), sk)
    e = jnp.exp(s - m)
    p = e / (jnp.sum(e, axis=-1, keepdims=True) + jnp.exp(sk - m))
    o = jnp.einsum('bnhgqk,bnkhd->bnqhgd', p.astype(v.dtype), vb)
    return o.reshape(bsz, seq, hk * grp * vb.shape[-1])


def dense_attention(q, k, v):
    bsz, seq, hk, grp, dk = q.shape
    nb = seq // BLOCK
    scale = dk ** -0.5
    qb = jnp.moveaxis(q.reshape(bsz, nb, BLOCK, hk, grp, dk), 1, 0)

    def attend_block(qblk):
        s = jnp.einsum('bqhgd,bkhd->bhgqk', qblk, k).astype(jnp.float32) * scale
        p = jax.nn.softmax(s, axis=-1)
        return jnp.einsum('bhgqk,bkhd->bqhgd', p.astype(v.dtype), v)

    o = lax.map(attend_block, qb)
    return jnp.moveaxis(o, 0, 1).reshape(bsz, seq, hk * grp * v.shape[-1])


def swiglu(x, w_gate, w_up, w_down):
    return (jax.nn.silu(x @ w_gate) * (x @ w_up)) @ w_down


def moe_swiglu(x, router, w_gate, w_up, w_down):
    bsz, seq, d = x.shape
    t = x.reshape(-1, d)
    logits = (t @ router).astype(jnp.float32)
    top_v, top_i = lax.top_k(logits, TOP_K)
    w = jax.nn.softmax(top_v, axis=-1)
    gates = jnp.sum(jax.nn.one_hot(top_i, N_EXPERTS, dtype=jnp.float32) * w[..., None], axis=1).astype(t.dtype)
    out = jnp.zeros_like(t)
    for e in range(N_EXPERTS):
        out = out + gates[:, e:e + 1] * swiglu(t, w_gate[e], w_up[e], w_down[e])
    return out.reshape(bsz, seq, d)


def even_mixer(h, w_in, sink, q_gain, k_gain, w_out, ang_axial):
    bsz, seq, _ = h.shape
    proj = h @ w_in
    aq, ak, av, bq, bk, bv = jnp.split(proj, list(np.cumsum(EVEN_SPLITS)[:-1]), axis=-1)
    ya = window_attention(aq.reshape(bsz, seq, A_KV_HEADS, A_HEADS // A_KV_HEADS, HEAD_DIM),
                          ak.reshape(bsz, seq, A_KV_HEADS, HEAD_DIM),
                          av.reshape(bsz, seq, A_KV_HEADS, HEAD_DIM), sink)
    bq = apply_rope(rms_norm(bq.reshape(bsz, seq, B_HEADS, HEAD_DIM), q_gain), ang_axial)
    bk = apply_rope(rms_norm(bk.reshape(bsz, seq, B_KV_HEADS, HEAD_DIM), k_gain), ang_axial)
    yb = dense_attention(bq.reshape(bsz, seq, B_KV_HEADS, B_HEADS // B_KV_HEADS, HEAD_DIM), bk,
                         bv.reshape(bsz, seq, B_KV_HEADS, HEAD_DIM))
    return jnp.concatenate([ya, yb], axis=-1) @ w_out


def odd_mixer(h, w_in, q_norm, w_q_up, kv_norm, w_kv_up, w_out, ang_1d):
    bsz, seq, _ = h.shape
    proj = h @ w_in
    c_q, c_kv, k_rope = jnp.split(proj, [MLA_Q_RANK, MLA_Q_RANK + MLA_KV_RANK], axis=-1)
    q = (rms_norm(c_q, q_norm) @ w_q_up).reshape(bsz, seq, MLA_HEADS, MLA_QK_DIM)
    q = jnp.concatenate([q[..., :MLA_NOPE_DIM], apply_rope(q[..., MLA_NOPE_DIM:], ang_1d)], axis=-1)
    kv = (rms_norm(c_kv, kv_norm) @ w_kv_up).reshape(bsz, seq, MLA_HEADS, MLA_NOPE_DIM + MLA_V_DIM)
    k_nope, v = kv[..., :MLA_NOPE_DIM], kv[..., MLA_NOPE_DIM:]
    k_rope = apply_rope(k_rope[:, :, None, :], ang_1d)
    k = jnp.concatenate([k_nope, jnp.broadcast_to(k_rope, (bsz, seq, MLA_HEADS, MLA_ROPE_DIM))], axis=-1)
    y = dense_attention(q[:, :, :, None, :], k, v)
    return y @ w_out


def encoder_trunk(x, ev_norm_mix, ev_w_in, a_sink, b_q_norm, b_k_norm, ev_w_out, ev_norm_ffn,
                  ffn_w_gate, ffn_w_up, ffn_w_down, od_norm_mix, od_w_in, mla_q_norm, mla_w_q_up,
                  mla_kv_norm, mla_w_kv_up, od_w_out, od_norm_ffn, moe_router, moe_w_gate, moe_w_up,
                  moe_w_down, final_norm):
    seq = x.shape[1]
    n_rows = seq // GRID_W
    row = jnp.repeat(jnp.arange(n_rows), GRID_W)
    col = jnp.tile(jnp.arange(GRID_W), n_rows)
    ang_axial = jnp.concatenate([rope_angles(row, HEAD_DIM // 2), rope_angles(col, HEAD_DIM // 2)], axis=-1)
    ang_1d = rope_angles(jnp.arange(seq), MLA_ROPE_DIM)
    for layer in range(DEPTH):
        i = layer // 2
        if layer % 2 == 0:
            x = x + even_mixer(rms_norm(x, ev_norm_mix[i]), ev_w_in[i], a_sink[i], b_q_norm[i],
                               b_k_norm[i], ev_w_out[i], ang_axial)
            x = x + swiglu(rms_norm(x, ev_norm_ffn[i]), ffn_w_gate[i], ffn_w_up[i], ffn_w_down[i])
        else:
            x = x + odd_mixer(rms_norm(x, od_norm_mix[i]), od_w_in[i], mla_q_norm[i], mla_w_q_up[i],
                              mla_kv_norm[i], mla_w_kv_up[i], od_w_out[i], ang_1d)
            x = x + moe_swiglu(rms_norm(x, od_norm_ffn[i]), moe_router[i], moe_w_gate[i],
                               moe_w_up[i], moe_w_down[i])
    return rms_norm(x, final_norm)


def _dense(key, shape, fan_in):
    return jax.random.normal(key, shape, jnp.float32) * (fan_in ** -0.5)


def _gain(key, shape):
    return 1.0 + 0.02 * jax.random.normal(key, shape, jnp.float32)


def setup_inputs(seed: int = 0) -> dict:
    key = jax.random.key(seed)
    ks = jax.random.split(key, 25)
    ne, no = N_EVEN, N_ODD
    return {
        'x_prompt': jax.random.normal(ks[0], (BATCH, SEQ, D_MODEL), jnp.float32),
        'x_sample': jax.random.normal(ks[1], (DEC_BATCH, DEC_SEQ, D_MODEL), jnp.float32),
        'ev_norm_mix': _gain(ks[2], (ne, D_MODEL)),
        'ev_w_in': _dense(ks[3], (ne, D_MODEL, EVEN_IN), D_MODEL),
        'a_sink': 0.5 * jax.random.normal(ks[4], (ne, A_HEADS), jnp.float32),
        'b_q_norm': _gain(ks[5], (ne, HEAD_DIM)),
        'b_k_norm': _gain(ks[6], (ne, HEAD_DIM)),
        'ev_w_out': _dense(ks[7], (ne, EVEN_MIX, D_MODEL), EVEN_MIX),
        'ev_norm_ffn': _gain(ks[8], (ne, D_MODEL)),
        'ffn_w_gate': _dense(ks[9], (ne, D_MODEL, DENSE_FF), D_MODEL),
        'ffn_w_up': _dense(ks[10], (ne, D_MODEL, DENSE_FF), D_MODEL),
        'ffn_w_down': _dense(ks[11], (ne, DENSE_FF, D_MODEL), DENSE_FF),
        'od_norm_mix': _gain(ks[12], (no, D_MODEL)),
        'od_w_in': _dense(ks[13], (no, D_MODEL, ODD_IN), D_MODEL),
        'mla_q_norm': _gain(ks[14], (no, MLA_Q_RANK)),
        'mla_w_q_up': _dense(ks[15], (no, MLA_Q_RANK, MLA_HEADS * MLA_QK_DIM), MLA_Q_RANK),
        'mla_kv_norm': _gain(ks[16], (no, MLA_KV_RANK)),
        'mla_w_kv_up': _dense(ks[17], (no, MLA_KV_RANK, MLA_HEADS * (MLA_NOPE_DIM + MLA_V_DIM)), MLA_KV_RANK),
        'od_w_out': _dense(ks[18], (no, ODD_MIX, D_MODEL), ODD_MIX),
        'od_norm_ffn': _gain(ks[19], (no, D_MODEL)),
        'moe_router': _dense(ks[20], (no, D_MODEL, N_EXPERTS), D_MODEL),
        'moe_w_gate': _dense(ks[21], (no, N_EXPERTS, D_MODEL, EXPERT_FF), D_MODEL),
        'moe_w_up': _dense(ks[22], (no, N_EXPERTS, D_MODEL, EXPERT_FF), D_MODEL),
        'moe_w_down': _dense(ks[23], (no, N_EXPERTS, EXPERT_FF, D_MODEL), EXPERT_FF),
        'final_norm': _gain(ks[24], (D_MODEL,)),
    }


def reference(x_prompt, x_sample, ev_norm_mix, ev_w_in, a_sink, b_q_norm, b_k_norm, ev_w_out,
              ev_norm_ffn, ffn_w_gate, ffn_w_up, ffn_w_down, od_norm_mix, od_w_in, mla_q_norm,
              mla_w_q_up, mla_kv_norm, mla_w_kv_up, od_w_out, od_norm_ffn, moe_router, moe_w_gate,
              moe_w_up, moe_w_down, final_norm):
    weights = (ev_norm_mix, ev_w_in, a_sink, b_q_norm, b_k_norm, ev_w_out, ev_norm_ffn,
               ffn_w_gate, ffn_w_up, ffn_w_down, od_norm_mix, od_w_in, mla_q_norm, mla_w_q_up,
               mla_kv_norm, mla_w_kv_up, od_w_out, od_norm_ffn, moe_router, moe_w_gate, moe_w_up,
               moe_w_down, final_norm)
    y_prompt = encoder_trunk(x_prompt, *weights)
    y_sample = encoder_trunk(x_sample, *weights)
    return (y_prompt, y_sample)
```

```python
import functools
import math

import numpy as np
import jax
import jax.numpy as jnp
from jax import lax
from jax.experimental import pallas as pl
from jax.experimental.pallas import tpu as pltpu

F32 = jnp.float32
BF16 = jnp.bfloat16

D_MODEL = 1024
HEAD_DIM = 64
WINDOW = 128
GRID_W = 64
ROPE_THETA = 10000.0
NORM_EPS = 1e-6
A_HEADS, A_KV_HEADS = 8, 2
B_HEADS, B_KV_HEADS = 8, 2
MLA_HEADS = 8
MLA_Q_RANK, MLA_KV_RANK = 256, 128
MLA_NOPE_DIM, MLA_ROPE_DIM, MLA_V_DIM = 64, 32, 64
MLA_QK_DIM = MLA_NOPE_DIM + MLA_ROPE_DIM
N_EXPERTS = 8
LANES = 128
LOG2E = math.log2(math.e)
VMEM_LIMIT = 48 * 1024 * 1024

TOK_TILE = 512
FFN_TOK_TILE = 1024
FFN_F_TILE = 512
KEY_CHUNK = 512
WIN_Q = 128


def _rms(x, g):
    return x * lax.rsqrt(jnp.mean(x * x, axis=-1, keepdims=True) + NORM_EPS) * g


def _swap_halves(x, half):
    lane = lax.broadcasted_iota(jnp.int32, x.shape, 1)
    return jnp.where(lane < half, pltpu.roll(x, LANES - half, 1), pltpu.roll(x, half, 1))


def _swap_halves_at(x, base, half):
    lane = lax.broadcasted_iota(jnp.int32, x.shape, 1)
    return jnp.where(lane < base + half, pltpu.roll(x, LANES - half, 1), pltpu.roll(x, half, 1))


def _params(sem):
    return pltpu.CompilerParams(dimension_semantics=sem, vmem_limit_bytes=VMEM_LIMIT)


_EV_AQ, _EV_AK, _EV_BQ, _EV_BK, _EV_V, _EV_END = 0, 1024, 1280, 2304, 2560, 2816


def _even_in_kernel(x_ref, g_ref, w_ref, cos_ref, sin_ref, qg_ref, kg_ref,
                    aq_ref, ak_ref, avt_ref, bq_ref, bk_ref, bvt_ref):
    xb = _rms(x_ref[...], g_ref[...]).astype(BF16)
    cos = cos_ref[...]
    sin = sin_ref[...]
    qscale = (HEAD_DIM ** -0.5) * LOG2E

    def norm_rope(blk, gain):
        ms = jnp.sum(blk * blk, axis=-1, keepdims=True) * (1.0 / HEAD_DIM)
        y = blk * lax.rsqrt(ms + NORM_EPS) * gain
        return y * cos + _swap_halves(y, HEAD_DIM // 2) * sin

    pa = jnp.dot(xb, w_ref[:, _EV_AQ:_EV_AK], preferred_element_type=F32)
    aq_ref[...] = (pa * qscale).astype(BF16)
    ak_ref[...] = jnp.dot(xb, w_ref[:, _EV_AK:_EV_BQ], preferred_element_type=F32).astype(BF16)
    pbq = jnp.dot(xb, w_ref[:, _EV_BQ:_EV_BK], preferred_element_type=F32)
    for h in range(B_HEADS):
        sl = slice(h * LANES, (h + 1) * LANES)
        bq_ref[:, sl] = (norm_rope(pbq[:, sl], qg_ref[...]) * qscale).astype(BF16)
    pbk = jnp.dot(xb, w_ref[:, _EV_BK:_EV_V], preferred_element_type=F32)
    for h in range(B_KV_HEADS):
        sl = slice(h * LANES, (h + 1) * LANES)
        bk_ref[:, sl] = norm_rope(pbk[:, sl], kg_ref[...]).astype(BF16)
    pv = jnp.dot(xb, w_ref[:, _EV_V:_EV_END], preferred_element_type=F32)
    avt_ref[0] = pv[:, :LANES].T.astype(BF16)
    bvt_ref[0] = pv[:, LANES:].T.astype(BF16)


def _even_in(x2, bsz, seq, g, w_exp, cos_t, sin_t, qg, kg):
    tokens = x2.shape[0]
    tt = min(TOK_TILE, seq)
    assert seq % tt == 0
    ns = seq // tt
    row = lambda i: (i, 0)
    const = lambda i: (0, 0)
    tab = lambda i: (i % ns, 0)
    vt = lambda i: (i // ns, 0, i % ns)
    return pl.pallas_call(
        _even_in_kernel,
        grid=(tokens // tt,),
        in_specs=[
            pl.BlockSpec((tt, D_MODEL), row),
            pl.BlockSpec((1, D_MODEL), const),
            pl.BlockSpec((D_MODEL, _EV_END), const),
            pl.BlockSpec((tt, LANES), tab),
            pl.BlockSpec((tt, LANES), tab),
            pl.BlockSpec((1, LANES), const),
            pl.BlockSpec((1, LANES), const),
        ],
        out_specs=[
            pl.BlockSpec((tt, A_HEADS * LANES), row),
            pl.BlockSpec((tt, A_KV_HEADS * LANES), row),
            pl.BlockSpec((1, A_KV_HEADS * HEAD_DIM, tt), vt),
            pl.BlockSpec((tt, B_HEADS * LANES), row),
            pl.BlockSpec((tt, B_KV_HEADS * LANES), row),
            pl.BlockSpec((1, B_KV_HEADS * HEAD_DIM, tt), vt),
        ],
        out_shape=[
            jax.ShapeDtypeStruct((tokens, A_HEADS * LANES), BF16),
            jax.ShapeDtypeStruct((tokens, A_KV_HEADS * LANES), BF16),
            jax.ShapeDtypeStruct((bsz, A_KV_HEADS * HEAD_DIM, seq), BF16),
            jax.ShapeDtypeStruct((tokens, B_HEADS * LANES), BF16),
            jax.ShapeDtypeStruct((tokens, B_KV_HEADS * LANES), BF16),
            jax.ShapeDtypeStruct((bsz, B_KV_HEADS * HEAD_DIM, seq), BF16),
        ],
        compiler_params=_params(("arbitrary",)),
        name="even_in_proj",
    )(x2, g, w_exp, cos_t, sin_t, qg, kg)


_MLA_ROPE_BASE = MLA_NOPE_DIM


def _odd_in_kernel(x_ref, g_ref, w_ref, qn_ref, wq_ref, kvn_ref, wk_ref, wv_ref,
                   cos_ref, sin_ref, q_ref, k_ref, vt_ref):
    xb = _rms(x_ref[...], g_ref[...]).astype(BF16)
    cos = cos_ref[...]
    sin = sin_ref[...]
    qscale = (MLA_QK_DIM ** -0.5) * LOG2E

    def rope(blk):
        return blk * cos + _swap_halves_at(blk, _MLA_ROPE_BASE, MLA_ROPE_DIM // 2) * sin

    proj = jnp.dot(xb, w_ref[...], preferred_element_type=F32)
    cq = _rms(proj[:, :MLA_Q_RANK], qn_ref[...]).astype(BF16)
    ckv = _rms(proj[:, MLA_Q_RANK:MLA_Q_RANK + MLA_KV_RANK], kvn_ref[...]).astype(BF16)
    kr = rope(proj[:, MLA_Q_RANK + MLA_KV_RANK:])
    q = jnp.dot(cq, wq_ref[...], preferred_element_type=F32)
    kn = jnp.dot(ckv, wk_ref[...], preferred_element_type=F32)
    for h in range(MLA_HEADS):
        sl = slice(h * LANES, (h + 1) * LANES)
        q_ref[:, sl] = (rope(q[:, sl]) * qscale).astype(BF16)
        k_ref[:, sl] = (kn[:, sl] + kr).astype(BF16)
    v = jnp.dot(ckv, wv_ref[...], preferred_element_type=F32)
    for c in range(MLA_HEADS * MLA_V_DIM // LANES):
        vt_ref[0, c * LANES:(c + 1) * LANES, :] = v[:, c * LANES:(c + 1) * LANES].T.astype(BF16)


def _odd_in(x2, bsz, seq, g, w_exp, qn, wq_exp, kvn, wk_exp, wv, cos_t, sin_t):
    tokens = x2.shape[0]
    tt = min(TOK_TILE, seq)
    assert seq % tt == 0
    ns = seq // tt
    row = lambda i: (i, 0)
    const = lambda i: (0, 0)
    tab = lambda i: (i % ns, 0)
    vt = lambda i: (i // ns, 0, i % ns)
    hl = MLA_HEADS * LANES
    hv = MLA_HEADS * MLA_V_DIM
    return pl.pallas_call(
        _odd_in_kernel,
        grid=(tokens // tt,),
        in_specs=[
            pl.BlockSpec((tt, D_MODEL), row),
            pl.BlockSpec((1, D_MODEL), const),
            pl.BlockSpec((D_MODEL, 4 * LANES), const),
            pl.BlockSpec((1, MLA_Q_RANK), const),
            pl.BlockSpec((MLA_Q_RANK, hl), const),
            pl.BlockSpec((1, MLA_KV_RANK), const),
            pl.BlockSpec((MLA_KV_RANK, hl), const),
            pl.BlockSpec((MLA_KV_RANK, hv), const),
            pl.BlockSpec((tt, LANES), tab),
            pl.BlockSpec((tt, LANES), tab),
        ],
        out_specs=[
            pl.BlockSpec((tt, hl), row),
            pl.BlockSpec((tt, hl), row),
            pl.BlockSpec((1, hv, tt), vt),
        ],
        out_shape=[
            jax.ShapeDtypeStruct((tokens, hl), BF16),
            jax.ShapeDtypeStruct((tokens, hl), BF16),
            jax.ShapeDtypeStruct((bsz, hv, seq), BF16),
        ],
        compiler_params=_params(("arbitrary",)),
        name="odd_in_proj",
    )(x2, g, w_exp, qn, wq_exp, kvn, wk_exp, wv, cos_t, sin_t)


def _pair_transpose(pieces):
    outs = []
    for a in range(0, len(pieces), 2):
        outs.append(jnp.concatenate([pieces[a], pieces[a + 1]], axis=0).T)
    return outs[0] if len(outs) == 1 else jnp.concatenate(outs, axis=1)


def _dense_attn_kernel(q_ref, k_ref, vt_ref, o_ref, m_ref, l_ref, acc_ref, *, kb, grp, tq, tk, seq):
    n = grp * tq
    dv = HEAD_DIM
    pieces = []
    for j in range(kb):
        qs = jnp.concatenate(
            [q_ref[:, (j * grp + g) * LANES:(j * grp + g + 1) * LANES] for g in range(grp)], axis=0)
        m_ref[...] = jnp.full((1, n), -jnp.inf, F32)
        l_ref[...] = jnp.zeros((1, n), F32)
        acc_ref[...] = jnp.zeros((dv, n), F32)

        def body(c, carry, j=j, qs=qs):
            off = pl.multiple_of(c * tk, tk)
            kc = k_ref[pl.ds(off, tk), j * LANES:(j + 1) * LANES]
            s = lax.dot_general(kc, qs, (((1,), (1,)), ((), ())),
                                preferred_element_type=F32)
            m_old = m_ref[...]
            m_new = jnp.maximum(m_old, jnp.max(s, axis=0, keepdims=True))
            alpha = jnp.exp2(m_old - m_new)
            p = jnp.exp2(s - m_new)
            l_ref[...] = alpha * l_ref[...] + jnp.sum(p, axis=0, keepdims=True)
            vc = vt_ref[0, j * dv:(j + 1) * dv, pl.ds(off, tk)]
            pv = jnp.dot(vc, p.astype(BF16), preferred_element_type=F32)
            acc_ref[...] = alpha * acc_ref[...] + pv
            m_ref[...] = m_new
            return carry

        lax.fori_loop(0, seq // tk, body, 0)
        o = acc_ref[...] / l_ref[...]
        for g in range(grp):
            pieces.append(o[:, g * tq:(g + 1) * tq])
    o_ref[...] = _pair_transpose(pieces).astype(o_ref.dtype)


def _dense_attn(q, k, vt, bsz, seq, n_kv, grp, kb, tq):
    tokens = q.shape[0]
    tq = min(tq, seq)
    tk = min(KEY_CHUNK, seq)
    assert seq % tq == 0 and seq % tk == 0 and n_kv % kb == 0 and (kb * grp) % 2 == 0
    nq = seq // tq
    n = grp * tq
    kern = functools.partial(_dense_attn_kernel, kb=kb, grp=grp, tq=tq, tk=tk, seq=seq)
    return pl.pallas_call(
        kern,
        grid=(bsz, n_kv // kb, nq),
        in_specs=[
            pl.BlockSpec((tq, kb * grp * LANES), lambda b, h, i: (b * nq + i, h)),
            pl.BlockSpec((seq, kb * LANES), lambda b, h, i: (b, h)),
            pl.BlockSpec((1, kb * HEAD_DIM, seq), lambda b, h, i: (b, h, 0)),
        ],
        out_specs=pl.BlockSpec((tq, kb * grp * HEAD_DIM), lambda b, h, i: (b * nq + i, h)),
        out_shape=jax.ShapeDtypeStruct((tokens, n_kv * grp * HEAD_DIM), BF16),
        scratch_shapes=[
            pltpu.VMEM((1, n), F32),
            pltpu.VMEM((1, n), F32),
            pltpu.VMEM((HEAD_DIM, n), F32),
        ],
        compiler_params=_params(("arbitrary", "arbitrary", "arbitrary")),
        name="dense_attn",
    )(q, k, vt)


def _window_attn_kernel(q_ref, k_ref, vt_ref, slope_ref, sink_ref, o_ref, *, grp, seq):
    tq = WIN_Q
    span = 3 * WIN_Q
    n = grp * tq
    i = pl.program_id(2)
    w0 = pl.multiple_of(jnp.clip((i - 1) * tq, 0, seq - span), tq)
    qs = jnp.concatenate([q_ref[:, g * LANES:(g + 1) * LANES] for g in range(grp)], axis=0)
    kc = k_ref[pl.ds(w0, span), :]
    s = lax.dot_general(kc, qs, (((1,), (1,)), ((), ())), preferred_element_type=F32)
    kpos = w0 + lax.broadcasted_iota(jnp.int32, (span, n), 0)
    qpos = i * tq + lax.broadcasted_iota(jnp.int32, (span, n), 1) % tq
    dist = jnp.abs(kpos - qpos)
    s = s - slope_ref[0] * dist.astype(F32)
    s = jnp.where(dist <= WINDOW, s, -jnp.inf)
    sink = sink_ref[0]
    m = jnp.maximum(jnp.max(s, axis=0, keepdims=True), sink)
    e = jnp.exp2(s - m)
    denom = jnp.sum(e, axis=0, keepdims=True) + jnp.exp2(sink - m)
    vc = vt_ref[0, :, pl.ds(w0, span)]
    o = jnp.dot(vc, e.astype(BF16), preferred_element_type=F32) / denom
    o_ref[...] = _pair_transpose([o[:, g * tq:(g + 1) * tq] for g in range(grp)]).astype(o_ref.dtype)


def _window_attn(q, k, vt, slope_row, sink_row, bsz, seq, n_kv, grp):
    tokens = q.shape[0]
    tq = WIN_Q
    assert seq % tq == 0 and seq >= 3 * tq
    nq = seq // tq
    n = grp * tq
    kern = functools.partial(_window_attn_kernel, grp=grp, seq=seq)
    return pl.pallas_call(
        kern,
        grid=(bsz, n_kv, nq),
        in_specs=[
            pl.BlockSpec((tq, grp * LANES), lambda b, h, i: (b * nq + i, h)),
            pl.BlockSpec((seq, LANES), lambda b, h, i: (b, h)),
            pl.BlockSpec((1, HEAD_DIM, seq), lambda b, h, i: (b, h, 0)),
            pl.BlockSpec((1, 1, n), lambda b, h, i: (h, 0, 0)),
            pl.BlockSpec((1, 1, n), lambda b, h, i: (h, 0, 0)),
        ],
        out_specs=pl.BlockSpec((tq, grp * HEAD_DIM), lambda b, h, i: (b * nq + i, h)),
        out_shape=jax.ShapeDtypeStruct((tokens, n_kv * grp * HEAD_DIM), BF16),
        compiler_params=_params(("arbitrary", "arbitrary", "arbitrary")),
        name="window_attn",
    )(q, k, vt, slope_row, sink_row)


def _out_proj_kernel(*refs, n_in):
    x_ref = refs[0]
    y_refs = refs[1:1 + n_in]
    w_refs = refs[1 + n_in:1 + 2 * n_in]
    o_ref = refs[1 + 2 * n_in]
    acc = x_ref[...]
    for y_ref, w_ref in zip(y_refs, w_refs):
        acc = acc + jnp.dot(y_ref[...], w_ref[...], preferred_element_type=F32)
    o_ref[...] = acc


def _out_proj(x2, ys, ws):
    tokens = x2.shape[0]
    tt = min(TOK_TILE, tokens)
    row = lambda i: (i, 0)
    const = lambda i: (0, 0)
    kern = functools.partial(_out_proj_kernel, n_in=len(ys))
    return pl.pallas_call(
        kern,
        grid=(tokens // tt,),
        in_specs=[pl.BlockSpec((tt, D_MODEL), row)]
        + [pl.BlockSpec((tt, y.shape[1]), row) for y in ys]
        + [pl.BlockSpec(w.shape, const) for w in ws],
        out_specs=pl.BlockSpec((tt, D_MODEL), row),
        out_shape=jax.ShapeDtypeStruct((tokens, D_MODEL), F32),
        compiler_params=_params(("arbitrary",)),
        name="out_proj",
    )(x2, *ys, *ws)


def _ffn_kernel(x_ref, g_ref, rh_ref, rl_ref, wg_ref, wu_ref, wd_ref, gf_ref, o_ref,
                xn_ref, gate_ref, acc_ref, *, n_exp, final_norm):
    e = pl.program_id(1)
    f = pl.program_id(2)
    first = jnp.logical_and(e == 0, f == 0)
    last = jnp.logical_and(e == pl.num_programs(1) - 1, f == pl.num_programs(2) - 1)
    tt = x_ref.shape[0]

    @pl.when(first)
    def _():
        xn = _rms(x_ref[...], g_ref[...])
        xh = xn.astype(BF16)
        xn_ref[...] = xh
        acc_ref[...] = jnp.zeros_like(acc_ref)
        if n_exp > 1:
            xl = (xn - xh.astype(F32)).astype(BF16)
            logits = (jnp.dot(xh, rh_ref[...], preferred_element_type=F32)
                      + jnp.dot(xh, rl_ref[...], preferred_element_type=F32)
                      + jnp.dot(xl, rh_ref[...], preferred_element_type=F32))
            lane = lax.broadcasted_iota(jnp.int32, (tt, LANES), 1).astype(F32)
            lg = jnp.where(lane < n_exp, logits, -jnp.inf)
            m1 = jnp.max(lg, axis=1, keepdims=True)
            i1 = jnp.min(jnp.where(lg == m1, lane, float(LANES)), axis=1, keepdims=True)
            lg2 = jnp.where(lane == i1, -jnp.inf, lg)
            m2 = jnp.max(lg2, axis=1, keepdims=True)
            i2 = jnp.min(jnp.where(lg2 == m2, lane, float(LANES)), axis=1, keepdims=True)
            e2 = jnp.exp(m2 - m1)
            w1 = 1.0 / (1.0 + e2)
            w2 = e2 * w1
            gate_ref[...] = jnp.where(lane == i1, w1, 0.0) + jnp.where(lane == i2, w2, 0.0)

    xb = xn_ref[...]
    hg = jnp.dot(xb, wg_ref[0], preferred_element_type=F32)
    hu = jnp.dot(xb, wu_ref[0], preferred_element_type=F32)
    h = hg * jax.nn.sigmoid(hg) * hu
    if n_exp > 1:
        lane = lax.broadcasted_iota(jnp.int32, (tt, LANES), 1)
        ge = jnp.sum(jnp.where(lane == e, gate_ref[...], 0.0), axis=1, keepdims=True)
        h = h * ge
    acc_ref[...] += jnp.dot(h.astype(BF16), wd_ref[0], preferred_element_type=F32)

    @pl.when(last)
    def _():
        y = x_ref[...] + acc_ref[...]
        if final_norm:
            y = _rms(y, gf_ref[...])
        o_ref[...] = y


def _ffn(x2, g, r_hi, r_lo, wg, wu, wd, g_final, final_norm):
    tokens = x2.shape[0]
    n_exp, _, ff = wg.shape
    tt = min(FFN_TOK_TILE, tokens)
    fc = FFN_F_TILE
    assert tokens % tt == 0 and ff % fc == 0
    kern = functools.partial(_ffn_kernel, n_exp=n_exp, final_norm=final_norm)
    row = lambda i, e, f: (i, 0)
    const = lambda i, e, f: (0, 0)
    return pl.pallas_call(
        kern,
        grid=(tokens // tt, n_exp, ff // fc),
        in_specs=[
            pl.BlockSpec((tt, D_MODEL), row),
            pl.BlockSpec((1, D_MODEL), const),
            pl.BlockSpec((D_MODEL, LANES), const),
            pl.BlockSpec((D_MODEL, LANES), const),
            pl.BlockSpec((1, D_MODEL, fc), lambda i, e, f: (e, 0, f)),
            pl.BlockSpec((1, D_MODEL, fc), lambda i, e, f: (e, 0, f)),
            pl.BlockSpec((1, fc, D_MODEL), lambda i, e, f: (e, f, 0)),
            pl.BlockSpec((1, D_MODEL), const),
        ],
        out_specs=pl.BlockSpec((tt, D_MODEL), row),
        out_shape=jax.ShapeDtypeStruct((tokens, D_MODEL), F32),
        scratch_shapes=[
            pltpu.VMEM((tt, D_MODEL), BF16),
            pltpu.VMEM((tt, LANES), F32),
            pltpu.VMEM((tt, D_MODEL), F32),
        ],
        compiler_params=_params(("arbitrary", "arbitrary", "arbitrary")),
        name="ffn",
    )(x2, g, r_hi, r_lo, wg, wu, wd, g_final)


def _pad_heads(w, n_heads, width, offset=0):
    r = w.shape[0]
    w = w.reshape(r, n_heads, width)
    w = jnp.pad(w, ((0, 0), (0, 0), (offset, LANES - width - offset)))
    return w.reshape(r, n_heads * LANES)


def _rope_tables(ang, base):
    half = ang.shape[1]
    cos = jnp.cos(ang)
    sin = jnp.sin(ang)
    pad = ((0, 0), (base, LANES - base - 2 * half))
    cos_t = jnp.pad(jnp.concatenate([cos, cos], axis=1) - 1.0, pad) + 1.0
    sin_t = jnp.pad(jnp.concatenate([-sin, sin], axis=1), pad)
    return cos_t, sin_t


def _rope_angles(pos, dim):
    inv = ROPE_THETA ** (-jnp.arange(0, dim, 2, dtype=F32) / dim)
    return pos.astype(F32)[:, None] * inv[None, :]


def _lane_gain(g):
    return jnp.pad(g.astype(F32), (0, LANES - g.shape[0]))[None, :]


def _trunk(x, p):
    bsz, seq, _ = x.shape
    x2 = x.reshape(bsz * seq, D_MODEL)
    n_rows = seq // GRID_W
    row = jnp.repeat(jnp.arange(n_rows), GRID_W)
    col = jnp.tile(jnp.arange(GRID_W), n_rows)
    ang_axial = jnp.concatenate([_rope_angles(row, HEAD_DIM // 2), _rope_angles(col, HEAD_DIM // 2)], axis=-1)
    cos_ax, sin_ax = _rope_tables(ang_axial, 0)
    cos_1d, sin_1d = _rope_tables(_rope_angles(jnp.arange(seq), MLA_ROPE_DIM), _MLA_ROPE_BASE)
    depth = p["ev_norm_mix"].shape[0] + p["od_norm_mix"].shape[0]
    a_grp = A_HEADS // A_KV_HEADS
    for layer in range(depth):
        i = layer // 2
        last_layer = layer == depth - 1
        if layer % 2 == 0:
            aq, ak, avt, bq, bk, bvt = _even_in(
                x2, bsz, seq, p["ev_norm_mix"][i][None, :], p["ev_w_in"][i], cos_ax, sin_ax,
                p["b_q_norm"][i], p["b_k_norm"][i])
            ya = _window_attn(aq, ak, avt, p["a_slope"], p["a_sink"][i], bsz, seq, A_KV_HEADS, a_grp)
            yb = _dense_attn(bq, bk, bvt, bsz, seq, B_KV_HEADS, B_HEADS // B_KV_HEADS, 1, 256)
            x2 = _out_proj(x2, [ya, yb], [p["ev_w_out_a"][i], p["ev_w_out_b"][i]])
            x2 = _ffn(x2, p["ev_norm_ffn"][i][None, :], p["zero_router"], p["zero_router"],
                      p["ffn_w_gate"][i][None], p["ffn_w_up"][i][None], p["ffn_w_down"][i][None],
                      p["final_norm"], last_layer)
        else:
            q, k, vt = _odd_in(
                x2, bsz, seq, p["od_norm_mix"][i][None, :], p["od_w_in"][i], p["mla_q_norm"][i][None, :],
                p["mla_w_q_up"][i], p["mla_kv_norm"][i][None, :], p["mla_w_k_up"][i], p["mla_w_v_up"][i],
                cos_1d, sin_1d)
            yc = _dense_attn(q, k, vt, bsz, seq, MLA_HEADS, 1, 2, 512)
            x2 = _out_proj(x2, [yc], [p["od_w_out"][i]])
            x2 = _ffn(x2, p["od_norm_ffn"][i][None, :], p["router_hi"][i], p["router_lo"][i],
                      p["moe_w_gate"][i], p["moe_w_up"][i], p["moe_w_down"][i],
                      p["final_norm"], last_layer)
    return x2.reshape(bsz, seq, D_MODEL)


def _prepare(ev_norm_mix, ev_w_in, a_sink, b_q_norm, b_k_norm, ev_w_out, ev_norm_ffn,
             ffn_w_gate, ffn_w_up, ffn_w_down, od_norm_mix, od_w_in, mla_q_norm, mla_w_q_up,
             mla_kv_norm, mla_w_kv_up, od_w_out, od_norm_ffn, moe_router, moe_w_gate, moe_w_up,
             moe_w_down, final_norm):
    hd = HEAD_DIM
    a_grp = A_HEADS // A_KV_HEADS

    def even_w_in(w):
        sizes = [A_HEADS * hd, A_KV_HEADS * hd, A_KV_HEADS * hd, B_HEADS * hd, B_KV_HEADS * hd, B_KV_HEADS * hd]
        aq, ak, av, bq, bk, bv = jnp.split(w, list(np.cumsum(sizes)[:-1]), axis=-1)
        return jnp.concatenate([_pad_heads(aq, A_HEADS, hd), _pad_heads(ak, A_KV_HEADS, hd),
                                _pad_heads(bq, B_HEADS, hd), _pad_heads(bk, B_KV_HEADS, hd), av, bv],
                               axis=-1).astype(BF16)

    def odd_w_in(w):
        c = w[:, :MLA_Q_RANK + MLA_KV_RANK]
        kr = _pad_heads(w[:, MLA_Q_RANK + MLA_KV_RANK:], 1, MLA_ROPE_DIM, _MLA_ROPE_BASE)
        return jnp.concatenate([c, kr], axis=-1).astype(BF16)

    def q_up(w):
        return _pad_heads(w, MLA_HEADS, MLA_QK_DIM).astype(BF16)

    def kv_up(w):
        w = w.reshape(MLA_KV_RANK, MLA_HEADS, MLA_NOPE_DIM + MLA_V_DIM)
        wk = _pad_heads(w[:, :, :MLA_NOPE_DIM].reshape(MLA_KV_RANK, -1), MLA_HEADS, MLA_NOPE_DIM)
        wv = w[:, :, MLA_NOPE_DIM:].reshape(MLA_KV_RANK, -1)
        return wk.astype(BF16), wv.astype(BF16)

    slopes = jnp.asarray(2.0 ** (-8.0 * np.arange(1, A_HEADS + 1) / A_HEADS), dtype=F32)
    per_col = lambda v: jnp.repeat(v.astype(F32).reshape(A_KV_HEADS, a_grp), WIN_Q, axis=1)[:, None, :] * LOG2E
    router = jnp.pad(moe_router.astype(F32), ((0, 0), (0, 0), (0, LANES - N_EXPERTS)))
    router_hi = router.astype(BF16)
    kv = [kv_up(w) for w in mla_w_kv_up]
    return {
        "ev_norm_mix": ev_norm_mix, "ev_w_in": jnp.stack([even_w_in(w) for w in ev_w_in]),
        "a_slope": per_col(slopes), "a_sink": jnp.stack([per_col(s) for s in a_sink]),
        "b_q_norm": jnp.stack([_lane_gain(g) for g in b_q_norm]),
        "b_k_norm": jnp.stack([_lane_gain(g) for g in b_k_norm]),
        "ev_w_out_a": ev_w_out[:, :A_HEADS * hd].astype(BF16),
        "ev_w_out_b": ev_w_out[:, A_HEADS * hd:].astype(BF16),
        "ev_norm_ffn": ev_norm_ffn,
        "ffn_w_gate": ffn_w_gate.astype(BF16), "ffn_w_up": ffn_w_up.astype(BF16),
        "ffn_w_down": ffn_w_down.astype(BF16),
        "od_norm_mix": od_norm_mix, "od_w_in": jnp.stack([odd_w_in(w) for w in od_w_in]),
        "mla_q_norm": mla_q_norm, "mla_w_q_up": jnp.stack([q_up(w) for w in mla_w_q_up]),
        "mla_kv_norm": mla_kv_norm,
        "mla_w_k_up": jnp.stack([a for a, _ in kv]), "mla_w_v_up": jnp.stack([b for _, b in kv]),
        "od_w_out": od_w_out.astype(BF16), "od_norm_ffn": od_norm_ffn,
        "router_hi": router_hi, "router_lo": (router - router_hi.astype(F32)).astype(BF16),
        "zero_router": jnp.zeros((D_MODEL, LANES), BF16),
        "moe_w_gate": moe_w_gate.astype(BF16), "moe_w_up": moe_w_up.astype(BF16),
        "moe_w_down": moe_w_down.astype(BF16),
        "final_norm": final_norm[None, :],
    }


def kernel(x_prompt, x_sample, ev_norm_mix, ev_w_in, a_sink, b_q_norm, b_k_norm, ev_w_out, ev_norm_ffn, ffn_w_gate, ffn_w_up, ffn_w_down, od_norm_mix, od_w_in, mla_q_norm, mla_w_q_up, mla_kv_norm, mla_w_kv_up, od_w_out, od_norm_ffn, moe_router, moe_w_gate, moe_w_up, moe_w_down, final_norm):
    p = _prepare(ev_norm_mix, ev_w_in, a_sink, b_q_norm, b_k_norm, ev_w_out, ev_norm_ffn,
                 ffn_w_gate, ffn_w_up, ffn_w_down, od_norm_mix, od_w_in, mla_q_norm, mla_w_q_up,
                 mla_kv_norm, mla_w_kv_up, od_w_out, od_norm_ffn, moe_router, moe_w_gate, moe_w_up,
                 moe_w_down, final_norm)
    return (_trunk(x_prompt, p), _trunk(x_sample, p))
```

```python
import functools
import math

import numpy as np
import jax
import jax.numpy as jnp
from jax import lax
from jax.experimental import pallas as pl
from jax.experimental.pallas import tpu as pltpu

F32 = jnp.float32
BF16 = jnp.bfloat16

D_MODEL = 1024
HEAD_DIM = 64
WINDOW = 128
GRID_W = 64
ROPE_THETA = 10000.0
NORM_EPS = 1e-6
A_HEADS, A_KV_HEADS = 8, 2
B_HEADS, B_KV_HEADS = 8, 2
MLA_HEADS = 8
MLA_Q_RANK, MLA_KV_RANK = 256, 128
MLA_NOPE_DIM, MLA_ROPE_DIM, MLA_V_DIM = 64, 32, 64
MLA_QK_DIM = MLA_NOPE_DIM + MLA_ROPE_DIM
N_EXPERTS = 8
LANES = 128
LOG2E = math.log2(math.e)
VMEM_LIMIT = 48 * 1024 * 1024

TOK_TILE = 512
FFN_TOK_TILE = 1024
FFN_F_TILE = 512
KEY_CHUNK = 512
WIN_Q = 128


def _rms(x, g):
    return x * lax.rsqrt(jnp.mean(x * x, axis=-1, keepdims=True) + NORM_EPS) * g


def _swap_halves(x, half):
    lane = lax.broadcasted_iota(jnp.int32, x.shape, 1)
    return jnp.where(lane < half, pltpu.roll(x, LANES - half, 1), pltpu.roll(x, half, 1))


def _swap_halves_at(x, base, half):
    lane = lax.broadcasted_iota(jnp.int32, x.shape, 1)
    return jnp.where(lane < base + half, pltpu.roll(x, LANES - half, 1), pltpu.roll(x, half, 1))


def _params(sem):
    return pltpu.CompilerParams(dimension_semantics=sem, vmem_limit_bytes=VMEM_LIMIT)


_EV_AQ, _EV_AK, _EV_BQ, _EV_BK, _EV_V, _EV_END = 0, 1024, 1280, 2304, 2560, 2816


def _even_in_kernel(x_ref, g_ref, w_ref, cos_ref, sin_ref, qg_ref, kg_ref,
                    aq_ref, ak_ref, avt_ref, bq_ref, bk_ref, bvt_ref):
    xb = _rms(x_ref[...], g_ref[...]).astype(BF16)
    cos = cos_ref[...]
    sin = sin_ref[...]
    qscale = (HEAD_DIM ** -0.5) * LOG2E

    def norm_rope(blk, gain):
        ms = jnp.sum(blk * blk, axis=-1, keepdims=True) * (1.0 / HEAD_DIM)
        y = blk * lax.rsqrt(ms + NORM_EPS) * gain
        return y * cos + _swap_halves(y, HEAD_DIM // 2) * sin

    pa = jnp.dot(xb, w_ref[:, _EV_AQ:_EV_AK], preferred_element_type=F32)
    aq_ref[...] = (pa * qscale).astype(BF16)
    ak_ref[...] = jnp.dot(xb, w_ref[:, _EV_AK:_EV_BQ], preferred_element_type=F32).astype(BF16)
    pbq = jnp.dot(xb, w_ref[:, _EV_BQ:_EV_BK], preferred_element_type=F32)
    for h in range(B_HEADS):
        sl = slice(h * LANES, (h + 1) * LANES)
        bq_ref[:, sl] = (norm_rope(pbq[:, sl], qg_ref[...]) * qscale).astype(BF16)
    pbk = jnp.dot(xb, w_ref[:, _EV_BK:_EV_V], preferred_element_type=F32)
    for h in range(B_KV_HEADS):
        sl = slice(h * LANES, (h + 1) * LANES)
        bk_ref[:, sl] = norm_rope(pbk[:, sl], kg_ref[...]).astype(BF16)
    pv = jnp.dot(xb, w_ref[:, _EV_V:_EV_END], preferred_element_type=F32)
    avt_ref[0] = pv[:, :LANES].T.astype(BF16)
    bvt_ref[0] = pv[:, LANES:].T.astype(BF16)


def _even_in(x2, bsz, seq, g, w_exp, cos_t, sin_t, qg, kg):
    tokens = x2.shape[0]
    tt = min(TOK_TILE, seq)
    assert seq % tt == 0
    ns = seq // tt
    row = lambda i: (i, 0)
    const = lambda i: (0, 0)
    tab = lambda i: (i % ns, 0)
    vt = lambda i: (i // ns, 0, i % ns)
    return pl.pallas_call(
        _even_in_kernel,
        grid=(tokens // tt,),
        in_specs=[
            pl.BlockSpec((tt, D_MODEL), row),
            pl.BlockSpec((1, D_MODEL), const),
            pl.BlockSpec((D_MODEL, _EV_END), const),
            pl.BlockSpec((tt, LANES), tab),
            pl.BlockSpec((tt, LANES), tab),
            pl.BlockSpec((1, LANES), const),
            pl.BlockSpec((1, LANES), const),
        ],
        out_specs=[
            pl.BlockSpec((tt, A_HEADS * LANES), row),
            pl.BlockSpec((tt, A_KV_HEADS * LANES), row),
            pl.BlockSpec((1, A_KV_HEADS * HEAD_DIM, tt), vt),
            pl.BlockSpec((tt, B_HEADS * LANES), row),
            pl.BlockSpec((tt, B_KV_HEADS * LANES), row),
            pl.BlockSpec((1, B_KV_HEADS * HEAD_DIM, tt), vt),
        ],
        out_shape=[
            jax.ShapeDtypeStruct((tokens, A_HEADS * LANES), BF16),
            jax.ShapeDtypeStruct((tokens, A_KV_HEADS * LANES), BF16),
            jax.ShapeDtypeStruct((bsz, A_KV_HEADS * HEAD_DIM, seq), BF16),
            jax.ShapeDtypeStruct((tokens, B_HEADS * LANES), BF16),
            jax.ShapeDtypeStruct((tokens, B_KV_HEADS * LANES), BF16),
            jax.ShapeDtypeStruct((bsz, B_KV_HEADS * HEAD_DIM, seq), BF16),
        ],
        compiler_params=_params(("arbitrary",)),
        name="even_in_proj",
    )(x2, g, w_exp, cos_t, sin_t, qg, kg)


_MLA_ROPE_BASE = MLA_NOPE_DIM


def _odd_in_kernel(x_ref, g_ref, w_ref, qn_ref, wq_ref, kvn_ref, wk_ref, wv_ref,
                   cos_ref, sin_ref, q_ref, k_ref, vt_ref):
    xb = _rms(x_ref[...], g_ref[...]).astype(BF16)
    cos = cos_ref[...]
    sin = sin_ref[...]
    qscale = (MLA_QK_DIM ** -0.5) * LOG2E

    def rope(blk):
        return blk * cos + _swap_halves_at(blk, _MLA_ROPE_BASE, MLA_ROPE_DIM // 2) * sin

    proj = jnp.dot(xb, w_ref[...], preferred_element_type=F32)
    cq = _rms(proj[:, :MLA_Q_RANK], qn_ref[...]).astype(BF16)
    ckv = _rms(proj[:, MLA_Q_RANK:MLA_Q_RANK + MLA_KV_RANK], kvn_ref[...]).astype(BF16)
    kr = rope(proj[:, MLA_Q_RANK + MLA_KV_RANK:])
    q = jnp.dot(cq, wq_ref[...], preferred_element_type=F32)
    kn = jnp.dot(ckv, wk_ref[...], preferred_element_type=F32)
    for h in range(MLA_HEADS):
        sl = slice(h * LANES, (h + 1) * LANES)
        q_ref[:, sl] = (rope(q[:, sl]) * qscale).astype(BF16)
        k_ref[:, sl] = (kn[:, sl] + kr).astype(BF16)
    v = jnp.dot(ckv, wv_ref[...], preferred_element_type=F32)
    for c in range(MLA_HEADS * MLA_V_DIM // LANES):
        vt_ref[0, c * LANES:(c + 1) * LANES, :] = v[:, c * LANES:(c + 1) * LANES].T.astype(BF16)


def _odd_in(x2, bsz, seq, g, w_exp, qn, wq_exp, kvn, wk_exp, wv, cos_t, sin_t):
    tokens = x2.shape[0]
    tt = min(TOK_TILE, seq)
    assert seq % tt == 0
    ns = seq // tt
    row = lambda i: (i, 0)
    const = lambda i: (0, 0)
    tab = lambda i: (i % ns, 0)
    vt = lambda i: (i // ns, 0, i % ns)
    hl = MLA_HEADS * LANES
    hv = MLA_HEADS * MLA_V_DIM
    return pl.pallas_call(
        _odd_in_kernel,
        grid=(tokens // tt,),
        in_specs=[
            pl.BlockSpec((tt, D_MODEL), row),
            pl.BlockSpec((1, D_MODEL), const),
            pl.BlockSpec((D_MODEL, 4 * LANES), const),
            pl.BlockSpec((1, MLA_Q_RANK), const),
            pl.BlockSpec((MLA_Q_RANK, hl), const),
            pl.BlockSpec((1, MLA_KV_RANK), const),
            pl.BlockSpec((MLA_KV_RANK, hl), const),
            pl.BlockSpec((MLA_KV_RANK, hv), const),
            pl.BlockSpec((tt, LANES), tab),
            pl.BlockSpec((tt, LANES), tab),
        ],
        out_specs=[
            pl.BlockSpec((tt, hl), row),
            pl.BlockSpec((tt, hl), row),
            pl.BlockSpec((1, hv, tt), vt),
        ],
        out_shape=[
            jax.ShapeDtypeStruct((tokens, hl), BF16),
            jax.ShapeDtypeStruct((tokens, hl), BF16),
            jax.ShapeDtypeStruct((bsz, hv, seq), BF16),
        ],
        compiler_params=_params(("arbitrary",)),
        name="odd_in_proj",
    )(x2, g, w_exp, qn, wq_exp, kvn, wk_exp, wv, cos_t, sin_t)


def _pair_transpose(pieces):
    outs = []
    for a in range(0, len(pieces), 2):
        outs.append(jnp.concatenate([pieces[a], pieces[a + 1]], axis=0).T)
    return outs[0] if len(outs) == 1 else jnp.concatenate(outs, axis=1)


SUM_ROWS = 16


def _dense_attn_kernel(q_ref, k_ref, vt_ref, o_ref, s_ref, cm_ref, m_ref, acc_ref, *, kb, grp, tq, tk, seq):
    n = grp * tq
    dv = HEAD_DIM
    n_chunks = seq // tk
    ones = jnp.ones((SUM_ROWS, tk), BF16)
    pieces = []
    for j in range(kb):
        qs = jnp.concatenate(
            [q_ref[:, (j * grp + g) * LANES:(j * grp + g + 1) * LANES] for g in range(grp)], axis=0)

        def scores(c, slot, j=j, qs=qs):
            off = pl.multiple_of(c * tk, tk)
            kc = k_ref[pl.ds(off, tk), j * LANES:(j + 1) * LANES]
            s = lax.dot_general(kc, qs, (((1,), (1,)), ((), ())),
                                preferred_element_type=F32)
            s_ref[slot] = s
            cm_ref[slot] = jnp.max(s, axis=0, keepdims=True)

        def softmax_pv(c, slot, j=j):
            off = pl.multiple_of(c * tk, tk)
            m_old = m_ref[...]
            m_new = jnp.maximum(m_old, cm_ref[slot])
            alpha = jnp.exp2(m_old - m_new)
            p = jnp.exp2((s_ref[slot] - m_new).astype(BF16))
            vc = jnp.concatenate([vt_ref[0, j * dv:(j + 1) * dv, pl.ds(off, tk)], ones], axis=0)
            pv = jnp.dot(vc, p, preferred_element_type=F32)
            acc_ref[...] = alpha * acc_ref[...] + pv
            m_ref[...] = m_new

        m_ref[...] = jnp.full((1, n), -jnp.inf, F32)
        acc_ref[...] = jnp.zeros((dv + SUM_ROWS, n), F32)
        scores(0, 0)

        def body(c2, carry, scores=scores, softmax_pv=softmax_pv):
            c = 2 * c2
            scores(c + 1, 1)
            softmax_pv(c, 0)
            scores(c + 2, 0)
            softmax_pv(c + 1, 1)
            return carry

        lax.fori_loop(0, n_chunks // 2 - 1, body, 0)
        scores(n_chunks - 1, 1)
        softmax_pv(n_chunks - 2, 0)
        softmax_pv(n_chunks - 1, 1)
        acc = acc_ref[...]
        o = acc[:dv] / acc[dv:dv + 1]
        for g in range(grp):
            pieces.append(o[:, g * tq:(g + 1) * tq])
    o_ref[...] = _pair_transpose(pieces).astype(o_ref.dtype)


def _dense_attn(q, k, vt, bsz, seq, n_kv, grp, kb, tq):
    tokens = q.shape[0]
    tq = min(tq, seq)
    tk = min(KEY_CHUNK, seq // 2)
    n_chunks = seq // tk
    assert seq % tq == 0 and seq % tk == 0 and n_chunks % 2 == 0
    assert n_kv % kb == 0 and (kb * grp) % 2 == 0
    nq = seq // tq
    n = grp * tq
    kern = functools.partial(_dense_attn_kernel, kb=kb, grp=grp, tq=tq, tk=tk, seq=seq)
    return pl.pallas_call(
        kern,
        grid=(bsz, n_kv // kb, nq),
        in_specs=[
            pl.BlockSpec((tq, kb * grp * LANES), lambda b, h, i: (b * nq + i, h)),
            pl.BlockSpec((seq, kb * LANES), lambda b, h, i: (b, h)),
            pl.BlockSpec((1, kb * HEAD_DIM, seq), lambda b, h, i: (b, h, 0)),
        ],
        out_specs=pl.BlockSpec((tq, kb * grp * HEAD_DIM), lambda b, h, i: (b * nq + i, h)),
        out_shape=jax.ShapeDtypeStruct((tokens, n_kv * grp * HEAD_DIM), BF16),
        scratch_shapes=[
            pltpu.VMEM((2, tk, n), F32),
            pltpu.VMEM((2, 1, n), F32),
            pltpu.VMEM((1, n), F32),
            pltpu.VMEM((HEAD_DIM + SUM_ROWS, n), F32),
        ],
        compiler_params=_params(("arbitrary", "arbitrary", "arbitrary")),
        name="dense_attn",
    )(q, k, vt)


def _window_attn_kernel(q_ref, k_ref, vt_ref, slope_ref, sink_ref, o_ref, *, grp, seq):
    tq = WIN_Q
    span = 3 * WIN_Q
    n = grp * tq
    i = pl.program_id(2)
    w0 = pl.multiple_of(jnp.clip((i - 1) * tq, 0, seq - span), tq)
    qs = jnp.concatenate([q_ref[:, g * LANES:(g + 1) * LANES] for g in range(grp)], axis=0)
    kc = k_ref[pl.ds(w0, span), :]
    s = lax.dot_general(kc, qs, (((1,), (1,)), ((), ())), preferred_element_type=F32)
    kpos = w0 + lax.broadcasted_iota(jnp.int32, (span, n), 0)
    qpos = i * tq + lax.broadcasted_iota(jnp.int32, (span, n), 1) % tq
    dist = jnp.abs(kpos - qpos)
    s = s - slope_ref[0] * dist.astype(F32)
    s = jnp.where(dist <= WINDOW, s, -jnp.inf)
    sink = sink_ref[0]
    m = jnp.maximum(jnp.max(s, axis=0, keepdims=True), sink)
    e = jnp.exp2(s - m)
    denom = jnp.sum(e, axis=0, keepdims=True) + jnp.exp2(sink - m)
    vc = vt_ref[0, :, pl.ds(w0, span)]
    o = jnp.dot(vc, e.astype(BF16), preferred_element_type=F32) / denom
    o_ref[...] = _pair_transpose([o[:, g * tq:(g + 1) * tq] for g in range(grp)]).astype(o_ref.dtype)


def _window_attn(q, k, vt, slope_row, sink_row, bsz, seq, n_kv, grp):
    tokens = q.shape[0]
    tq = WIN_Q
    assert seq % tq == 0 and seq >= 3 * tq
    nq = seq // tq
    n = grp * tq
    kern = functools.partial(_window_attn_kernel, grp=grp, seq=seq)
    return pl.pallas_call(
        kern,
        grid=(bsz, n_kv, nq),
        in_specs=[
            pl.BlockSpec((tq, grp * LANES), lambda b, h, i: (b * nq + i, h)),
            pl.BlockSpec((seq, LANES), lambda b, h, i: (b, h)),
            pl.BlockSpec((1, HEAD_DIM, seq), lambda b, h, i: (b, h, 0)),
            pl.BlockSpec((1, 1, n), lambda b, h, i: (h, 0, 0)),
            pl.BlockSpec((1, 1, n), lambda b, h, i: (h, 0, 0)),
        ],
        out_specs=pl.BlockSpec((tq, grp * HEAD_DIM), lambda b, h, i: (b * nq + i, h)),
        out_shape=jax.ShapeDtypeStruct((tokens, n_kv * grp * HEAD_DIM), BF16),
        compiler_params=_params(("arbitrary", "arbitrary", "arbitrary")),
        name="window_attn",
    )(q, k, vt, slope_row, sink_row)


def _out_proj_kernel(*refs, n_in):
    x_ref = refs[0]
    y_refs = refs[1:1 + n_in]
    w_refs = refs[1 + n_in:1 + 2 * n_in]
    o_ref = refs[1 + 2 * n_in]
    acc = x_ref[...]
    for y_ref, w_ref in zip(y_refs, w_refs):
        acc = acc + jnp.dot(y_ref[...], w_ref[...], preferred_element_type=F32)
    o_ref[...] = acc


def _out_proj(x2, ys, ws):
    tokens = x2.shape[0]
    tt = min(TOK_TILE, tokens)
    row = lambda i: (i, 0)
    const = lambda i: (0, 0)
    kern = functools.partial(_out_proj_kernel, n_in=len(ys))
    return pl.pallas_call(
        kern,
        grid=(tokens // tt,),
        in_specs=[pl.BlockSpec((tt, D_MODEL), row)]
        + [pl.BlockSpec((tt, y.shape[1]), row) for y in ys]
        + [pl.BlockSpec(w.shape, const) for w in ws],
        out_specs=pl.BlockSpec((tt, D_MODEL), row),
        out_shape=jax.ShapeDtypeStruct((tokens, D_MODEL), F32),
        compiler_params=_params(("arbitrary",)),
        name="out_proj",
    )(x2, *ys, *ws)


def _ffn_kernel(x_ref, g_ref, rh_ref, rl_ref, wg_ref, wu_ref, wd_ref, gf_ref, o_ref,
                xn_ref, gate_ref, acc_ref, *, n_exp, final_norm):
    e = pl.program_id(1)
    f = pl.program_id(2)
    first = jnp.logical_and(e == 0, f == 0)
    last = jnp.logical_and(e == pl.num_programs(1) - 1, f == pl.num_programs(2) - 1)
    tt = x_ref.shape[0]

    @pl.when(first)
    def _():
        xn = _rms(x_ref[...], g_ref[...])
        xh = xn.astype(BF16)
        xn_ref[...] = xh
        acc_ref[...] = jnp.zeros_like(acc_ref)
        if n_exp > 1:
            xl = (xn - xh.astype(F32)).astype(BF16)
            logits = (jnp.dot(xh, rh_ref[...], preferred_element_type=F32)
                      + jnp.dot(xh, rl_ref[...], preferred_element_type=F32)
                      + jnp.dot(xl, rh_ref[...], preferred_element_type=F32))
            lane = lax.broadcasted_iota(jnp.int32, (tt, LANES), 1).astype(F32)
            lg = jnp.where(lane < n_exp, logits, -jnp.inf)
            m1 = jnp.max(lg, axis=1, keepdims=True)
            i1 = jnp.min(jnp.where(lg == m1, lane, float(LANES)), axis=1, keepdims=True)
            lg2 = jnp.where(lane == i1, -jnp.inf, lg)
            m2 = jnp.max(lg2, axis=1, keepdims=True)
            i2 = jnp.min(jnp.where(lg2 == m2, lane, float(LANES)), axis=1, keepdims=True)
            e2 = jnp.exp(m2 - m1)
            w1 = 1.0 / (1.0 + e2)
            w2 = e2 * w1
            gate_ref[...] = jnp.where(lane == i1, w1, 0.0) + jnp.where(lane == i2, w2, 0.0)

    xb = xn_ref[...]
    hg = jnp.dot(xb, wg_ref[0], preferred_element_type=F32)
    hu = jnp.dot(xb, wu_ref[0], preferred_element_type=F32)
    h = hg * jax.nn.sigmoid(hg) * hu
    if n_exp > 1:
        lane = lax.broadcasted_iota(jnp.int32, (tt, LANES), 1)
        ge = jnp.sum(jnp.where(lane == e, gate_ref[...], 0.0), axis=1, keepdims=True)
        h = h * ge
    acc_ref[...] += jnp.dot(h.astype(BF16), wd_ref[0], preferred_element_type=F32)

    @pl.when(last)
    def _():
        y = x_ref[...] + acc_ref[...]
        if final_norm:
            y = _rms(y, gf_ref[...])
        o_ref[...] = y


def _ffn(x2, g, r_hi, r_lo, wg, wu, wd, g_final, final_norm):
    tokens = x2.shape[0]
    n_exp, _, ff = wg.shape
    tt = min(FFN_TOK_TILE, tokens)
    fc = FFN_F_TILE
    assert tokens % tt == 0 and ff % fc == 0
    kern = functools.partial(_ffn_kernel, n_exp=n_exp, final_norm=final_norm)
    row = lambda i, e, f: (i, 0)
    const = lambda i, e, f: (0, 0)
    return pl.pallas_call(
        kern,
        grid=(tokens // tt, n_exp, ff // fc),
        in_specs=[
            pl.BlockSpec((tt, D_MODEL), row),
            pl.BlockSpec((1, D_MODEL), const),
            pl.BlockSpec((D_MODEL, LANES), const),
            pl.BlockSpec((D_MODEL, LANES), const),
            pl.BlockSpec((1, D_MODEL, fc), lambda i, e, f: (e, 0, f)),
            pl.BlockSpec((1, D_MODEL, fc), lambda i, e, f: (e, 0, f)),
            pl.BlockSpec((1, fc, D_MODEL), lambda i, e, f: (e, f, 0)),
            pl.BlockSpec((1, D_MODEL), const),
        ],
        out_specs=pl.BlockSpec((tt, D_MODEL), row),
        out_shape=jax.ShapeDtypeStruct((tokens, D_MODEL), F32),
        scratch_shapes=[
            pltpu.VMEM((tt, D_MODEL), BF16),
            pltpu.VMEM((tt, LANES), F32),
            pltpu.VMEM((tt, D_MODEL), F32),
        ],
        compiler_params=_params(("arbitrary", "arbitrary", "arbitrary")),
        name="ffn",
    )(x2, g, r_hi, r_lo, wg, wu, wd, g_final)


def _pad_heads(w, n_heads, width, offset=0):
    r = w.shape[0]
    w = w.reshape(r, n_heads, width)
    w = jnp.pad(w, ((0, 0), (0, 0), (offset, LANES - width - offset)))
    return w.reshape(r, n_heads * LANES)


def _rope_tables(ang, base):
    half = ang.shape[1]
    cos = jnp.cos(ang)
    sin = jnp.sin(ang)
    pad = ((0, 0), (base, LANES - base - 2 * half))
    cos_t = jnp.pad(jnp.concatenate([cos, cos], axis=1) - 1.0, pad) + 1.0
    sin_t = jnp.pad(jnp.concatenate([-sin, sin], axis=1), pad)
    return cos_t, sin_t


def _rope_angles(pos, dim):
    inv = ROPE_THETA ** (-jnp.arange(0, dim, 2, dtype=F32) / dim)
    return pos.astype(F32)[:, None] * inv[None, :]


def _lane_gain(g):
    return jnp.pad(g.astype(F32), (0, LANES - g.shape[0]))[None, :]


def _trunk(x, p):
    bsz, seq, _ = x.shape
    x2 = x.reshape(bsz * seq, D_MODEL)
    n_rows = seq // GRID_W
    row = jnp.repeat(jnp.arange(n_rows), GRID_W)
    col = jnp.tile(jnp.arange(GRID_W), n_rows)
    ang_axial = jnp.concatenate([_rope_angles(row, HEAD_DIM // 2), _rope_angles(col, HEAD_DIM // 2)], axis=-1)
    cos_ax, sin_ax = _rope_tables(ang_axial, 0)
    cos_1d, sin_1d = _rope_tables(_rope_angles(jnp.arange(seq), MLA_ROPE_DIM), _MLA_ROPE_BASE)
    depth = p["ev_norm_mix"].shape[0] + p["od_norm_mix"].shape[0]
    a_grp = A_HEADS // A_KV_HEADS
    for layer in range(depth):
        i = layer // 2
        last_layer = layer == depth - 1
        if layer % 2 == 0:
            aq, ak, avt, bq, bk, bvt = _even_in(
                x2, bsz, seq, p["ev_norm_mix"][i][None, :], p["ev_w_in"][i], cos_ax, sin_ax,
                p["b_q_norm"][i], p["b_k_norm"][i])
            ya = _window_attn(aq, ak, avt, p["a_slope"], p["a_sink"][i], bsz, seq, A_KV_HEADS, a_grp)
            yb = _dense_attn(bq, bk, bvt, bsz, seq, B_KV_HEADS, B_HEADS // B_KV_HEADS, 1, 256)
            x2 = _out_proj(x2, [ya, yb], [p["ev_w_out_a"][i], p["ev_w_out_b"][i]])
            x2 = _ffn(x2, p["ev_norm_ffn"][i][None, :], p["zero_router"], p["zero_router"],
                      p["ffn_w_gate"][i][None], p["ffn_w_up"][i][None], p["ffn_w_down"][i][None],
                      p["final_norm"], last_layer)
        else:
            q, k, vt = _odd_in(
                x2, bsz, seq, p["od_norm_mix"][i][None, :], p["od_w_in"][i], p["mla_q_norm"][i][None, :],
                p["mla_w_q_up"][i], p["mla_kv_norm"][i][None, :], p["mla_w_k_up"][i], p["mla_w_v_up"][i],
                cos_1d, sin_1d)
            yc = _dense_attn(q, k, vt, bsz, seq, MLA_HEADS, 1, 2, 1024)
            x2 = _out_proj(x2, [yc], [p["od_w_out"][i]])
            x2 = _ffn(x2, p["od_norm_ffn"][i][None, :], p["router_hi"][i], p["router_lo"][i],
                      p["moe_w_gate"][i], p["moe_w_up"][i], p["moe_w_down"][i],
                      p["final_norm"], last_layer)
    return x2.reshape(bsz, seq, D_MODEL)


def _prepare(ev_norm_mix, ev_w_in, a_sink, b_q_norm, b_k_norm, ev_w_out, ev_norm_ffn,
             ffn_w_gate, ffn_w_up, ffn_w_down, od_norm_mix, od_w_in, mla_q_norm, mla_w_q_up,
             mla_kv_norm, mla_w_kv_up, od_w_out, od_norm_ffn, moe_router, moe_w_gate, moe_w_up,
             moe_w_down, final_norm):
    hd = HEAD_DIM
    a_grp = A_HEADS // A_KV_HEADS

    def even_w_in(w):
        sizes = [A_HEADS * hd, A_KV_HEADS * hd, A_KV_HEADS * hd, B_HEADS * hd, B_KV_HEADS * hd, B_KV_HEADS * hd]
        aq, ak, av, bq, bk, bv = jnp.split(w, list(np.cumsum(sizes)[:-1]), axis=-1)
        return jnp.concatenate([_pad_heads(aq, A_HEADS, hd), _pad_heads(ak, A_KV_HEADS, hd),
                                _pad_heads(bq, B_HEADS, hd), _pad_heads(bk, B_KV_HEADS, hd), av, bv],
                               axis=-1).astype(BF16)

    def odd_w_in(w):
        c = w[:, :MLA_Q_RANK + MLA_KV_RANK]
        kr = _pad_heads(w[:, MLA_Q_RANK + MLA_KV_RANK:], 1, MLA_ROPE_DIM, _MLA_ROPE_BASE)
        return jnp.concatenate([c, kr], axis=-1).astype(BF16)

    def q_up(w):
        return _pad_heads(w, MLA_HEADS, MLA_QK_DIM).astype(BF16)

    def kv_up(w):
        w = w.reshape(MLA_KV_RANK, MLA_HEADS, MLA_NOPE_DIM + MLA_V_DIM)
        wk = _pad_heads(w[:, :, :MLA_NOPE_DIM].reshape(MLA_KV_RANK, -1), MLA_HEADS, MLA_NOPE_DIM)
        wv = w[:, :, MLA_NOPE_DIM:].reshape(MLA_KV_RANK, -1)
        return wk.astype(BF16), wv.astype(BF16)

    slopes = jnp.asarray(2.0 ** (-8.0 * np.arange(1, A_HEADS + 1) / A_HEADS), dtype=F32)
    per_col = lambda v: jnp.repeat(v.astype(F32).reshape(A_KV_HEADS, a_grp), WIN_Q, axis=1)[:, None, :] * LOG2E
    router = jnp.pad(moe_router.astype(F32), ((0, 0), (0, 0), (0, LANES - N_EXPERTS)))
    router_hi = router.astype(BF16)
    kv = [kv_up(w) for w in mla_w_kv_up]
    return {
        "ev_norm_mix": ev_norm_mix, "ev_w_in": jnp.stack([even_w_in(w) for w in ev_w_in]),
        "a_slope": per_col(slopes), "a_sink": jnp.stack([per_col(s) for s in a_sink]),
        "b_q_norm": jnp.stack([_lane_gain(g) for g in b_q_norm]),
        "b_k_norm": jnp.stack([_lane_gain(g) for g in b_k_norm]),
        "ev_w_out_a": ev_w_out[:, :A_HEADS * hd].astype(BF16),
        "ev_w_out_b": ev_w_out[:, A_HEADS * hd:].astype(BF16),
        "ev_norm_ffn": ev_norm_ffn,
        "ffn_w_gate": ffn_w_gate.astype(BF16), "ffn_w_up": ffn_w_up.astype(BF16),
        "ffn_w_down": ffn_w_down.astype(BF16),
        "od_norm_mix": od_norm_mix, "od_w_in": jnp.stack([odd_w_in(w) for w in od_w_in]),
        "mla_q_norm": mla_q_norm, "mla_w_q_up": jnp.stack([q_up(w) for w in mla_w_q_up]),
        "mla_kv_norm": mla_kv_norm,
        "mla_w_k_up": jnp.stack([a for a, _ in kv]), "mla_w_v_up": jnp.stack([b for _, b in kv]),
        "od_w_out": od_w_out.astype(BF16), "od_norm_ffn": od_norm_ffn,
        "router_hi": router_hi, "router_lo": (router - router_hi.astype(F32)).astype(BF16),
        "zero_router": jnp.zeros((D_MODEL, LANES), BF16),
        "moe_w_gate": moe_w_gate.astype(BF16), "moe_w_up": moe_w_up.astype(BF16),
        "moe_w_down": moe_w_down.astype(BF16),
        "final_norm": final_norm[None, :],
    }


def kernel(x_prompt, x_sample, ev_norm_mix, ev_w_in, a_sink, b_q_norm, b_k_norm, ev_w_out, ev_norm_ffn, ffn_w_gate, ffn_w_up, ffn_w_down, od_norm_mix, od_w_in, mla_q_norm, mla_w_q_up, mla_kv_norm, mla_w_kv_up, od_w_out, od_norm_ffn, moe_router, moe_w_gate, moe_w_up, moe_w_down, final_norm):
    p = _prepare(ev_norm_mix, ev_w_in, a_sink, b_q_norm, b_k_norm, ev_w_out, ev_norm_ffn,
                 ffn_w_gate, ffn_w_up, ffn_w_down, od_norm_mix, od_w_in, mla_q_norm, mla_w_q_up,
                 mla_kv_norm, mla_w_kv_up, od_w_out, od_norm_ffn, moe_router, moe_w_gate, moe_w_up,
                 moe_w_down, final_norm)
    return (_trunk(x_prompt, p), _trunk(x_sample, p))
```

```python
import functools
import math

import numpy as np
import jax
import jax.numpy as jnp
from jax import lax
from jax.experimental import pallas as pl
from jax.experimental.pallas import tpu as pltpu

F32 = jnp.float32
BF16 = jnp.bfloat16

D_MODEL = 1024
HEAD_DIM = 64
WINDOW = 128
GRID_W = 64
ROPE_THETA = 10000.0
NORM_EPS = 1e-6
A_HEADS, A_KV_HEADS = 8, 2
B_HEADS, B_KV_HEADS = 8, 2
MLA_HEADS = 8
MLA_Q_RANK, MLA_KV_RANK = 256, 128
MLA_NOPE_DIM, MLA_ROPE_DIM, MLA_V_DIM = 64, 32, 64
MLA_QK_DIM = MLA_NOPE_DIM + MLA_ROPE_DIM
N_EXPERTS = 8
TOP_K = 2
LANES = 128
LOG2E = math.log2(math.e)
VMEM_LIMIT = 48 * 1024 * 1024

TOK_TILE = 512
FFN_TOK_TILE = 1024
FFN_F_TILE = 512
KEY_CHUNK = 512
WIN_Q = 128
MOE_ROW_TILE = 512
GATHER_ROWS = 2048
GATHER_UNROLL = 8


def _rms(x, g):
    return x * lax.rsqrt(jnp.mean(x * x, axis=-1, keepdims=True) + NORM_EPS) * g


def _swap_halves(x, half):
    lane = lax.broadcasted_iota(jnp.int32, x.shape, 1)
    return jnp.where(lane < half, pltpu.roll(x, LANES - half, 1), pltpu.roll(x, half, 1))


def _swap_halves_at(x, base, half):
    lane = lax.broadcasted_iota(jnp.int32, x.shape, 1)
    return jnp.where(lane < base + half, pltpu.roll(x, LANES - half, 1), pltpu.roll(x, half, 1))


def _params(sem):
    return pltpu.CompilerParams(dimension_semantics=sem, vmem_limit_bytes=VMEM_LIMIT)


_EV_AQ, _EV_AK, _EV_BQ, _EV_BK, _EV_V, _EV_END = 0, 1024, 1280, 2304, 2560, 2816


def _even_in_kernel(x_ref, g_ref, w_ref, cos_ref, sin_ref, qg_ref, kg_ref,
                    aq_ref, ak_ref, avt_ref, bq_ref, bk_ref, bvt_ref):
    xb = _rms(x_ref[...], g_ref[...]).astype(BF16)
    cos = cos_ref[...]
    sin = sin_ref[...]
    qscale = (HEAD_DIM ** -0.5) * LOG2E

    def norm_rope(blk, gain):
        ms = jnp.sum(blk * blk, axis=-1, keepdims=True) * (1.0 / HEAD_DIM)
        y = blk * lax.rsqrt(ms + NORM_EPS) * gain
        return y * cos + _swap_halves(y, HEAD_DIM // 2) * sin

    pa = jnp.dot(xb, w_ref[:, _EV_AQ:_EV_AK], preferred_element_type=F32)
    aq_ref[...] = (pa * qscale).astype(BF16)
    ak_ref[...] = jnp.dot(xb, w_ref[:, _EV_AK:_EV_BQ], preferred_element_type=F32).astype(BF16)
    pbq = jnp.dot(xb, w_ref[:, _EV_BQ:_EV_BK], preferred_element_type=F32)
    for h in range(B_HEADS):
        sl = slice(h * LANES, (h + 1) * LANES)
        bq_ref[:, sl] = (norm_rope(pbq[:, sl], qg_ref[...]) * qscale).astype(BF16)
    pbk = jnp.dot(xb, w_ref[:, _EV_BK:_EV_V], preferred_element_type=F32)
    for h in range(B_KV_HEADS):
        sl = slice(h * LANES, (h + 1) * LANES)
        bk_ref[:, sl] = norm_rope(pbk[:, sl], kg_ref[...]).astype(BF16)
    pv = jnp.dot(xb, w_ref[:, _EV_V:_EV_END], preferred_element_type=F32)
    avt_ref[0] = pv[:, :LANES].T.astype(BF16)
    bvt_ref[0] = pv[:, LANES:].T.astype(BF16)


def _even_in(x2, bsz, seq, g, w_exp, cos_t, sin_t, qg, kg):
    tokens = x2.shape[0]
    tt = min(TOK_TILE, seq)
    assert seq % tt == 0
    ns = seq // tt
    row = lambda i: (i, 0)
    const = lambda i: (0, 0)
    tab = lambda i: (i % ns, 0)
    vt = lambda i: (i // ns, 0, i % ns)
    return pl.pallas_call(
        _even_in_kernel,
        grid=(tokens // tt,),
        in_specs=[
            pl.BlockSpec((tt, D_MODEL), row),
            pl.BlockSpec((1, D_MODEL), const),
            pl.BlockSpec((D_MODEL, _EV_END), const),
            pl.BlockSpec((tt, LANES), tab),
            pl.BlockSpec((tt, LANES), tab),
            pl.BlockSpec((1, LANES), const),
            pl.BlockSpec((1, LANES), const),
        ],
        out_specs=[
            pl.BlockSpec((tt, A_HEADS * LANES), row),
            pl.BlockSpec((tt, A_KV_HEADS * LANES), row),
            pl.BlockSpec((1, A_KV_HEADS * HEAD_DIM, tt), vt),
            pl.BlockSpec((tt, B_HEADS * LANES), row),
            pl.BlockSpec((tt, B_KV_HEADS * LANES), row),
            pl.BlockSpec((1, B_KV_HEADS * HEAD_DIM, tt), vt),
        ],
        out_shape=[
            jax.ShapeDtypeStruct((tokens, A_HEADS * LANES), BF16),
            jax.ShapeDtypeStruct((tokens, A_KV_HEADS * LANES), BF16),
            jax.ShapeDtypeStruct((bsz, A_KV_HEADS * HEAD_DIM, seq), BF16),
            jax.ShapeDtypeStruct((tokens, B_HEADS * LANES), BF16),
            jax.ShapeDtypeStruct((tokens, B_KV_HEADS * LANES), BF16),
            jax.ShapeDtypeStruct((bsz, B_KV_HEADS * HEAD_DIM, seq), BF16),
        ],
        compiler_params=_params(("arbitrary",)),
        name="even_in_proj",
    )(x2, g, w_exp, cos_t, sin_t, qg, kg)


_MLA_ROPE_BASE = MLA_NOPE_DIM


def _odd_in_kernel(x_ref, g_ref, w_ref, qn_ref, wq_ref, kvn_ref, wk_ref, wv_ref,
                   cos_ref, sin_ref, q_ref, k_ref, vt_ref):
    xb = _rms(x_ref[...], g_ref[...]).astype(BF16)
    cos = cos_ref[...]
    sin = sin_ref[...]
    qscale = (MLA_QK_DIM ** -0.5) * LOG2E

    def rope(blk):
        return blk * cos + _swap_halves_at(blk, _MLA_ROPE_BASE, MLA_ROPE_DIM // 2) * sin

    proj = jnp.dot(xb, w_ref[...], preferred_element_type=F32)
    cq = _rms(proj[:, :MLA_Q_RANK], qn_ref[...]).astype(BF16)
    ckv = _rms(proj[:, MLA_Q_RANK:MLA_Q_RANK + MLA_KV_RANK], kvn_ref[...]).astype(BF16)
    kr = rope(proj[:, MLA_Q_RANK + MLA_KV_RANK:])
    q = jnp.dot(cq, wq_ref[...], preferred_element_type=F32)
    kn = jnp.dot(ckv, wk_ref[...], preferred_element_type=F32)
    for h in range(MLA_HEADS):
        sl = slice(h * LANES, (h + 1) * LANES)
        q_ref[:, sl] = (rope(q[:, sl]) * qscale).astype(BF16)
        k_ref[:, sl] = (kn[:, sl] + kr).astype(BF16)
    v = jnp.dot(ckv, wv_ref[...], preferred_element_type=F32)
    for c in range(MLA_HEADS * MLA_V_DIM // LANES):
        vt_ref[0, c * LANES:(c + 1) * LANES, :] = v[:, c * LANES:(c + 1) * LANES].T.astype(BF16)


def _odd_in(x2, bsz, seq, g, w_exp, qn, wq_exp, kvn, wk_exp, wv, cos_t, sin_t):
    tokens = x2.shape[0]
    tt = min(TOK_TILE, seq)
    assert seq % tt == 0
    ns = seq // tt
    row = lambda i: (i, 0)
    const = lambda i: (0, 0)
    tab = lambda i: (i % ns, 0)
    vt = lambda i: (i // ns, 0, i % ns)
    hl = MLA_HEADS * LANES
    hv = MLA_HEADS * MLA_V_DIM
    return pl.pallas_call(
        _odd_in_kernel,
        grid=(tokens // tt,),
        in_specs=[
            pl.BlockSpec((tt, D_MODEL), row),
            pl.BlockSpec((1, D_MODEL), const),
            pl.BlockSpec((D_MODEL, 4 * LANES), const),
            pl.BlockSpec((1, MLA_Q_RANK), const),
            pl.BlockSpec((MLA_Q_RANK, hl), const),
            pl.BlockSpec((1, MLA_KV_RANK), const),
            pl.BlockSpec((MLA_KV_RANK, hl), const),
            pl.BlockSpec((MLA_KV_RANK, hv), const),
            pl.BlockSpec((tt, LANES), tab),
            pl.BlockSpec((tt, LANES), tab),
        ],
        out_specs=[
            pl.BlockSpec((tt, hl), row),
            pl.BlockSpec((tt, hl), row),
            pl.BlockSpec((1, hv, tt), vt),
        ],
        out_shape=[
            jax.ShapeDtypeStruct((tokens, hl), BF16),
            jax.ShapeDtypeStruct((tokens, hl), BF16),
            jax.ShapeDtypeStruct((bsz, hv, seq), BF16),
        ],
        compiler_params=_params(("arbitrary",)),
        name="odd_in_proj",
    )(x2, g, w_exp, qn, wq_exp, kvn, wk_exp, wv, cos_t, sin_t)


def _pair_transpose(pieces):
    outs = []
    for a in range(0, len(pieces), 2):
        outs.append(jnp.concatenate([pieces[a], pieces[a + 1]], axis=0).T)
    return outs[0] if len(outs) == 1 else jnp.concatenate(outs, axis=1)


SUM_ROWS = 16


def _dense_attn_kernel(q_ref, k_ref, vt_ref, o_ref, s_ref, cm_ref, m_ref, acc_ref, *, kb, grp, tq, tk, seq):
    n = grp * tq
    dv = HEAD_DIM
    n_chunks = seq // tk
    ones = jnp.ones((SUM_ROWS, tk), BF16)
    pieces = []
    for j in range(kb):
        qs = jnp.concatenate(
            [q_ref[:, (j * grp + g) * LANES:(j * grp + g + 1) * LANES] for g in range(grp)], axis=0)

        def scores(c, slot, j=j, qs=qs):
            off = pl.multiple_of(c * tk, tk)
            kc = k_ref[pl.ds(off, tk), j * LANES:(j + 1) * LANES]
            s = lax.dot_general(kc, qs, (((1,), (1,)), ((), ())),
                                preferred_element_type=F32)
            s_ref[slot] = s
            cm_ref[slot] = jnp.max(s, axis=0, keepdims=True)

        def softmax_pv(c, slot, j=j):
            off = pl.multiple_of(c * tk, tk)
            m_old = m_ref[...]
            m_new = jnp.maximum(m_old, cm_ref[slot])
            alpha = jnp.exp2(m_old - m_new)
            p = jnp.exp2((s_ref[slot] - m_new).astype(BF16))
            vc = jnp.concatenate([vt_ref[0, j * dv:(j + 1) * dv, pl.ds(off, tk)], ones], axis=0)
            pv = jnp.dot(vc, p, preferred_element_type=F32)
            acc_ref[...] = alpha * acc_ref[...] + pv
            m_ref[...] = m_new

        m_ref[...] = jnp.full((1, n), -jnp.inf, F32)
        acc_ref[...] = jnp.zeros((dv + SUM_ROWS, n), F32)
        scores(0, 0)

        def body(c2, carry, scores=scores, softmax_pv=softmax_pv):
            c = 2 * c2
            scores(c + 1, 1)
            softmax_pv(c, 0)
            scores(c + 2, 0)
            softmax_pv(c + 1, 1)
            return carry

        lax.fori_loop(0, n_chunks // 2 - 1, body, 0)
        scores(n_chunks - 1, 1)
        softmax_pv(n_chunks - 2, 0)
        softmax_pv(n_chunks - 1, 1)
        acc = acc_ref[...]
        o = acc[:dv] / acc[dv:dv + 1]
        for g in range(grp):
            pieces.append(o[:, g * tq:(g + 1) * tq])
    o_ref[...] = _pair_transpose(pieces).astype(o_ref.dtype)


def _dense_attn(q, k, vt, bsz, seq, n_kv, grp, kb, tq):
    tokens = q.shape[0]
    tq = min(tq, seq)
    tk = min(KEY_CHUNK, seq // 2)
    n_chunks = seq // tk
    assert seq % tq == 0 and seq % tk == 0 and n_chunks % 2 == 0
    assert n_kv % kb == 0 and (kb * grp) % 2 == 0
    nq = seq // tq
    n = grp * tq
    kern = functools.partial(_dense_attn_kernel, kb=kb, grp=grp, tq=tq, tk=tk, seq=seq)
    return pl.pallas_call(
        kern,
        grid=(bsz, n_kv // kb, nq),
        in_specs=[
            pl.BlockSpec((tq, kb * grp * LANES), lambda b, h, i: (b * nq + i, h)),
            pl.BlockSpec((seq, kb * LANES), lambda b, h, i: (b, h)),
            pl.BlockSpec((1, kb * HEAD_DIM, seq), lambda b, h, i: (b, h, 0)),
        ],
        out_specs=pl.BlockSpec((tq, kb * grp * HEAD_DIM), lambda b, h, i: (b * nq + i, h)),
        out_shape=jax.ShapeDtypeStruct((tokens, n_kv * grp * HEAD_DIM), BF16),
        scratch_shapes=[
            pltpu.VMEM((2, tk, n), F32),
            pltpu.VMEM((2, 1, n), F32),
            pltpu.VMEM((1, n), F32),
            pltpu.VMEM((HEAD_DIM + SUM_ROWS, n), F32),
        ],
        compiler_params=_params(("arbitrary", "arbitrary", "arbitrary")),
        name="dense_attn",
    )(q, k, vt)


def _window_attn_kernel(q_ref, k_ref, vt_ref, slope_ref, sink_ref, o_ref, *, grp, seq):
    tq = WIN_Q
    span = 3 * WIN_Q
    n = grp * tq
    i = pl.program_id(2)
    w0 = pl.multiple_of(jnp.clip((i - 1) * tq, 0, seq - span), tq)
    qs = jnp.concatenate([q_ref[:, g * LANES:(g + 1) * LANES] for g in range(grp)], axis=0)
    kc = k_ref[pl.ds(w0, span), :]
    s = lax.dot_general(kc, qs, (((1,), (1,)), ((), ())), preferred_element_type=F32)
    kpos = w0 + lax.broadcasted_iota(jnp.int32, (span, n), 0)
    qpos = i * tq + lax.broadcasted_iota(jnp.int32, (span, n), 1) % tq
    dist = jnp.abs(kpos - qpos)
    s = s - slope_ref[0] * dist.astype(F32)
    s = jnp.where(dist <= WINDOW, s, -jnp.inf)
    sink = sink_ref[0]
    m = jnp.maximum(jnp.max(s, axis=0, keepdims=True), sink)
    e = jnp.exp2(s - m)
    denom = jnp.sum(e, axis=0, keepdims=True) + jnp.exp2(sink - m)
    vc = vt_ref[0, :, pl.ds(w0, span)]
    o = jnp.dot(vc, e.astype(BF16), preferred_element_type=F32) / denom
    o_ref[...] = _pair_transpose([o[:, g * tq:(g + 1) * tq] for g in range(grp)]).astype(o_ref.dtype)


def _window_attn(q, k, vt, slope_row, sink_row, bsz, seq, n_kv, grp):
    tokens = q.shape[0]
    tq = WIN_Q
    assert seq % tq == 0 and seq >= 3 * tq
    nq = seq // tq
    n = grp * tq
    kern = functools.partial(_window_attn_kernel, grp=grp, seq=seq)
    return pl.pallas_call(
        kern,
        grid=(bsz, n_kv, nq),
        in_specs=[
            pl.BlockSpec((tq, grp * LANES), lambda b, h, i: (b * nq + i, h)),
            pl.BlockSpec((seq, LANES), lambda b, h, i: (b, h)),
            pl.BlockSpec((1, HEAD_DIM, seq), lambda b, h, i: (b, h, 0)),
            pl.BlockSpec((1, 1, n), lambda b, h, i: (h, 0, 0)),
            pl.BlockSpec((1, 1, n), lambda b, h, i: (h, 0, 0)),
        ],
        out_specs=pl.BlockSpec((tq, grp * HEAD_DIM), lambda b, h, i: (b * nq + i, h)),
        out_shape=jax.ShapeDtypeStruct((tokens, n_kv * grp * HEAD_DIM), BF16),
        compiler_params=_params(("arbitrary", "arbitrary", "arbitrary")),
        name="window_attn",
    )(q, k, vt, slope_row, sink_row)


def _out_proj_kernel(*refs, n_in):
    x_ref = refs[0]
    y_refs = refs[1:1 + n_in]
    w_refs = refs[1 + n_in:1 + 2 * n_in]
    o_ref = refs[1 + 2 * n_in]
    acc = x_ref[...]
    for y_ref, w_ref in zip(y_refs, w_refs):
        acc = acc + jnp.dot(y_ref[...], w_ref[...], preferred_element_type=F32)
    o_ref[...] = acc


def _out_proj(x2, ys, ws):
    tokens = x2.shape[0]
    tt = min(TOK_TILE, tokens)
    row = lambda i: (i, 0)
    const = lambda i: (0, 0)
    kern = functools.partial(_out_proj_kernel, n_in=len(ys))
    return pl.pallas_call(
        kern,
        grid=(tokens // tt,),
        in_specs=[pl.BlockSpec((tt, D_MODEL), row)]
        + [pl.BlockSpec((tt, y.shape[1]), row) for y in ys]
        + [pl.BlockSpec(w.shape, const) for w in ws],
        out_specs=pl.BlockSpec((tt, D_MODEL), row),
        out_shape=jax.ShapeDtypeStruct((tokens, D_MODEL), F32),
        compiler_params=_params(("arbitrary",)),
        name="out_proj",
    )(x2, *ys, *ws)


def _swiglu_partial(xb, wg, wu, wd):
    hg = jnp.dot(xb, wg, preferred_element_type=F32)
    hu = jnp.dot(xb, wu, preferred_element_type=F32)
    h = hg * jax.nn.sigmoid(hg) * hu
    return jnp.dot(h.astype(BF16), wd, preferred_element_type=F32)


def _ffn_kernel(x_ref, g_ref, wg_ref, wu_ref, wd_ref, o_ref, xn_ref, acc_ref):
    f = pl.program_id(1)

    @pl.when(f == 0)
    def _():
        xn_ref[...] = _rms(x_ref[...], g_ref[...]).astype(BF16)
        acc_ref[...] = jnp.zeros_like(acc_ref)

    acc_ref[...] += _swiglu_partial(xn_ref[...], wg_ref[...], wu_ref[...], wd_ref[...])

    @pl.when(f == pl.num_programs(1) - 1)
    def _():
        o_ref[...] = x_ref[...] + acc_ref[...]


def _ffn(x2, g, wg, wu, wd):
    tokens = x2.shape[0]
    ff = wg.shape[1]
    tt = min(FFN_TOK_TILE, tokens)
    fc = FFN_F_TILE
    assert tokens % tt == 0 and ff % fc == 0
    row = lambda i, f: (i, 0)
    return pl.pallas_call(
        _ffn_kernel,
        grid=(tokens // tt, ff // fc),
        in_specs=[
            pl.BlockSpec((tt, D_MODEL), row),
            pl.BlockSpec((1, D_MODEL), lambda i, f: (0, 0)),
            pl.BlockSpec((D_MODEL, fc), lambda i, f: (0, f)),
            pl.BlockSpec((D_MODEL, fc), lambda i, f: (0, f)),
            pl.BlockSpec((fc, D_MODEL), lambda i, f: (f, 0)),
        ],
        out_specs=pl.BlockSpec((tt, D_MODEL), row),
        out_shape=jax.ShapeDtypeStruct((tokens, D_MODEL), F32),
        scratch_shapes=[pltpu.VMEM((tt, D_MODEL), BF16), pltpu.VMEM((tt, D_MODEL), F32)],
        compiler_params=_params(("arbitrary", "arbitrary")),
        name="ffn",
    )(x2, g, wg, wu, wd)


def _router_kernel(x_ref, g_ref, rh_ref, rl_ref, xn_ref, ti_ref, tw_ref):
    tt = x_ref.shape[0]
    xn = _rms(x_ref[...], g_ref[...])
    xn_ref[...] = xn
    xh = xn.astype(BF16)
    xl = (xn - xh.astype(F32)).astype(BF16)
    logits = (jnp.dot(xh, rh_ref[...], preferred_element_type=F32)
              + jnp.dot(xh, rl_ref[...], preferred_element_type=F32)
              + jnp.dot(xl, rh_ref[...], preferred_element_type=F32))
    lane = lax.broadcasted_iota(jnp.int32, (tt, LANES), 1).astype(F32)
    lg = jnp.where(lane < N_EXPERTS, logits, -jnp.inf)
    m1 = jnp.max(lg, axis=1, keepdims=True)
    i1 = jnp.min(jnp.where(lg == m1, lane, float(LANES)), axis=1, keepdims=True)
    lg2 = jnp.where(lane == i1, -jnp.inf, lg)
    m2 = jnp.max(lg2, axis=1, keepdims=True)
    i2 = jnp.min(jnp.where(lg2 == m2, lane, float(LANES)), axis=1, keepdims=True)
    e2 = jnp.exp(m2 - m1)
    w1 = 1.0 / (1.0 + e2)
    w2 = e2 * w1
    ti_ref[...] = jnp.where(lane == 0.0, i1, i2).astype(jnp.int32)
    tw_ref[...] = jnp.where(lane == 0.0, w1, w2)


def _router(x2, g, r_hi, r_lo):
    tokens = x2.shape[0]
    tt = min(TOK_TILE, tokens)
    row = lambda i: (i, 0)
    const = lambda i: (0, 0)
    return pl.pallas_call(
        _router_kernel,
        grid=(tokens // tt,),
        in_specs=[
            pl.BlockSpec((tt, D_MODEL), row),
            pl.BlockSpec((1, D_MODEL), const),
            pl.BlockSpec((D_MODEL, LANES), const),
            pl.BlockSpec((D_MODEL, LANES), const),
        ],
        out_specs=[
            pl.BlockSpec((tt, D_MODEL), row),
            pl.BlockSpec((tt, LANES), row),
            pl.BlockSpec((tt, LANES), row),
        ],
        out_shape=[
            jax.ShapeDtypeStruct((tokens, D_MODEL), F32),
            jax.ShapeDtypeStruct((tokens, LANES), jnp.int32),
            jax.ShapeDtypeStruct((tokens, LANES), F32),
        ],
        compiler_params=_params(("arbitrary",)),
        name="moe_router",
    )(x2, g, r_hi, r_lo)


def _row_copy(idx_ref, src_ref, dst_ref, sem, r):
    return pltpu.make_async_copy(src_ref.at[pl.ds(idx_ref[r], 1)], dst_ref.at[pl.ds(r, 1)], sem)


def _row_gather_kernel(idx_ref, src_ref, dst_ref, sem, *, rows_per_step):
    base = pl.program_id(0) * rows_per_step

    def issue(u, carry):
        for v in range(GATHER_UNROLL):
            _row_copy(idx_ref, src_ref, dst_ref, sem, base + u * GATHER_UNROLL + v).start()
        return carry

    def drain(u, carry):
        for v in range(GATHER_UNROLL):
            _row_copy(idx_ref, src_ref, dst_ref, sem, base + u * GATHER_UNROLL + v).wait()
        return carry

    lax.fori_loop(0, rows_per_step // GATHER_UNROLL, issue, 0)
    lax.fori_loop(0, rows_per_step // GATHER_UNROLL, drain, 0)


def _row_gather(src, idx):
    n = idx.shape[0]
    rows_per_step = min(GATHER_ROWS, n)
    assert n % rows_per_step == 0 and rows_per_step % GATHER_UNROLL == 0
    return pl.pallas_call(
        functools.partial(_row_gather_kernel, rows_per_step=rows_per_step),
        grid_spec=pltpu.PrefetchScalarGridSpec(
            num_scalar_prefetch=1,
            grid=(n // rows_per_step,),
            in_specs=[pl.BlockSpec(memory_space=pl.ANY)],
            out_specs=pl.BlockSpec(memory_space=pl.ANY),
            scratch_shapes=[pltpu.SemaphoreType.DMA],
        ),
        out_shape=jax.ShapeDtypeStruct((n, src.shape[1]), src.dtype),
        compiler_params=_params(("arbitrary",)),
        name="row_gather",
    )(idx, src)


def _expert_ffn_kernel(be_ref, nu_ref, x_ref, wg_ref, wu_ref, wd_ref, o_ref, xb_ref, acc_ref):
    f = pl.program_id(1)
    used = pl.program_id(0) < nu_ref[0]
    last = f == pl.num_programs(1) - 1

    @pl.when(jnp.logical_and(used, f == 0))
    def _():
        xb_ref[...] = x_ref[...].astype(BF16)
        acc_ref[...] = jnp.zeros_like(acc_ref)

    @pl.when(used)
    def _():
        acc_ref[...] += _swiglu_partial(xb_ref[...], wg_ref[0], wu_ref[0], wd_ref[0])

    @pl.when(jnp.logical_and(used, last))
    def _():
        o_ref[...] = acc_ref[...]

    @pl.when(jnp.logical_and(jnp.logical_not(used), last))
    def _():
        o_ref[...] = jnp.zeros_like(o_ref)


def _expert_ffn(xg, blk_expert, n_used, wg, wu, wd):
    rows = xg.shape[0]
    ff = wg.shape[2]
    m = MOE_ROW_TILE
    fc = FFN_F_TILE
    assert rows % m == 0 and ff % fc == 0
    return pl.pallas_call(
        _expert_ffn_kernel,
        grid_spec=pltpu.PrefetchScalarGridSpec(
            num_scalar_prefetch=2,
            grid=(rows // m, ff // fc),
            in_specs=[
                pl.BlockSpec((m, D_MODEL), lambda b, f, be, nu: (b, 0)),
                pl.BlockSpec((1, D_MODEL, fc), lambda b, f, be, nu: (be[b], 0, f)),
                pl.BlockSpec((1, D_MODEL, fc), lambda b, f, be, nu: (be[b], 0, f)),
                pl.BlockSpec((1, fc, D_MODEL), lambda b, f, be, nu: (be[b], f, 0)),
            ],
            out_specs=pl.BlockSpec((m, D_MODEL), lambda b, f, be, nu: (b, 0)),
            scratch_shapes=[pltpu.VMEM((m, D_MODEL), BF16), pltpu.VMEM((m, D_MODEL), F32)],
        ),
        out_shape=jax.ShapeDtypeStruct((rows, D_MODEL), F32),
        compiler_params=_params(("arbitrary", "arbitrary")),
        name="expert_ffn",
    )(blk_expert, n_used, xg, wg, wu, wd)


def _moe_combine_kernel(x_ref, y_ref, tw_ref, gf_ref, o_ref, *, final_norm):
    tw = tw_ref[...]
    y = x_ref[...] + tw[:, 0:1] * y_ref[:, :D_MODEL] + tw[:, 1:2] * y_ref[:, D_MODEL:]
    if final_norm:
        y = _rms(y, gf_ref[...])
    o_ref[...] = y


def _moe_combine(x2, y_pairs, top_w, g_final, final_norm):
    tokens = x2.shape[0]
    tt = min(TOK_TILE, tokens)
    row = lambda i: (i, 0)
    return pl.pallas_call(
        functools.partial(_moe_combine_kernel, final_norm=final_norm),
        grid=(tokens // tt,),
        in_specs=[
            pl.BlockSpec((tt, D_MODEL), row),
            pl.BlockSpec((tt, 2 * D_MODEL), row),
            pl.BlockSpec((tt, LANES), row),
            pl.BlockSpec((1, D_MODEL), lambda i: (0, 0)),
        ],
        out_specs=pl.BlockSpec((tt, D_MODEL), row),
        out_shape=jax.ShapeDtypeStruct((tokens, D_MODEL), F32),
        compiler_params=_params(("arbitrary",)),
        name="moe_combine",
    )(x2, y_pairs, top_w, g_final)


def _route(top_i):
    m = MOE_ROW_TILE
    n_assign = top_i.shape[0] * TOP_K
    rows = n_assign + N_EXPERTS * m
    if rows > GATHER_ROWS:
        rows = -(-rows // GATHER_ROWS) * GATHER_ROWS
    e_flat = top_i.reshape(-1)
    onehot = (e_flat[:, None] == jnp.arange(N_EXPERTS, dtype=jnp.int32)[None, :]).astype(jnp.int32)
    csum = jnp.cumsum(onehot, axis=0)
    counts = csum[-1]
    rank = jnp.sum(csum * onehot, axis=1) - 1
    padded = (counts + m - 1) // m * m
    ends = jnp.cumsum(padded)
    starts = ends - padded
    pos = starts[e_flat] + rank
    order = jnp.argsort(e_flat, stable=True).astype(jnp.int32)
    ustarts = jnp.cumsum(counts) - counts
    slot = jnp.arange(rows, dtype=jnp.int32)
    e_slot = jnp.minimum(jnp.sum((slot[:, None] >= ends[None, :]).astype(jnp.int32), axis=1), N_EXPERTS - 1)
    j = jnp.clip(slot - starts[e_slot], 0, jnp.maximum(counts[e_slot] - 1, 0))
    tok = order[jnp.clip(ustarts[e_slot] + j, 0, n_assign - 1)] // TOP_K
    blk_expert = e_slot[::m]
    n_used = (ends[-1] // m).astype(jnp.int32)[None]
    return tok.astype(jnp.int32), pos.astype(jnp.int32), blk_expert, n_used


def _moe(x2, g, r_hi, r_lo, wg, wu, wd, g_final, final_norm):
    xn, top_i, top_w = _router(x2, g, r_hi, r_lo)
    tok, pos, blk_expert, n_used = _route(top_i[:, :TOP_K])
    xg = _row_gather(xn, tok)
    yg = _expert_ffn(xg, blk_expert, n_used, wg, wu, wd)
    y_pairs = _row_gather(yg, pos).reshape(x2.shape[0], TOP_K * D_MODEL)
    return _moe_combine(x2, y_pairs, top_w, g_final, final_norm)


def _pad_heads(w, n_heads, width, offset=0):
    r = w.shape[0]
    w = w.reshape(r, n_heads, width)
    w = jnp.pad(w, ((0, 0), (0, 0), (offset, LANES - width - offset)))
    return w.reshape(r, n_heads * LANES)


def _rope_tables(ang, base):
    half = ang.shape[1]
    cos = jnp.cos(ang)
    sin = jnp.sin(ang)
    pad = ((0, 0), (base, LANES - base - 2 * half))
    cos_t = jnp.pad(jnp.concatenate([cos, cos], axis=1) - 1.0, pad) + 1.0
    sin_t = jnp.pad(jnp.concatenate([-sin, sin], axis=1), pad)
    return cos_t, sin_t


def _rope_angles(pos, dim):
    inv = ROPE_THETA ** (-jnp.arange(0, dim, 2, dtype=F32) / dim)
    return pos.astype(F32)[:, None] * inv[None, :]


def _lane_gain(g):
    return jnp.pad(g.astype(F32), (0, LANES - g.shape[0]))[None, :]


def _trunk(x, p):
    bsz, seq, _ = x.shape
    x2 = x.reshape(bsz * seq, D_MODEL)
    n_rows = seq // GRID_W
    row = jnp.repeat(jnp.arange(n_rows), GRID_W)
    col = jnp.tile(jnp.arange(GRID_W), n_rows)
    ang_axial = jnp.concatenate([_rope_angles(row, HEAD_DIM // 2), _rope_angles(col, HEAD_DIM // 2)], axis=-1)
    cos_ax, sin_ax = _rope_tables(ang_axial, 0)
    cos_1d, sin_1d = _rope_tables(_rope_angles(jnp.arange(seq), MLA_ROPE_DIM), _MLA_ROPE_BASE)
    depth = p["ev_norm_mix"].shape[0] + p["od_norm_mix"].shape[0]
    assert depth % 2 == 0
    a_grp = A_HEADS // A_KV_HEADS
    for layer in range(depth):
        i = layer // 2
        last_layer = layer == depth - 1
        if layer % 2 == 0:
            aq, ak, avt, bq, bk, bvt = _even_in(
                x2, bsz, seq, p["ev_norm_mix"][i][None, :], p["ev_w_in"][i], cos_ax, sin_ax,
                p["b_q_norm"][i], p["b_k_norm"][i])
            ya = _window_attn(aq, ak, avt, p["a_slope"], p["a_sink"][i], bsz, seq, A_KV_HEADS, a_grp)
            yb = _dense_attn(bq, bk, bvt, bsz, seq, B_KV_HEADS, B_HEADS // B_KV_HEADS, 1, 256)
            x2 = _out_proj(x2, [ya, yb], [p["ev_w_out_a"][i], p["ev_w_out_b"][i]])
            x2 = _ffn(x2, p["ev_norm_ffn"][i][None, :], p["ffn_w_gate"][i], p["ffn_w_up"][i], p["ffn_w_down"][i])
        else:
            q, k, vt = _odd_in(
                x2, bsz, seq, p["od_norm_mix"][i][None, :], p["od_w_in"][i], p["mla_q_norm"][i][None, :],
                p["mla_w_q_up"][i], p["mla_kv_norm"][i][None, :], p["mla_w_k_up"][i], p["mla_w_v_up"][i],
                cos_1d, sin_1d)
            yc = _dense_attn(q, k, vt, bsz, seq, MLA_HEADS, 1, 2, 1024)
            x2 = _out_proj(x2, [yc], [p["od_w_out"][i]])
            x2 = _moe(x2, p["od_norm_ffn"][i][None, :], p["router_hi"][i], p["router_lo"][i],
                      p["moe_w_gate"][i], p["moe_w_up"][i], p["moe_w_down"][i],
                      p["final_norm"], last_layer)
    return x2.reshape(bsz, seq, D_MODEL)


def _prepare(ev_norm_mix, ev_w_in, a_sink, b_q_norm, b_k_norm, ev_w_out, ev_norm_ffn,
             ffn_w_gate, ffn_w_up, ffn_w_down, od_norm_mix, od_w_in, mla_q_norm, mla_w_q_up,
             mla_kv_norm, mla_w_kv_up, od_w_out, od_norm_ffn, moe_router, moe_w_gate, moe_w_up,
             moe_w_down, final_norm):
    hd = HEAD_DIM
    a_grp = A_HEADS // A_KV_HEADS

    def even_w_in(w):
        sizes = [A_HEADS * hd, A_KV_HEADS * hd, A_KV_HEADS * hd, B_HEADS * hd, B_KV_HEADS * hd, B_KV_HEADS * hd]
        aq, ak, av, bq, bk, bv = jnp.split(w, list(np.cumsum(sizes)[:-1]), axis=-1)
        return jnp.concatenate([_pad_heads(aq, A_HEADS, hd), _pad_heads(ak, A_KV_HEADS, hd),
                                _pad_heads(bq, B_HEADS, hd), _pad_heads(bk, B_KV_HEADS, hd), av, bv],
                               axis=-1).astype(BF16)

    def odd_w_in(w):
        c = w[:, :MLA_Q_RANK + MLA_KV_RANK]
        kr = _pad_heads(w[:, MLA_Q_RANK + MLA_KV_RANK:], 1, MLA_ROPE_DIM, _MLA_ROPE_BASE)
        return jnp.concatenate([c, kr], axis=-1).astype(BF16)

    def q_up(w):
        return _pad_heads(w, MLA_HEADS, MLA_QK_DIM).astype(BF16)

    def kv_up(w):
        w = w.reshape(MLA_KV_RANK, MLA_HEADS, MLA_NOPE_DIM + MLA_V_DIM)
        wk = _pad_heads(w[:, :, :MLA_NOPE_DIM].reshape(MLA_KV_RANK, -1), MLA_HEADS, MLA_NOPE_DIM)
        wv = w[:, :, MLA_NOPE_DIM:].reshape(MLA_KV_RANK, -1)
        return wk.astype(BF16), wv.astype(BF16)

    slopes = jnp.asarray(2.0 ** (-8.0 * np.arange(1, A_HEADS + 1) / A_HEADS), dtype=F32)
    per_col = lambda v: jnp.repeat(v.astype(F32).reshape(A_KV_HEADS, a_grp), WIN_Q, axis=1)[:, None, :] * LOG2E
    router = jnp.pad(moe_router.astype(F32), ((0, 0), (0, 0), (0, LANES - N_EXPERTS)))
    router_hi = router.astype(BF16)
    kv = [kv_up(w) for w in mla_w_kv_up]
    return {
        "ev_norm_mix": ev_norm_mix, "ev_w_in": jnp.stack([even_w_in(w) for w in ev_w_in]),
        "a_slope": per_col(slopes), "a_sink": jnp.stack([per_col(s) for s in a_sink]),
        "b_q_norm": jnp.stack([_lane_gain(g) for g in b_q_norm]),
        "b_k_norm": jnp.stack([_lane_gain(g) for g in b_k_norm]),
        "ev_w_out_a": ev_w_out[:, :A_HEADS * hd].astype(BF16),
        "ev_w_out_b": ev_w_out[:, A_HEADS * hd:].astype(BF16),
        "ev_norm_ffn": ev_norm_ffn,
        "ffn_w_gate": ffn_w_gate.astype(BF16), "ffn_w_up": ffn_w_up.astype(BF16),
        "ffn_w_down": ffn_w_down.astype(BF16),
        "od_norm_mix": od_norm_mix, "od_w_in": jnp.stack([odd_w_in(w) for w in od_w_in]),
        "mla_q_norm": mla_q_norm, "mla_w_q_up": jnp.stack([q_up(w) for w in mla_w_q_up]),
        "mla_kv_norm": mla_kv_norm,
        "mla_w_k_up": jnp.stack([a for a, _ in kv]), "mla_w_v_up": jnp.stack([b for _, b in kv]),
        "od_w_out": od_w_out.astype(BF16), "od_norm_ffn": od_norm_ffn,
        "router_hi": router_hi, "router_lo": (router - router_hi.astype(F32)).astype(BF16),
        "moe_w_gate": moe_w_gate.astype(BF16), "moe_w_up": moe_w_up.astype(BF16),
        "moe_w_down": moe_w_down.astype(BF16),
        "final_norm": final_norm[None, :],
    }


def kernel(x_prompt, x_sample, ev_norm_mix, ev_w_in, a_sink, b_q_norm, b_k_norm, ev_w_out, ev_norm_ffn, ffn_w_gate, ffn_w_up, ffn_w_down, od_norm_mix, od_w_in, mla_q_norm, mla_w_q_up, mla_kv_norm, mla_w_kv_up, od_w_out, od_norm_ffn, moe_router, moe_w_gate, moe_w_up, moe_w_down, final_norm):
    p = _prepare(ev_norm_mix, ev_w_in, a_sink, b_q_norm, b_k_norm, ev_w_out, ev_norm_ffn,
                 ffn_w_gate, ffn_w_up, ffn_w_down, od_norm_mix, od_w_in, mla_q_norm, mla_w_q_up,
                 mla_kv_norm, mla_w_kv_up, od_w_out, od_norm_ffn, moe_router, moe_w_gate, moe_w_up,
                 moe_w_down, final_norm)
    return (_trunk(x_prompt, p), _trunk(x_sample, p))
```

```python
import functools
import math

import numpy as np
import jax
import jax.numpy as jnp
from jax import lax
from jax.experimental import pallas as pl
from jax.experimental.pallas import tpu as pltpu

F32 = jnp.float32
BF16 = jnp.bfloat16

D_MODEL = 1024
HEAD_DIM = 64
WINDOW = 128
GRID_W = 64
ROPE_THETA = 10000.0
NORM_EPS = 1e-6
A_HEADS, A_KV_HEADS = 8, 2
B_HEADS, B_KV_HEADS = 8, 2
MLA_HEADS = 8
MLA_Q_RANK, MLA_KV_RANK = 256, 128
MLA_NOPE_DIM, MLA_ROPE_DIM, MLA_V_DIM = 64, 32, 64
MLA_QK_DIM = MLA_NOPE_DIM + MLA_ROPE_DIM
N_EXPERTS = 8
TOP_K = 2
LANES = 128
LOG2E = math.log2(math.e)
VMEM_LIMIT = 48 * 1024 * 1024

TOK_TILE = 512
FFN_TOK_TILE = 1024
FFN_F_TILE = 512
KEY_CHUNK = 512
WIN_Q = 128
MOE_ROW_TILE = 512
EXPERT_F_STEPS = 4
DMA_UNROLL = 8


def _rms(x, g):
    return x * lax.rsqrt(jnp.mean(x * x, axis=-1, keepdims=True) + NORM_EPS) * g


def _swap_halves(x, half):
    lane = lax.broadcasted_iota(jnp.int32, x.shape, 1)
    return jnp.where(lane < half, pltpu.roll(x, LANES - half, 1), pltpu.roll(x, half, 1))


def _swap_halves_at(x, base, half):
    lane = lax.broadcasted_iota(jnp.int32, x.shape, 1)
    return jnp.where(lane < base + half, pltpu.roll(x, LANES - half, 1), pltpu.roll(x, half, 1))


def _params(sem):
    return pltpu.CompilerParams(dimension_semantics=sem, vmem_limit_bytes=VMEM_LIMIT)


_EV_AQ, _EV_AK, _EV_BQ, _EV_BK, _EV_V, _EV_END = 0, 1024, 1280, 2304, 2560, 2816


def _even_in_kernel(x_ref, g_ref, w_ref, cos_ref, sin_ref, qg_ref, kg_ref,
                    aq_ref, ak_ref, avt_ref, bq_ref, bk_ref, bvt_ref):
    xb = _rms(x_ref[...], g_ref[...]).astype(BF16)
    cos = cos_ref[...]
    sin = sin_ref[...]
    qscale = (HEAD_DIM ** -0.5) * LOG2E

    def norm_rope(blk, gain):
        ms = jnp.sum(blk * blk, axis=-1, keepdims=True) * (1.0 / HEAD_DIM)
        y = blk * lax.rsqrt(ms + NORM_EPS) * gain
        return y * cos + _swap_halves(y, HEAD_DIM // 2) * sin

    pa = jnp.dot(xb, w_ref[:, _EV_AQ:_EV_AK], preferred_element_type=F32)
    aq_ref[...] = (pa * qscale).astype(BF16)
    ak_ref[...] = jnp.dot(xb, w_ref[:, _EV_AK:_EV_BQ], preferred_element_type=F32).astype(BF16)
    pbq = jnp.dot(xb, w_ref[:, _EV_BQ:_EV_BK], preferred_element_type=F32)
    for h in range(B_HEADS):
        sl = slice(h * LANES, (h + 1) * LANES)
        bq_ref[:, sl] = (norm_rope(pbq[:, sl], qg_ref[...]) * qscale).astype(BF16)
    pbk = jnp.dot(xb, w_ref[:, _EV_BK:_EV_V], preferred_element_type=F32)
    for h in range(B_KV_HEADS):
        sl = slice(h * LANES, (h + 1) * LANES)
        bk_ref[:, sl] = norm_rope(pbk[:, sl], kg_ref[...]).astype(BF16)
    pv = jnp.dot(xb, w_ref[:, _EV_V:_EV_END], preferred_element_type=F32)
    avt_ref[0] = pv[:, :LANES].T.astype(BF16)
    bvt_ref[0] = pv[:, LANES:].T.astype(BF16)


def _even_in(x2, bsz, seq, g, w_exp, cos_t, sin_t, qg, kg):
    tokens = x2.shape[0]
    tt = min(TOK_TILE, seq)
    assert seq % tt == 0
    ns = seq // tt
    row = lambda i: (i, 0)
    const = lambda i: (0, 0)
    tab = lambda i: (i % ns, 0)
    vt = lambda i: (i // ns, 0, i % ns)
    return pl.pallas_call(
        _even_in_kernel,
        grid=(tokens // tt,),
        in_specs=[
            pl.BlockSpec((tt, D_MODEL), row),
            pl.BlockSpec((1, D_MODEL), const),
            pl.BlockSpec((D_MODEL, _EV_END), const),
            pl.BlockSpec((tt, LANES), tab),
            pl.BlockSpec((tt, LANES), tab),
            pl.BlockSpec((1, LANES), const),
            pl.BlockSpec((1, LANES), const),
        ],
        out_specs=[
            pl.BlockSpec((tt, A_HEADS * LANES), row),
            pl.BlockSpec((tt, A_KV_HEADS * LANES), row),
            pl.BlockSpec((1, A_KV_HEADS * HEAD_DIM, tt), vt),
            pl.BlockSpec((tt, B_HEADS * LANES), row),
            pl.BlockSpec((tt, B_KV_HEADS * LANES), row),
            pl.BlockSpec((1, B_KV_HEADS * HEAD_DIM, tt), vt),
        ],
        out_shape=[
            jax.ShapeDtypeStruct((tokens, A_HEADS * LANES), BF16),
            jax.ShapeDtypeStruct((tokens, A_KV_HEADS * LANES), BF16),
            jax.ShapeDtypeStruct((bsz, A_KV_HEADS * HEAD_DIM, seq), BF16),
            jax.ShapeDtypeStruct((tokens, B_HEADS * LANES), BF16),
            jax.ShapeDtypeStruct((tokens, B_KV_HEADS * LANES), BF16),
            jax.ShapeDtypeStruct((bsz, B_KV_HEADS * HEAD_DIM, seq), BF16),
        ],
        compiler_params=_params(("arbitrary",)),
        name="even_in_proj",
    )(x2, g, w_exp, cos_t, sin_t, qg, kg)


_MLA_ROPE_BASE = MLA_NOPE_DIM


def _odd_in_kernel(x_ref, g_ref, w_ref, qn_ref, wq_ref, kvn_ref, wk_ref, wv_ref,
                   cos_ref, sin_ref, q_ref, k_ref, vt_ref):
    xb = _rms(x_ref[...], g_ref[...]).astype(BF16)
    cos = cos_ref[...]
    sin = sin_ref[...]
    qscale = (MLA_QK_DIM ** -0.5) * LOG2E

    def rope(blk):
        return blk * cos + _swap_halves_at(blk, _MLA_ROPE_BASE, MLA_ROPE_DIM // 2) * sin

    proj = jnp.dot(xb, w_ref[...], preferred_element_type=F32)
    cq = _rms(proj[:, :MLA_Q_RANK], qn_ref[...]).astype(BF16)
    ckv = _rms(proj[:, MLA_Q_RANK:MLA_Q_RANK + MLA_KV_RANK], kvn_ref[...]).astype(BF16)
    kr = rope(proj[:, MLA_Q_RANK + MLA_KV_RANK:])
    q = jnp.dot(cq, wq_ref[...], preferred_element_type=F32)
    kn = jnp.dot(ckv, wk_ref[...], preferred_element_type=F32)
    for h in range(MLA_HEADS):
        sl = slice(h * LANES, (h + 1) * LANES)
        q_ref[:, sl] = (rope(q[:, sl]) * qscale).astype(BF16)
        k_ref[:, sl] = (kn[:, sl] + kr).astype(BF16)
    v = jnp.dot(ckv, wv_ref[...], preferred_element_type=F32)
    for c in range(MLA_HEADS * MLA_V_DIM // LANES):
        vt_ref[0, c * LANES:(c + 1) * LANES, :] = v[:, c * LANES:(c + 1) * LANES].T.astype(BF16)


def _odd_in(x2, bsz, seq, g, w_exp, qn, wq_exp, kvn, wk_exp, wv, cos_t, sin_t):
    tokens = x2.shape[0]
    tt = min(TOK_TILE, seq)
    assert seq % tt == 0
    ns = seq // tt
    row = lambda i: (i, 0)
    const = lambda i: (0, 0)
    tab = lambda i: (i % ns, 0)
    vt = lambda i: (i // ns, 0, i % ns)
    hl = MLA_HEADS * LANES
    hv = MLA_HEADS * MLA_V_DIM
    return pl.pallas_call(
        _odd_in_kernel,
        grid=(tokens // tt,),
        in_specs=[
            pl.BlockSpec((tt, D_MODEL), row),
            pl.BlockSpec((1, D_MODEL), const),
            pl.BlockSpec((D_MODEL, 4 * LANES), const),
            pl.BlockSpec((1, MLA_Q_RANK), const),
            pl.BlockSpec((MLA_Q_RANK, hl), const),
            pl.BlockSpec((1, MLA_KV_RANK), const),
            pl.BlockSpec((MLA_KV_RANK, hl), const),
            pl.BlockSpec((MLA_KV_RANK, hv), const),
            pl.BlockSpec((tt, LANES), tab),
            pl.BlockSpec((tt, LANES), tab),
        ],
        out_specs=[
            pl.BlockSpec((tt, hl), row),
            pl.BlockSpec((tt, hl), row),
            pl.BlockSpec((1, hv, tt), vt),
        ],
        out_shape=[
            jax.ShapeDtypeStruct((tokens, hl), BF16),
            jax.ShapeDtypeStruct((tokens, hl), BF16),
            jax.ShapeDtypeStruct((bsz, hv, seq), BF16),
        ],
        compiler_params=_params(("arbitrary",)),
        name="odd_in_proj",
    )(x2, g, w_exp, qn, wq_exp, kvn, wk_exp, wv, cos_t, sin_t)


def _pair_transpose(pieces):
    outs = []
    for a in range(0, len(pieces), 2):
        outs.append(jnp.concatenate([pieces[a], pieces[a + 1]], axis=0).T)
    return outs[0] if len(outs) == 1 else jnp.concatenate(outs, axis=1)


SUM_ROWS = 16


def _dense_attn_kernel(q_ref, k_ref, vt_ref, o_ref, s_ref, cm_ref, m_ref, acc_ref, *, kb, grp, tq, tk, seq):
    n = grp * tq
    dv = HEAD_DIM
    n_chunks = seq // tk
    ones = jnp.ones((SUM_ROWS, tk), BF16)
    pieces = []
    for j in range(kb):
        qs = jnp.concatenate(
            [q_ref[:, (j * grp + g) * LANES:(j * grp + g + 1) * LANES] for g in range(grp)], axis=0)

        def scores(c, slot, j=j, qs=qs):
            off = pl.multiple_of(c * tk, tk)
            kc = k_ref[pl.ds(off, tk), j * LANES:(j + 1) * LANES]
            s = lax.dot_general(kc, qs, (((1,), (1,)), ((), ())),
                                preferred_element_type=F32)
            s_ref[slot] = s
            cm_ref[slot] = jnp.max(s, axis=0, keepdims=True)

        def softmax_pv(c, slot, j=j):
            off = pl.multiple_of(c * tk, tk)
            m_old = m_ref[...]
            m_new = jnp.maximum(m_old, cm_ref[slot])
            alpha = jnp.exp2(m_old - m_new)
            p = jnp.exp2((s_ref[slot] - m_new).astype(BF16))
            vc = jnp.concatenate([vt_ref[0, j * dv:(j + 1) * dv, pl.ds(off, tk)], ones], axis=0)
            pv = jnp.dot(vc, p, preferred_element_type=F32)
            acc_ref[...] = alpha * acc_ref[...] + pv
            m_ref[...] = m_new

        m_ref[...] = jnp.full((1, n), -jnp.inf, F32)
        acc_ref[...] = jnp.zeros((dv + SUM_ROWS, n), F32)
        scores(0, 0)

        def body(c2, carry, scores=scores, softmax_pv=softmax_pv):
            c = 2 * c2
            scores(c + 1, 1)
            softmax_pv(c, 0)
            scores(c + 2, 0)
            softmax_pv(c + 1, 1)
            return carry

        lax.fori_loop(0, n_chunks // 2 - 1, body, 0)
        scores(n_chunks - 1, 1)
        softmax_pv(n_chunks - 2, 0)
        softmax_pv(n_chunks - 1, 1)
        acc = acc_ref[...]
        o = acc[:dv] / acc[dv:dv + 1]
        for g in range(grp):
            pieces.append(o[:, g * tq:(g + 1) * tq])
    o_ref[...] = _pair_transpose(pieces).astype(o_ref.dtype)


def _dense_attn(q, k, vt, bsz, seq, n_kv, grp, kb, tq):
    tokens = q.shape[0]
    tq = min(tq, seq)
    tk = min(KEY_CHUNK, seq // 2)
    n_chunks = seq // tk
    assert seq % tq == 0 and seq % tk == 0 and n_chunks % 2 == 0
    assert n_kv % kb == 0 and (kb * grp) % 2 == 0
    nq = seq // tq
    n = grp * tq
    kern = functools.partial(_dense_attn_kernel, kb=kb, grp=grp, tq=tq, tk=tk, seq=seq)
    return pl.pallas_call(
        kern,
        grid=(bsz, n_kv // kb, nq),
        in_specs=[
            pl.BlockSpec((tq, kb * grp * LANES), lambda b, h, i: (b * nq + i, h)),
            pl.BlockSpec((seq, kb * LANES), lambda b, h, i: (b, h)),
            pl.BlockSpec((1, kb * HEAD_DIM, seq), lambda b, h, i: (b, h, 0)),
        ],
        out_specs=pl.BlockSpec((tq, kb * grp * HEAD_DIM), lambda b, h, i: (b * nq + i, h)),
        out_shape=jax.ShapeDtypeStruct((tokens, n_kv * grp * HEAD_DIM), BF16),
        scratch_shapes=[
            pltpu.VMEM((2, tk, n), F32),
            pltpu.VMEM((2, 1, n), F32),
            pltpu.VMEM((1, n), F32),
            pltpu.VMEM((HEAD_DIM + SUM_ROWS, n), F32),
        ],
        compiler_params=_params(("arbitrary", "arbitrary", "arbitrary")),
        name="dense_attn",
    )(q, k, vt)


def _window_attn_kernel(q_ref, k_ref, vt_ref, slope_ref, sink_ref, o_ref, *, grp, seq):
    tq = WIN_Q
    span = 3 * WIN_Q
    n = grp * tq
    i = pl.program_id(2)
    w0 = pl.multiple_of(jnp.clip((i - 1) * tq, 0, seq - span), tq)
    qs = jnp.concatenate([q_ref[:, g * LANES:(g + 1) * LANES] for g in range(grp)], axis=0)
    kc = k_ref[pl.ds(w0, span), :]
    s = lax.dot_general(kc, qs, (((1,), (1,)), ((), ())), preferred_element_type=F32)
    kpos = w0 + lax.broadcasted_iota(jnp.int32, (span, n), 0)
    qpos = i * tq + lax.broadcasted_iota(jnp.int32, (span, n), 1) % tq
    dist = jnp.abs(kpos - qpos)
    s = s - slope_ref[0] * dist.astype(F32)
    s = jnp.where(dist <= WINDOW, s, -jnp.inf)
    sink = sink_ref[0]
    m = jnp.maximum(jnp.max(s, axis=0, keepdims=True), sink)
    e = jnp.exp2(s - m)
    denom = jnp.sum(e, axis=0, keepdims=True) + jnp.exp2(sink - m)
    vc = vt_ref[0, :, pl.ds(w0, span)]
    o = jnp.dot(vc, e.astype(BF16), preferred_element_type=F32) / denom
    o_ref[...] = _pair_transpose([o[:, g * tq:(g + 1) * tq] for g in range(grp)]).astype(o_ref.dtype)


def _window_attn(q, k, vt, slope_row, sink_row, bsz, seq, n_kv, grp):
    tokens = q.shape[0]
    tq = WIN_Q
    assert seq % tq == 0 and seq >= 3 * tq
    nq = seq // tq
    n = grp * tq
    kern = functools.partial(_window_attn_kernel, grp=grp, seq=seq)
    return pl.pallas_call(
        kern,
        grid=(bsz, n_kv, nq),
        in_specs=[
            pl.BlockSpec((tq, grp * LANES), lambda b, h, i: (b * nq + i, h)),
            pl.BlockSpec((seq, LANES), lambda b, h, i: (b, h)),
            pl.BlockSpec((1, HEAD_DIM, seq), lambda b, h, i: (b, h, 0)),
            pl.BlockSpec((1, 1, n), lambda b, h, i: (h, 0, 0)),
            pl.BlockSpec((1, 1, n), lambda b, h, i: (h, 0, 0)),
        ],
        out_specs=pl.BlockSpec((tq, grp * HEAD_DIM), lambda b, h, i: (b * nq + i, h)),
        out_shape=jax.ShapeDtypeStruct((tokens, n_kv * grp * HEAD_DIM), BF16),
        compiler_params=_params(("arbitrary", "arbitrary", "arbitrary")),
        name="window_attn",
    )(q, k, vt, slope_row, sink_row)


def _out_proj_kernel(*refs, n_in):
    x_ref = refs[0]
    y_refs = refs[1:1 + n_in]
    w_refs = refs[1 + n_in:1 + 2 * n_in]
    o_ref = refs[1 + 2 * n_in]
    acc = x_ref[...]
    for y_ref, w_ref in zip(y_refs, w_refs):
        acc = acc + jnp.dot(y_ref[...], w_ref[...], preferred_element_type=F32)
    o_ref[...] = acc


def _out_proj(x2, ys, ws):
    tokens = x2.shape[0]
    tt = min(TOK_TILE, tokens)
    row = lambda i: (i, 0)
    const = lambda i: (0, 0)
    kern = functools.partial(_out_proj_kernel, n_in=len(ys))
    return pl.pallas_call(
        kern,
        grid=(tokens // tt,),
        in_specs=[pl.BlockSpec((tt, D_MODEL), row)]
        + [pl.BlockSpec((tt, y.shape[1]), row) for y in ys]
        + [pl.BlockSpec(w.shape, const) for w in ws],
        out_specs=pl.BlockSpec((tt, D_MODEL), row),
        out_shape=jax.ShapeDtypeStruct((tokens, D_MODEL), F32),
        compiler_params=_params(("arbitrary",)),
        name="out_proj",
    )(x2, *ys, *ws)


def _swiglu_partial(xb, wg, wu, wd):
    hg = jnp.dot(xb, wg, preferred_element_type=F32)
    hu = jnp.dot(xb, wu, preferred_element_type=F32)
    h = hg * jax.nn.sigmoid(hg) * hu
    return jnp.dot(h.astype(BF16), wd, preferred_element_type=F32)


def _ffn_kernel(x_ref, g_ref, wg_ref, wu_ref, wd_ref, o_ref, xn_ref, acc_ref):
    f = pl.program_id(1)

    @pl.when(f == 0)
    def _():
        xn_ref[...] = _rms(x_ref[...], g_ref[...]).astype(BF16)
        acc_ref[...] = jnp.zeros_like(acc_ref)

    acc_ref[...] += _swiglu_partial(xn_ref[...], wg_ref[...], wu_ref[...], wd_ref[...])

    @pl.when(f == pl.num_programs(1) - 1)
    def _():
        o_ref[...] = x_ref[...] + acc_ref[...]


def _ffn(x2, g, wg, wu, wd):
    tokens = x2.shape[0]
    ff = wg.shape[1]
    tt = min(FFN_TOK_TILE, tokens)
    fc = FFN_F_TILE
    assert tokens % tt == 0 and ff % fc == 0
    row = lambda i, f: (i, 0)
    return pl.pallas_call(
        _ffn_kernel,
        grid=(tokens // tt, ff // fc),
        in_specs=[
            pl.BlockSpec((tt, D_MODEL), row),
            pl.BlockSpec((1, D_MODEL), lambda i, f: (0, 0)),
            pl.BlockSpec((D_MODEL, fc), lambda i, f: (0, f)),
            pl.BlockSpec((D_MODEL, fc), lambda i, f: (0, f)),
            pl.BlockSpec((fc, D_MODEL), lambda i, f: (f, 0)),
        ],
        out_specs=pl.BlockSpec((tt, D_MODEL), row),
        out_shape=jax.ShapeDtypeStruct((tokens, D_MODEL), F32),
        scratch_shapes=[pltpu.VMEM((tt, D_MODEL), BF16), pltpu.VMEM((tt, D_MODEL), F32)],
        compiler_params=_params(("arbitrary", "arbitrary")),
        name="ffn",
    )(x2, g, wg, wu, wd)


def _router_kernel(x_ref, g_ref, rh_ref, rl_ref, xn_ref, ti_ref, tw_ref):
    tt = x_ref.shape[0]
    xn = _rms(x_ref[...], g_ref[...])
    xn_ref[...] = xn
    xh = xn.astype(BF16)
    xl = (xn - xh.astype(F32)).astype(BF16)
    logits = (jnp.dot(xh, rh_ref[...], preferred_element_type=F32)
              + jnp.dot(xh, rl_ref[...], preferred_element_type=F32)
              + jnp.dot(xl, rh_ref[...], preferred_element_type=F32))
    lane = lax.broadcasted_iota(jnp.int32, (tt, LANES), 1).astype(F32)
    lg = jnp.where(lane < N_EXPERTS, logits, -jnp.inf)
    m1 = jnp.max(lg, axis=1, keepdims=True)
    i1 = jnp.min(jnp.where(lg == m1, lane, float(LANES)), axis=1, keepdims=True)
    lg2 = jnp.where(lane == i1, -jnp.inf, lg)
    m2 = jnp.max(lg2, axis=1, keepdims=True)
    i2 = jnp.min(jnp.where(lg2 == m2, lane, float(LANES)), axis=1, keepdims=True)
    e2 = jnp.exp(m2 - m1)
    w1 = 1.0 / (1.0 + e2)
    w2 = e2 * w1
    ti_ref[...] = jnp.where(lane == 0.0, i1, i2).astype(jnp.int32)
    tw_ref[...] = jnp.where(lane == 0.0, w1, w2)


def _router(x2, g, r_hi, r_lo):
    tokens = x2.shape[0]
    tt = min(TOK_TILE, tokens)
    row = lambda i: (i, 0)
    const = lambda i: (0, 0)
    return pl.pallas_call(
        _router_kernel,
        grid=(tokens // tt,),
        in_specs=[
            pl.BlockSpec((tt, D_MODEL), row),
            pl.BlockSpec((1, D_MODEL), const),
            pl.BlockSpec((D_MODEL, LANES), const),
            pl.BlockSpec((D_MODEL, LANES), const),
        ],
        out_specs=[
            pl.BlockSpec((tt, D_MODEL), row),
            pl.BlockSpec((tt, LANES), row),
            pl.BlockSpec((tt, LANES), row),
        ],
        out_shape=[
            jax.ShapeDtypeStruct((tokens, D_MODEL), F32),
            jax.ShapeDtypeStruct((tokens, LANES), jnp.int32),
            jax.ShapeDtypeStruct((tokens, LANES), F32),
        ],
        compiler_params=_params(("arbitrary",)),
        name="moe_router",
    )(x2, g, r_hi, r_lo)


def _for_rows(n_rows, fn):
    def group(u, carry):
        for v in range(DMA_UNROLL):
            fn(u * DMA_UNROLL + v)
        return carry
    lax.fori_loop(0, n_rows // DMA_UNROLL, group, 0)


def _expert_ffn_kernel(be_ref, nu_ref, dst_ref, xn_hbm, wg_ref, wu_ref, wd_ref, y_hbm,
                       xg_ref, xb_ref, acc_ref, yb_ref, gsem, ssem, *, tokens):
    b = pl.program_id(0)
    f = pl.program_id(1)
    nf = pl.num_programs(1)
    nu = nu_ref[0]
    used = b < nu
    last = f == nf - 1
    slot = b % 2
    m = MOE_ROW_TILE
    per_step = m // EXPERT_F_STEPS
    nxt = jnp.minimum(b + 1, nu - 1)

    def gather(blk, r, sl):
        tok = lax.rem(dst_ref[blk * m + r], tokens)
        return pltpu.make_async_copy(xn_hbm.at[pl.ds(tok, 1)], xg_ref.at[sl, pl.ds(r, 1)], gsem.at[sl])

    def scatter(blk, r):
        return pltpu.make_async_copy(yb_ref.at[pl.ds(r, 1)], y_hbm.at[pl.ds(dst_ref[blk * m + r], 1)], ssem)

    @pl.when(jnp.logical_and(used, jnp.logical_and(f == 0, b == 0)))
    def _():
        _for_rows(m, lambda r: gather(0, r, 0).start())

    @pl.when(jnp.logical_and(used, f == 0))
    def _():
        _for_rows(m, lambda r: gather(b, r, slot).wait())
        xb_ref[...] = xg_ref[slot].astype(BF16)
        acc_ref[...] = jnp.zeros_like(acc_ref)

    def step(with_scatter):
        for v in range(per_step):
            r = f * per_step + v
            gather(nxt, r, 1 - slot).start()
            if with_scatter:
                scatter(b - 1, r).start()
        acc_ref[...] += _swiglu_partial(xb_ref[...], wg_ref[0], wu_ref[0], wd_ref[0])

    @pl.when(jnp.logical_and(used, b == 0))
    def _():
        step(False)

    @pl.when(jnp.logical_and(used, b > 0))
    def _():
        step(True)

    @pl.when(jnp.logical_and(used, jnp.logical_and(last, b > 0)))
    def _():
        _for_rows(m, lambda r: scatter(b - 1, r).wait())

    @pl.when(jnp.logical_and(used, last))
    def _():
        yb_ref[...] = acc_ref[...]

    @pl.when(jnp.logical_and(last, b == nu - 1))
    def _():
        _for_rows(m, lambda r: scatter(b, r).start())
        _for_rows(m, lambda r: gather(nxt, r, 1 - slot).wait())
        _for_rows(m, lambda r: scatter(b, r).wait())


def _expert_ffn(xn, dst, blk_expert, n_used, wg, wu, wd):
    tokens = xn.shape[0]
    rows = dst.shape[0]
    ff = wg.shape[2]
    m = MOE_ROW_TILE
    assert rows % m == 0 and ff % EXPERT_F_STEPS == 0 and m % EXPERT_F_STEPS == 0
    fc = ff // EXPERT_F_STEPS
    assert fc % LANES == 0
    return pl.pallas_call(
        functools.partial(_expert_ffn_kernel, tokens=tokens),
        grid_spec=pltpu.PrefetchScalarGridSpec(
            num_scalar_prefetch=3,
            grid=(rows // m, EXPERT_F_STEPS),
            in_specs=[
                pl.BlockSpec(memory_space=pl.ANY),
                pl.BlockSpec((1, D_MODEL, fc), lambda b, f, be, nu, ds: (be[b], 0, f)),
                pl.BlockSpec((1, D_MODEL, fc), lambda b, f, be, nu, ds: (be[b], 0, f)),
                pl.BlockSpec((1, fc, D_MODEL), lambda b, f, be, nu, ds: (be[b], f, 0)),
            ],
            out_specs=pl.BlockSpec(memory_space=pl.ANY),
            scratch_shapes=[
                pltpu.VMEM((2, m, D_MODEL), F32),
                pltpu.VMEM((m, D_MODEL), BF16),
                pltpu.VMEM((m, D_MODEL), F32),
                pltpu.VMEM((m, D_MODEL), F32),
                pltpu.SemaphoreType.DMA((2,)),
                pltpu.SemaphoreType.DMA,
            ],
        ),
        out_shape=jax.ShapeDtypeStruct((TOP_K * tokens + rows, D_MODEL), F32),
        compiler_params=_params(("arbitrary", "arbitrary")),
        name="expert_ffn",
    )(blk_expert, n_used, dst, xn, wg, wu, wd)


def _moe_combine_kernel(x_ref, y0_ref, y1_ref, tw_ref, gf_ref, o_ref, *, final_norm):
    tw = tw_ref[...]
    y = x_ref[...] + tw[:, 0:1] * y0_ref[...] + tw[:, 1:2] * y1_ref[...]
    if final_norm:
        y = _rms(y, gf_ref[...])
    o_ref[...] = y


def _moe_combine(x2, y, top_w, g_final, final_norm):
    tokens = x2.shape[0]
    tt = min(TOK_TILE, tokens)
    nt = tokens // tt
    row = lambda i: (i, 0)
    return pl.pallas_call(
        functools.partial(_moe_combine_kernel, final_norm=final_norm),
        grid=(nt,),
        in_specs=[
            pl.BlockSpec((tt, D_MODEL), row),
            pl.BlockSpec((tt, D_MODEL), row),
            pl.BlockSpec((tt, D_MODEL), lambda i: (nt + i, 0)),
            pl.BlockSpec((tt, LANES), row),
            pl.BlockSpec((1, D_MODEL), lambda i: (0, 0)),
        ],
        out_specs=pl.BlockSpec((tt, D_MODEL), row),
        out_shape=jax.ShapeDtypeStruct((tokens, D_MODEL), F32),
        compiler_params=_params(("arbitrary",)),
        name="moe_combine",
    )(x2, y, y, top_w, g_final)


def _route(top_i):
    m = MOE_ROW_TILE
    tokens = top_i.shape[0]
    n_assign = tokens * TOP_K
    rows = n_assign + N_EXPERTS * m
    e_flat = top_i.T.reshape(-1)
    counts = jnp.sum((e_flat[:, None] == jnp.arange(N_EXPERTS, dtype=jnp.int32)[None, :]).astype(jnp.int32), axis=0)
    padded = (counts + m - 1) // m * m
    ends = jnp.cumsum(padded)
    starts = ends - padded
    ustarts = jnp.cumsum(counts) - counts
    order = jnp.argsort(e_flat, stable=True).astype(jnp.int32)
    slot = jnp.arange(rows, dtype=jnp.int32)
    e_slot = jnp.minimum(jnp.sum((slot[:, None] >= ends[None, :]).astype(jnp.int32), axis=1), N_EXPERTS - 1)
    j = slot - starts[e_slot]
    valid = jnp.logical_and(j < counts[e_slot], slot < ends[-1])
    a = order[jnp.clip(ustarts[e_slot] + j, 0, n_assign - 1)]
    dst = jnp.where(valid, a, n_assign + slot).astype(jnp.int32)
    n_used = (ends[-1] // m).astype(jnp.int32)[None]
    return dst, e_slot[::m], n_used


def _moe(x2, g, r_hi, r_lo, wg, wu, wd, g_final, final_norm):
    xn, top_i, top_w = _router(x2, g, r_hi, r_lo)
    dst, blk_expert, n_used = _route(top_i[:, :TOP_K])
    y = _expert_ffn(xn, dst, blk_expert, n_used, wg, wu, wd)
    return _moe_combine(x2, y, top_w, g_final, final_norm)


def _pad_heads(w, n_heads, width, offset=0):
    r = w.shape[0]
    w = w.reshape(r, n_heads, width)
    w = jnp.pad(w, ((0, 0), (0, 0), (offset, LANES - width - offset)))
    return w.reshape(r, n_heads * LANES)


def _rope_tables(ang, base):
    half = ang.shape[1]
    cos = jnp.cos(ang)
    sin = jnp.sin(ang)
    pad = ((0, 0), (base, LANES - base - 2 * half))
    cos_t = jnp.pad(jnp.concatenate([cos, cos], axis=1) - 1.0, pad) + 1.0
    sin_t = jnp.pad(jnp.concatenate([-sin, sin], axis=1), pad)
    return cos_t, sin_t


def _rope_angles(pos, dim):
    inv = ROPE_THETA ** (-jnp.arange(0, dim, 2, dtype=F32) / dim)
    return pos.astype(F32)[:, None] * inv[None, :]


def _lane_gain(g):
    return jnp.pad(g.astype(F32), (0, LANES - g.shape[0]))[None, :]


def _trunk(x, p):
    bsz, seq, _ = x.shape
    x2 = x.reshape(bsz * seq, D_MODEL)
    n_rows = seq // GRID_W
    row = jnp.repeat(jnp.arange(n_rows), GRID_W)
    col = jnp.tile(jnp.arange(GRID_W), n_rows)
    ang_axial = jnp.concatenate([_rope_angles(row, HEAD_DIM // 2), _rope_angles(col, HEAD_DIM // 2)], axis=-1)
    cos_ax, sin_ax = _rope_tables(ang_axial, 0)
    cos_1d, sin_1d = _rope_tables(_rope_angles(jnp.arange(seq), MLA_ROPE_DIM), _MLA_ROPE_BASE)
    depth = p["ev_norm_mix"].shape[0] + p["od_norm_mix"].shape[0]
    assert depth % 2 == 0
    a_grp = A_HEADS // A_KV_HEADS
    for layer in range(depth):
        i = layer // 2
        last_layer = layer == depth - 1
        if layer % 2 == 0:
            aq, ak, avt, bq, bk, bvt = _even_in(
                x2, bsz, seq, p["ev_norm_mix"][i][None, :], p["ev_w_in"][i], cos_ax, sin_ax,
                p["b_q_norm"][i], p["b_k_norm"][i])
            ya = _window_attn(aq, ak, avt, p["a_slope"], p["a_sink"][i], bsz, seq, A_KV_HEADS, a_grp)
            yb = _dense_attn(bq, bk, bvt, bsz, seq, B_KV_HEADS, B_HEADS // B_KV_HEADS, 1, 256)
            x2 = _out_proj(x2, [ya, yb], [p["ev_w_out_a"][i], p["ev_w_out_b"][i]])
            x2 = _ffn(x2, p["ev_norm_ffn"][i][None, :], p["ffn_w_gate"][i], p["ffn_w_up"][i], p["ffn_w_down"][i])
        else:
            q, k, vt = _odd_in(
                x2, bsz, seq, p["od_norm_mix"][i][None, :], p["od_w_in"][i], p["mla_q_norm"][i][None, :],
                p["mla_w_q_up"][i], p["mla_kv_norm"][i][None, :], p["mla_w_k_up"][i], p["mla_w_v_up"][i],
                cos_1d, sin_1d)
            yc = _dense_attn(q, k, vt, bsz, seq, MLA_HEADS, 1, 2, 1024)
            x2 = _out_proj(x2, [yc], [p["od_w_out"][i]])
            x2 = _moe(x2, p["od_norm_ffn"][i][None, :], p["router_hi"][i], p["router_lo"][i],
                      p["moe_w_gate"][i], p["moe_w_up"][i], p["moe_w_down"][i],
                      p["final_norm"], last_layer)
    return x2.reshape(bsz, seq, D_MODEL)


def _prepare(ev_norm_mix, ev_w_in, a_sink, b_q_norm, b_k_norm, ev_w_out, ev_norm_ffn,
             ffn_w_gate, ffn_w_up, ffn_w_down, od_norm_mix, od_w_in, mla_q_norm, mla_w_q_up,
             mla_kv_norm, mla_w_kv_up, od_w_out, od_norm_ffn, moe_router, moe_w_gate, moe_w_up,
             moe_w_down, final_norm):
    hd = HEAD_DIM
    a_grp = A_HEADS // A_KV_HEADS

    def even_w_in(w):
        sizes = [A_HEADS * hd, A_KV_HEADS * hd, A_KV_HEADS * hd, B_HEADS * hd, B_KV_HEADS * hd, B_KV_HEADS * hd]
        aq, ak, av, bq, bk, bv = jnp.split(w, list(np.cumsum(sizes)[:-1]), axis=-1)
        return jnp.concatenate([_pad_heads(aq, A_HEADS, hd), _pad_heads(ak, A_KV_HEADS, hd),
                                _pad_heads(bq, B_HEADS, hd), _pad_heads(bk, B_KV_HEADS, hd), av, bv],
                               axis=-1).astype(BF16)

    def odd_w_in(w):
        c = w[:, :MLA_Q_RANK + MLA_KV_RANK]
        kr = _pad_heads(w[:, MLA_Q_RANK + MLA_KV_RANK:], 1, MLA_ROPE_DIM, _MLA_ROPE_BASE)
        return jnp.concatenate([c, kr], axis=-1).astype(BF16)

    def q_up(w):
        return _pad_heads(w, MLA_HEADS, MLA_QK_DIM).astype(BF16)

    def kv_up(w):
        w = w.reshape(MLA_KV_RANK, MLA_HEADS, MLA_NOPE_DIM + MLA_V_DIM)
        wk = _pad_heads(w[:, :, :MLA_NOPE_DIM].reshape(MLA_KV_RANK, -1), MLA_HEADS, MLA_NOPE_DIM)
        wv = w[:, :, MLA_NOPE_DIM:].reshape(MLA_KV_RANK, -1)
        return wk.astype(BF16), wv.astype(BF16)

    slopes = jnp.asarray(2.0 ** (-8.0 * np.arange(1, A_HEADS + 1) / A_HEADS), dtype=F32)
    per_col = lambda v: jnp.repeat(v.astype(F32).reshape(A_KV_HEADS, a_grp), WIN_Q, axis=1)[:, None, :] * LOG2E
    router = jnp.pad(moe_router.astype(F32), ((0, 0), (0, 0), (0, LANES - N_EXPERTS)))
    router_hi = router.astype(BF16)
    kv = [kv_up(w) for w in mla_w_kv_up]
    return {
        "ev_norm_mix": ev_norm_mix, "ev_w_in": jnp.stack([even_w_in(w) for w in ev_w_in]),
        "a_slope": per_col(slopes), "a_sink": jnp.stack([per_col(s) for s in a_sink]),
        "b_q_norm": jnp.stack([_lane_gain(g) for g in b_q_norm]),
        "b_k_norm": jnp.stack([_lane_gain(g) for g in b_k_norm]),
        "ev_w_out_a": ev_w_out[:, :A_HEADS * hd].astype(BF16),
        "ev_w_out_b": ev_w_out[:, A_HEADS * hd:].astype(BF16),
        "ev_norm_ffn": ev_norm_ffn,
        "ffn_w_gate": ffn_w_gate.astype(BF16), "ffn_w_up": ffn_w_up.astype(BF16),
        "ffn_w_down": ffn_w_down.astype(BF16),
        "od_norm_mix": od_norm_mix, "od_w_in": jnp.stack([odd_w_in(w) for w in od_w_in]),
        "mla_q_norm": mla_q_norm, "mla_w_q_up": jnp.stack([q_up(w) for w in mla_w_q_up]),
        "mla_kv_norm": mla_kv_norm,
        "mla_w_k_up": jnp.stack([a for a, _ in kv]), "mla_w_v_up": jnp.stack([b for _, b in kv]),
        "od_w_out": od_w_out.astype(BF16), "od_norm_ffn": od_norm_ffn,
        "router_hi": router_hi, "router_lo": (router - router_hi.astype(F32)).astype(BF16),
        "moe_w_gate": moe_w_gate.astype(BF16), "moe_w_up": moe_w_up.astype(BF16),
        "moe_w_down": moe_w_down.astype(BF16),
        "final_norm": final_norm[None, :],
    }


def kernel(x_prompt, x_sample, ev_norm_mix, ev_w_in, a_sink, b_q_norm, b_k_norm, ev_w_out, ev_norm_ffn, ffn_w_gate, ffn_w_up, ffn_w_down, od_norm_mix, od_w_in, mla_q_norm, mla_w_q_up, mla_kv_norm, mla_w_kv_up, od_w_out, od_norm_ffn, moe_router, moe_w_gate, moe_w_up, moe_w_down, final_norm):
    p = _prepare(ev_norm_mix, ev_w_in, a_sink, b_q_norm, b_k_norm, ev_w_out, ev_norm_ffn,
                 ffn_w_gate, ffn_w_up, ffn_w_down, od_norm_mix, od_w_in, mla_q_norm, mla_w_q_up,
                 mla_kv_norm, mla_w_kv_up, od_w_out, od_norm_ffn, moe_router, moe_w_gate, moe_w_up,
                 moe_w_down, final_norm)
    return (_trunk(x_prompt, p), _trunk(x_sample, p))
```

```python
import functools
import math

import numpy as np
import jax
import jax.numpy as jnp
from jax import lax
from jax.experimental import pallas as pl
from jax.experimental.pallas import tpu as pltpu

F32 = jnp.float32
BF16 = jnp.bfloat16

D_MODEL = 1024
HEAD_DIM = 64
WINDOW = 128
GRID_W = 64
ROPE_THETA = 10000.0
NORM_EPS = 1e-6
A_HEADS, A_KV_HEADS = 8, 2
B_HEADS, B_KV_HEADS = 8, 2
MLA_HEADS = 8
MLA_Q_RANK, MLA_KV_RANK = 256, 128
MLA_NOPE_DIM, MLA_ROPE_DIM, MLA_V_DIM = 64, 32, 64
MLA_QK_DIM = MLA_NOPE_DIM + MLA_ROPE_DIM
N_EXPERTS = 8
TOP_K = 2
LANES = 128
LOG2E = math.log2(math.e)
VMEM_LIMIT = 48 * 1024 * 1024

TOK_TILE = 512
FFN_TOK_TILE = 1024
FFN_F_TILE = 512
KEY_CHUNK = 512
WIN_Q = 128
MOE_ROW_TILE = 512
EXPERT_F_STEPS = 4
DMA_UNROLL = 8


def _rms(x, g):
    return x * lax.rsqrt(jnp.mean(x * x, axis=-1, keepdims=True) + NORM_EPS) * g


def _swap_halves(x, half):
    lane = lax.broadcasted_iota(jnp.int32, x.shape, 1)
    return jnp.where(lane < half, pltpu.roll(x, LANES - half, 1), pltpu.roll(x, half, 1))


def _swap_halves_at(x, base, half):
    lane = lax.broadcasted_iota(jnp.int32, x.shape, 1)
    return jnp.where(lane < base + half, pltpu.roll(x, LANES - half, 1), pltpu.roll(x, half, 1))


def _params(sem):
    return pltpu.CompilerParams(dimension_semantics=sem, vmem_limit_bytes=VMEM_LIMIT)


_EV_AQ, _EV_AK, _EV_BQ, _EV_BK, _EV_V, _EV_END = 0, 1024, 1280, 2304, 2560, 2816


def _even_in_kernel(x_ref, g_ref, w_ref, cos_ref, sin_ref, qg_ref, kg_ref,
                    aq_ref, ak_ref, avt_ref, bq_ref, bk_ref, bvt_ref):
    xb = _rms(x_ref[...], g_ref[...]).astype(BF16)
    cos = cos_ref[...]
    sin = sin_ref[...]
    qscale = (HEAD_DIM ** -0.5) * LOG2E

    def norm_rope(blk, gain):
        ms = jnp.sum(blk * blk, axis=-1, keepdims=True) * (1.0 / HEAD_DIM)
        y = blk * lax.rsqrt(ms + NORM_EPS) * gain
        return y * cos + _swap_halves(y, HEAD_DIM // 2) * sin

    pa = jnp.dot(xb, w_ref[:, _EV_AQ:_EV_AK], preferred_element_type=F32)
    aq_ref[...] = (pa * qscale).astype(BF16)
    ak_ref[...] = jnp.dot(xb, w_ref[:, _EV_AK:_EV_BQ], preferred_element_type=F32).astype(BF16)
    pbq = jnp.dot(xb, w_ref[:, _EV_BQ:_EV_BK], preferred_element_type=F32)
    for h in range(B_HEADS):
        sl = slice(h * LANES, (h + 1) * LANES)
        bq_ref[:, sl] = (norm_rope(pbq[:, sl], qg_ref[...]) * qscale).astype(BF16)
    pbk = jnp.dot(xb, w_ref[:, _EV_BK:_EV_V], preferred_element_type=F32)
    lane = lax.broadcasted_iota(jnp.int32, (pbk.shape[0], LANES), 1)
    for h in range(B_KV_HEADS):
        sl = slice(h * LANES, (h + 1) * LANES)
        bk_ref[:, sl] = jnp.where(lane == REF_LANE, 1.0, norm_rope(pbk[:, sl], kg_ref[...])).astype(BF16)
    pv = jnp.dot(xb, w_ref[:, _EV_V:_EV_END], preferred_element_type=F32)
    avt_ref[0] = pv[:, :LANES].T.astype(BF16)
    bvt_ref[0] = pv[:, LANES:].T.astype(BF16)


def _even_in(x2, bsz, seq, g, w_exp, cos_t, sin_t, qg, kg):
    tokens = x2.shape[0]
    tt = min(TOK_TILE, seq)
    assert seq % tt == 0
    ns = seq // tt
    row = lambda i: (i, 0)
    const = lambda i: (0, 0)
    tab = lambda i: (i % ns, 0)
    vt = lambda i: (i // ns, 0, i % ns)
    return pl.pallas_call(
        _even_in_kernel,
        grid=(tokens // tt,),
        in_specs=[
            pl.BlockSpec((tt, D_MODEL), row),
            pl.BlockSpec((1, D_MODEL), const),
            pl.BlockSpec((D_MODEL, _EV_END), const),
            pl.BlockSpec((tt, LANES), tab),
            pl.BlockSpec((tt, LANES), tab),
            pl.BlockSpec((1, LANES), const),
            pl.BlockSpec((1, LANES), const),
        ],
        out_specs=[
            pl.BlockSpec((tt, A_HEADS * LANES), row),
            pl.BlockSpec((tt, A_KV_HEADS * LANES), row),
            pl.BlockSpec((1, A_KV_HEADS * HEAD_DIM, tt), vt),
            pl.BlockSpec((tt, B_HEADS * LANES), row),
            pl.BlockSpec((tt, B_KV_HEADS * LANES), row),
            pl.BlockSpec((1, B_KV_HEADS * HEAD_DIM, tt), vt),
        ],
        out_shape=[
            jax.ShapeDtypeStruct((tokens, A_HEADS * LANES), BF16),
            jax.ShapeDtypeStruct((tokens, A_KV_HEADS * LANES), BF16),
            jax.ShapeDtypeStruct((bsz, A_KV_HEADS * HEAD_DIM, seq), BF16),
            jax.ShapeDtypeStruct((tokens, B_HEADS * LANES), BF16),
            jax.ShapeDtypeStruct((tokens, B_KV_HEADS * LANES), BF16),
            jax.ShapeDtypeStruct((bsz, B_KV_HEADS * HEAD_DIM, seq), BF16),
        ],
        compiler_params=_params(("arbitrary",)),
        name="even_in_proj",
    )(x2, g, w_exp, cos_t, sin_t, qg, kg)


_MLA_ROPE_BASE = MLA_NOPE_DIM


def _odd_in_kernel(x_ref, g_ref, w_ref, qn_ref, wq_ref, kvn_ref, wk_ref, wv_ref,
                   cos_ref, sin_ref, q_ref, k_ref, vt_ref):
    xb = _rms(x_ref[...], g_ref[...]).astype(BF16)
    cos = cos_ref[...]
    sin = sin_ref[...]
    qscale = (MLA_QK_DIM ** -0.5) * LOG2E

    def rope(blk):
        return blk * cos + _swap_halves_at(blk, _MLA_ROPE_BASE, MLA_ROPE_DIM // 2) * sin

    proj = jnp.dot(xb, w_ref[...], preferred_element_type=F32)
    cq = _rms(proj[:, :MLA_Q_RANK], qn_ref[...]).astype(BF16)
    ckv = _rms(proj[:, MLA_Q_RANK:MLA_Q_RANK + MLA_KV_RANK], kvn_ref[...]).astype(BF16)
    kr = rope(proj[:, MLA_Q_RANK + MLA_KV_RANK:])
    q = jnp.dot(cq, wq_ref[...], preferred_element_type=F32)
    kn = jnp.dot(ckv, wk_ref[...], preferred_element_type=F32)
    lane = lax.broadcasted_iota(jnp.int32, (kr.shape[0], LANES), 1)
    for h in range(MLA_HEADS):
        sl = slice(h * LANES, (h + 1) * LANES)
        q_ref[:, sl] = (rope(q[:, sl]) * qscale).astype(BF16)
        k_ref[:, sl] = jnp.where(lane == REF_LANE, 1.0, kn[:, sl] + kr).astype(BF16)
    v = jnp.dot(ckv, wv_ref[...], preferred_element_type=F32)
    for c in range(MLA_HEADS * MLA_V_DIM // LANES):
        vt_ref[0, c * LANES:(c + 1) * LANES, :] = v[:, c * LANES:(c + 1) * LANES].T.astype(BF16)


def _odd_in(x2, bsz, seq, g, w_exp, qn, wq_exp, kvn, wk_exp, wv, cos_t, sin_t):
    tokens = x2.shape[0]
    tt = min(TOK_TILE, seq)
    assert seq % tt == 0
    ns = seq // tt
    row = lambda i: (i, 0)
    const = lambda i: (0, 0)
    tab = lambda i: (i % ns, 0)
    vt = lambda i: (i // ns, 0, i % ns)
    hl = MLA_HEADS * LANES
    hv = MLA_HEADS * MLA_V_DIM
    return pl.pallas_call(
        _odd_in_kernel,
        grid=(tokens // tt,),
        in_specs=[
            pl.BlockSpec((tt, D_MODEL), row),
            pl.BlockSpec((1, D_MODEL), const),
            pl.BlockSpec((D_MODEL, 4 * LANES), const),
            pl.BlockSpec((1, MLA_Q_RANK), const),
            pl.BlockSpec((MLA_Q_RANK, hl), const),
            pl.BlockSpec((1, MLA_KV_RANK), const),
            pl.BlockSpec((MLA_KV_RANK, hl), const),
            pl.BlockSpec((MLA_KV_RANK, hv), const),
            pl.BlockSpec((tt, LANES), tab),
            pl.BlockSpec((tt, LANES), tab),
        ],
        out_specs=[
            pl.BlockSpec((tt, hl), row),
            pl.BlockSpec((tt, hl), row),
            pl.BlockSpec((1, hv, tt), vt),
        ],
        out_shape=[
            jax.ShapeDtypeStruct((tokens, hl), BF16),
            jax.ShapeDtypeStruct((tokens, hl), BF16),
            jax.ShapeDtypeStruct((bsz, hv, seq), BF16),
        ],
        compiler_params=_params(("arbitrary",)),
        name="odd_in_proj",
    )(x2, g, w_exp, qn, wq_exp, kvn, wk_exp, wv, cos_t, sin_t)


def _pair_transpose(pieces):
    outs = []
    for a in range(0, len(pieces), 2):
        outs.append(jnp.concatenate([pieces[a], pieces[a + 1]], axis=0).T)
    return outs[0] if len(outs) == 1 else jnp.concatenate(outs, axis=1)


SUM_ROWS = 16
REF_LANE = LANES - 1
CHUNK_UNROLL = 4
JUMP_LIMIT = 64.0


def _bf16_round(x):
    return x.astype(BF16).astype(F32)


def _dense_attn_kernel(q_ref, k_ref, vt_ref, o_ref, qt_ref, acc_ref, ob_ref, *, kb, grp, tq, tk, seq):
    n = grp * tq
    dv = HEAD_DIM
    n_chunks = seq // tk
    ones = jnp.ones((SUM_ROWS, tk), BF16)
    row16 = lax.broadcasted_iota(jnp.int32, (16, n), 0)
    pieces = []
    for j in range(kb):
        qs = jnp.concatenate(
            [q_ref[:, (j * grp + g) * LANES:(j * grp + g + 1) * LANES] for g in range(grp)], axis=0)
        qt = qs.astype(F32).T.astype(BF16)
        qt_ref[0] = qt
        qt_ref[1] = qt

        def keys(c, j=j):
            off = pl.multiple_of(c * tk, tk)
            return k_ref[pl.ds(off, tk), j * LANES:(j + 1) * LANES]

        def values(c, j=j):
            off = pl.multiple_of(c * tk, tk)
            return jnp.concatenate([vt_ref[0, j * dv:(j + 1) * dv, pl.ds(off, tk)], ones], axis=0)

        def set_reference(buf, ref):
            qt_ref[buf, LANES - 16:, :] = jnp.where(row16 == 15, -ref, 0.0).astype(BF16)

        def chunk(c, buf, ref, ref_acc, run, jump_max):
            set_reference(buf, ref)
            t = jnp.dot(keys(c), qt_ref[buf], preferred_element_type=F32)
            jump = jnp.max(t, axis=0, keepdims=True)
            p = jnp.exp2(t.astype(BF16))
            pv = jnp.dot(values(c), p, preferred_element_type=F32)
            acc_ref[...] = jnp.exp2(ref_acc - ref) * acc_ref[...] + pv
            return jnp.maximum(run, ref + jump), jnp.maximum(jump_max, jump)

        run0 = jnp.max(jnp.dot(keys(0), qt, preferred_element_type=F32), axis=0, keepdims=True)
        ref0 = _bf16_round(run0)
        acc_ref[...] = jnp.zeros((dv + SUM_ROWS, n), F32)

        def group(ci, carry, chunk=chunk):
            ref_a, ref_b, ref_acc, run, jump_max = carry
            for u in range(CHUNK_UNROLL):
                run, jump_max = chunk(ci * CHUNK_UNROLL + u, u % 2, ref_a, ref_acc, run, jump_max)
                ref_a, ref_b, ref_acc = ref_b, _bf16_round(run), ref_a
            return ref_a, ref_b, ref_acc, run, jump_max

        init = (ref0, ref0, ref0, run0, jnp.zeros((1, n), F32))
        jump_max = lax.fori_loop(0, n_chunks // CHUNK_UNROLL, group, init)[4]
        acc = acc_ref[...]
        ob_ref[j] = acc[:dv] / acc[dv:dv + 1]

        @pl.when(jnp.max(jump_max) > JUMP_LIMIT)
        def _(j=j, qt=qt, keys=keys, values=values):
            acc_ref[...] = jnp.zeros((dv + SUM_ROWS, n), F32)

            def exact(c, m_old):
                s = jnp.dot(keys(c), qt, preferred_element_type=F32)
                m_new = jnp.maximum(m_old, jnp.max(s, axis=0, keepdims=True))
                p = jnp.exp2((s - m_new).astype(BF16))
                pv = jnp.dot(values(c), p, preferred_element_type=F32)
                acc_ref[...] = jnp.exp2(m_old - m_new) * acc_ref[...] + pv
                return m_new

            lax.fori_loop(0, n_chunks, exact, jnp.full((1, n), -jnp.inf, F32))
            acc = acc_ref[...]
            ob_ref[j] = acc[:dv] / acc[dv:dv + 1]

        o = ob_ref[j]
        for g in range(grp):
            pieces.append(o[:, g * tq:(g + 1) * tq])
    o_ref[...] = _pair_transpose(pieces).astype(o_ref.dtype)


def _dense_attn(q, k, vt, bsz, seq, n_kv, grp, kb, tq):
    tokens = q.shape[0]
    tq = min(tq, seq)
    tk = min(KEY_CHUNK, seq // CHUNK_UNROLL)
    n_chunks = seq // tk
    assert seq % tq == 0 and seq % tk == 0 and n_chunks % CHUNK_UNROLL == 0
    assert n_kv % kb == 0 and (kb * grp) % 2 == 0
    nq = seq // tq
    n = grp * tq
    kern = functools.partial(_dense_attn_kernel, kb=kb, grp=grp, tq=tq, tk=tk, seq=seq)
    return pl.pallas_call(
        kern,
        grid=(bsz, n_kv // kb, nq),
        in_specs=[
            pl.BlockSpec((tq, kb * grp * LANES), lambda b, h, i: (b * nq + i, h)),
            pl.BlockSpec((seq, kb * LANES), lambda b, h, i: (b, h)),
            pl.BlockSpec((1, kb * HEAD_DIM, seq), lambda b, h, i: (b, h, 0)),
        ],
        out_specs=pl.BlockSpec((tq, kb * grp * HEAD_DIM), lambda b, h, i: (b * nq + i, h)),
        out_shape=jax.ShapeDtypeStruct((tokens, n_kv * grp * HEAD_DIM), BF16),
        scratch_shapes=[
            pltpu.VMEM((2, LANES, n), BF16),
            pltpu.VMEM((HEAD_DIM + SUM_ROWS, n), F32),
            pltpu.VMEM((kb, HEAD_DIM, n), F32),
        ],
        compiler_params=_params(("arbitrary", "arbitrary", "arbitrary")),
        name="dense_attn",
    )(q, k, vt)


def _window_attn_kernel(q_ref, k_ref, vt_ref, slope_ref, sink_ref, o_ref, *, grp, seq):
    tq = WIN_Q
    span = 3 * WIN_Q
    n = grp * tq
    i = pl.program_id(2)
    w0 = pl.multiple_of(jnp.clip((i - 1) * tq, 0, seq - span), tq)
    qs = jnp.concatenate([q_ref[:, g * LANES:(g + 1) * LANES] for g in range(grp)], axis=0)
    kc = k_ref[pl.ds(w0, span), :]
    s = lax.dot_general(kc, qs, (((1,), (1,)), ((), ())), preferred_element_type=F32)
    kpos = w0 + lax.broadcasted_iota(jnp.int32, (span, n), 0)
    qpos = i * tq + lax.broadcasted_iota(jnp.int32, (span, n), 1) % tq
    dist = jnp.abs(kpos - qpos)
    s = s - slope_ref[0] * dist.astype(F32)
    s = jnp.where(dist <= WINDOW, s, -jnp.inf)
    sink = sink_ref[0]
    m = jnp.maximum(jnp.max(s, axis=0, keepdims=True), sink)
    e = jnp.exp2(s - m)
    denom = jnp.sum(e, axis=0, keepdims=True) + jnp.exp2(sink - m)
    vc = vt_ref[0, :, pl.ds(w0, span)]
    o = jnp.dot(vc, e.astype(BF16), preferred_element_type=F32) / denom
    o_ref[...] = _pair_transpose([o[:, g * tq:(g + 1) * tq] for g in range(grp)]).astype(o_ref.dtype)


def _window_attn(q, k, vt, slope_row, sink_row, bsz, seq, n_kv, grp):
    tokens = q.shape[0]
    tq = WIN_Q
    assert seq % tq == 0 and seq >= 3 * tq
    nq = seq // tq
    n = grp * tq
    kern = functools.partial(_window_attn_kernel, grp=grp, seq=seq)
    return pl.pallas_call(
        kern,
        grid=(bsz, n_kv, nq),
        in_specs=[
            pl.BlockSpec((tq, grp * LANES), lambda b, h, i: (b * nq + i, h)),
            pl.BlockSpec((seq, LANES), lambda b, h, i: (b, h)),
            pl.BlockSpec((1, HEAD_DIM, seq), lambda b, h, i: (b, h, 0)),
            pl.BlockSpec((1, 1, n), lambda b, h, i: (h, 0, 0)),
            pl.BlockSpec((1, 1, n), lambda b, h, i: (h, 0, 0)),
        ],
        out_specs=pl.BlockSpec((tq, grp * HEAD_DIM), lambda b, h, i: (b * nq + i, h)),
        out_shape=jax.ShapeDtypeStruct((tokens, n_kv * grp * HEAD_DIM), BF16),
        compiler_params=_params(("arbitrary", "arbitrary", "arbitrary")),
        name="window_attn",
    )(q, k, vt, slope_row, sink_row)


def _out_proj_kernel(*refs, n_in):
    x_ref = refs[0]
    y_refs = refs[1:1 + n_in]
    w_refs = refs[1 + n_in:1 + 2 * n_in]
    o_ref = refs[1 + 2 * n_in]
    acc = x_ref[...]
    for y_ref, w_ref in zip(y_refs, w_refs):
        acc = acc + jnp.dot(y_ref[...], w_ref[...], preferred_element_type=F32)
    o_ref[...] = acc


def _out_proj(x2, ys, ws):
    tokens = x2.shape[0]
    tt = min(TOK_TILE, tokens)
    row = lambda i: (i, 0)
    const = lambda i: (0, 0)
    kern = functools.partial(_out_proj_kernel, n_in=len(ys))
    return pl.pallas_call(
        kern,
        grid=(tokens // tt,),
        in_specs=[pl.BlockSpec((tt, D_MODEL), row)]
        + [pl.BlockSpec((tt, y.shape[1]), row) for y in ys]
        + [pl.BlockSpec(w.shape, const) for w in ws],
        out_specs=pl.BlockSpec((tt, D_MODEL), row),
        out_shape=jax.ShapeDtypeStruct((tokens, D_MODEL), F32),
        compiler_params=_params(("arbitrary",)),
        name="out_proj",
    )(x2, *ys, *ws)


def _swiglu_partial(xb, wg, wu, wd):
    hg = jnp.dot(xb, wg, preferred_element_type=F32)
    hu = jnp.dot(xb, wu, preferred_element_type=F32)
    h = hg * jax.nn.sigmoid(hg) * hu
    return jnp.dot(h.astype(BF16), wd, preferred_element_type=F32)


def _ffn_kernel(x_ref, g_ref, wg_ref, wu_ref, wd_ref, o_ref, xn_ref, acc_ref):
    f = pl.program_id(1)

    @pl.when(f == 0)
    def _():
        xn_ref[...] = _rms(x_ref[...], g_ref[...]).astype(BF16)
        acc_ref[...] = jnp.zeros_like(acc_ref)

    acc_ref[...] += _swiglu_partial(xn_ref[...], wg_ref[...], wu_ref[...], wd_ref[...])

    @pl.when(f == pl.num_programs(1) - 1)
    def _():
        o_ref[...] = x_ref[...] + acc_ref[...]


def _ffn(x2, g, wg, wu, wd):
    tokens = x2.shape[0]
    ff = wg.shape[1]
    tt = min(FFN_TOK_TILE, tokens)
    fc = FFN_F_TILE
    assert tokens % tt == 0 and ff % fc == 0
    row = lambda i, f: (i, 0)
    return pl.pallas_call(
        _ffn_kernel,
        grid=(tokens // tt, ff // fc),
        in_specs=[
            pl.BlockSpec((tt, D_MODEL), row),
            pl.BlockSpec((1, D_MODEL), lambda i, f: (0, 0)),
            pl.BlockSpec((D_MODEL, fc), lambda i, f: (0, f)),
            pl.BlockSpec((D_MODEL, fc), lambda i, f: (0, f)),
            pl.BlockSpec((fc, D_MODEL), lambda i, f: (f, 0)),
        ],
        out_specs=pl.BlockSpec((tt, D_MODEL), row),
        out_shape=jax.ShapeDtypeStruct((tokens, D_MODEL), F32),
        scratch_shapes=[pltpu.VMEM((tt, D_MODEL), BF16), pltpu.VMEM((tt, D_MODEL), F32)],
        compiler_params=_params(("arbitrary", "arbitrary")),
        name="ffn",
    )(x2, g, wg, wu, wd)


def _router_kernel(x_ref, g_ref, rh_ref, rl_ref, xn_ref, ti_ref, tw_ref):
    tt = x_ref.shape[0]
    xn = _rms(x_ref[...], g_ref[...])
    xn_ref[...] = xn
    xh = xn.astype(BF16)
    xl = (xn - xh.astype(F32)).astype(BF16)
    logits = (jnp.dot(xh, rh_ref[...], preferred_element_type=F32)
              + jnp.dot(xh, rl_ref[...], preferred_element_type=F32)
              + jnp.dot(xl, rh_ref[...], preferred_element_type=F32))
    lane = lax.broadcasted_iota(jnp.int32, (tt, LANES), 1).astype(F32)
    lg = jnp.where(lane < N_EXPERTS, logits, -jnp.inf)
    m1 = jnp.max(lg, axis=1, keepdims=True)
    i1 = jnp.min(jnp.where(lg == m1, lane, float(LANES)), axis=1, keepdims=True)
    lg2 = jnp.where(lane == i1, -jnp.inf, lg)
    m2 = jnp.max(lg2, axis=1, keepdims=True)
    i2 = jnp.min(jnp.where(lg2 == m2, lane, float(LANES)), axis=1, keepdims=True)
    e2 = jnp.exp(m2 - m1)
    w1 = 1.0 / (1.0 + e2)
    w2 = e2 * w1
    ti_ref[...] = jnp.where(lane == 0.0, i1, i2).astype(jnp.int32)
    tw_ref[...] = jnp.where(lane == 0.0, w1, w2)


def _router(x2, g, r_hi, r_lo):
    tokens = x2.shape[0]
    tt = min(TOK_TILE, tokens)
    row = lambda i: (i, 0)
    const = lambda i: (0, 0)
    return pl.pallas_call(
        _router_kernel,
        grid=(tokens // tt,),
        in_specs=[
            pl.BlockSpec((tt, D_MODEL), row),
            pl.BlockSpec((1, D_MODEL), const),
            pl.BlockSpec((D_MODEL, LANES), const),
            pl.BlockSpec((D_MODEL, LANES), const),
        ],
        out_specs=[
            pl.BlockSpec((tt, D_MODEL), row),
            pl.BlockSpec((tt, LANES), row),
            pl.BlockSpec((tt, LANES), row),
        ],
        out_shape=[
            jax.ShapeDtypeStruct((tokens, D_MODEL), F32),
            jax.ShapeDtypeStruct((tokens, LANES), jnp.int32),
            jax.ShapeDtypeStruct((tokens, LANES), F32),
        ],
        compiler_params=_params(("arbitrary",)),
        name="moe_router",
    )(x2, g, r_hi, r_lo)


def _for_rows(n_rows, fn):
    def group(u, carry):
        for v in range(DMA_UNROLL):
            fn(u * DMA_UNROLL + v)
        return carry
    lax.fori_loop(0, n_rows // DMA_UNROLL, group, 0)


def _expert_ffn_kernel(be_ref, nu_ref, dst_ref, xn_hbm, wg_ref, wu_ref, wd_ref, y_hbm,
                       xg_ref, xb_ref, acc_ref, yb_ref, gsem, ssem, *, tokens):
    b = pl.program_id(0)
    f = pl.program_id(1)
    nf = pl.num_programs(1)
    nu = nu_ref[0]
    used = b < nu
    last = f == nf - 1
    slot = b % 2
    m = MOE_ROW_TILE
    per_step = m // EXPERT_F_STEPS
    nxt = jnp.minimum(b + 1, nu - 1)

    def gather(blk, r, sl):
        tok = lax.rem(dst_ref[blk * m + r], tokens)
        return pltpu.make_async_copy(xn_hbm.at[pl.ds(tok, 1)], xg_ref.at[sl, pl.ds(r, 1)], gsem.at[sl])

    def scatter(blk, r):
        return pltpu.make_async_copy(yb_ref.at[pl.ds(r, 1)], y_hbm.at[pl.ds(dst_ref[blk * m + r], 1)], ssem)

    @pl.when(jnp.logical_and(used, jnp.logical_and(f == 0, b == 0)))
    def _():
        _for_rows(m, lambda r: gather(0, r, 0).start())

    @pl.when(jnp.logical_and(used, f == 0))
    def _():
        _for_rows(m, lambda r: gather(b, r, slot).wait())
        xb_ref[...] = xg_ref[slot].astype(BF16)
        acc_ref[...] = jnp.zeros_like(acc_ref)

    def step(with_scatter):
        for v in range(per_step):
            r = f * per_step + v
            gather(nxt, r, 1 - slot).start()
            if with_scatter:
                scatter(b - 1, r).start()
        acc_ref[...] += _swiglu_partial(xb_ref[...], wg_ref[0], wu_ref[0], wd_ref[0])

    @pl.when(jnp.logical_and(used, b == 0))
    def _():
        step(False)

    @pl.when(jnp.logical_and(used, b > 0))
    def _():
        step(True)

    @pl.when(jnp.logical_and(used, jnp.logical_and(last, b > 0)))
    def _():
        _for_rows(m, lambda r: scatter(b - 1, r).wait())

    @pl.when(jnp.logical_and(used, last))
    def _():
        yb_ref[...] = acc_ref[...]

    @pl.when(jnp.logical_and(last, b == nu - 1))
    def _():
        _for_rows(m, lambda r: scatter(b, r).start())
        _for_rows(m, lambda r: gather(nxt, r, 1 - slot).wait())
        _for_rows(m, lambda r: scatter(b, r).wait())


def _expert_ffn(xn, dst, blk_expert, n_used, wg, wu, wd):
    tokens = xn.shape[0]
    rows = dst.shape[0]
    ff = wg.shape[2]
    m = MOE_ROW_TILE
    assert rows % m == 0 and ff % EXPERT_F_STEPS == 0 and m % EXPERT_F_STEPS == 0
    fc = ff // EXPERT_F_STEPS
    assert fc % LANES == 0
    return pl.pallas_call(
        functools.partial(_expert_ffn_kernel, tokens=tokens),
        grid_spec=pltpu.PrefetchScalarGridSpec(
            num_scalar_prefetch=3,
            grid=(rows // m, EXPERT_F_STEPS),
            in_specs=[
                pl.BlockSpec(memory_space=pl.ANY),
                pl.BlockSpec((1, D_MODEL, fc), lambda b, f, be, nu, ds: (be[b], 0, f)),
                pl.BlockSpec((1, D_MODEL, fc), lambda b, f, be, nu, ds: (be[b], 0, f)),
                pl.BlockSpec((1, fc, D_MODEL), lambda b, f, be, nu, ds: (be[b], f, 0)),
            ],
            out_specs=pl.BlockSpec(memory_space=pl.ANY),
            scratch_shapes=[
                pltpu.VMEM((2, m, D_MODEL), F32),
                pltpu.VMEM((m, D_MODEL), BF16),
                pltpu.VMEM((m, D_MODEL), F32),
                pltpu.VMEM((m, D_MODEL), F32),
                pltpu.SemaphoreType.DMA((2,)),
                pltpu.SemaphoreType.DMA,
            ],
        ),
        out_shape=jax.ShapeDtypeStruct((TOP_K * tokens + rows, D_MODEL), F32),
        compiler_params=_params(("arbitrary", "arbitrary")),
        name="expert_ffn",
    )(blk_expert, n_used, dst, xn, wg, wu, wd)


def _moe_combine_kernel(x_ref, y0_ref, y1_ref, tw_ref, gf_ref, o_ref, *, final_norm):
    tw = tw_ref[...]
    y = x_ref[...] + tw[:, 0:1] * y0_ref[...] + tw[:, 1:2] * y1_ref[...]
    if final_norm:
        y = _rms(y, gf_ref[...])
    o_ref[...] = y


def _moe_combine(x2, y, top_w, g_final, final_norm):
    tokens = x2.shape[0]
    tt = min(TOK_TILE, tokens)
    nt = tokens // tt
    row = lambda i: (i, 0)
    return pl.pallas_call(
        functools.partial(_moe_combine_kernel, final_norm=final_norm),
        grid=(nt,),
        in_specs=[
            pl.BlockSpec((tt, D_MODEL), row),
            pl.BlockSpec((tt, D_MODEL), row),
            pl.BlockSpec((tt, D_MODEL), lambda i: (nt + i, 0)),
            pl.BlockSpec((tt, LANES), row),
            pl.BlockSpec((1, D_MODEL), lambda i: (0, 0)),
        ],
        out_specs=pl.BlockSpec((tt, D_MODEL), row),
        out_shape=jax.ShapeDtypeStruct((tokens, D_MODEL), F32),
        compiler_params=_params(("arbitrary",)),
        name="moe_combine",
    )(x2, y, y, top_w, g_final)


def _route(top_i):
    m = MOE_ROW_TILE
    tokens = top_i.shape[0]
    n_assign = tokens * TOP_K
    rows = n_assign + N_EXPERTS * m
    e_flat = top_i.T.reshape(-1)
    counts = jnp.sum((e_flat[:, None] == jnp.arange(N_EXPERTS, dtype=jnp.int32)[None, :]).astype(jnp.int32), axis=0)
    padded = (counts + m - 1) // m * m
    ends = jnp.cumsum(padded)
    starts = ends - padded
    ustarts = jnp.cumsum(counts) - counts
    order = jnp.argsort(e_flat, stable=True).astype(jnp.int32)
    slot = jnp.arange(rows, dtype=jnp.int32)
    e_slot = jnp.minimum(jnp.sum((slot[:, None] >= ends[None, :]).astype(jnp.int32), axis=1), N_EXPERTS - 1)
    j = slot - starts[e_slot]
    valid = jnp.logical_and(j < counts[e_slot], slot < ends[-1])
    a = order[jnp.clip(ustarts[e_slot] + j, 0, n_assign - 1)]
    dst = jnp.where(valid, a, n_assign + slot).astype(jnp.int32)
    n_used = (ends[-1] // m).astype(jnp.int32)[None]
    return dst, e_slot[::m], n_used


def _moe(x2, g, r_hi, r_lo, wg, wu, wd, g_final, final_norm):
    xn, top_i, top_w = _router(x2, g, r_hi, r_lo)
    dst, blk_expert, n_used = _route(top_i[:, :TOP_K])
    y = _expert_ffn(xn, dst, blk_expert, n_used, wg, wu, wd)
    return _moe_combine(x2, y, top_w, g_final, final_norm)


def _pad_heads(w, n_heads, width, offset=0):
    r = w.shape[0]
    w = w.reshape(r, n_heads, width)
    w = jnp.pad(w, ((0, 0), (0, 0), (offset, LANES - width - offset)))
    return w.reshape(r, n_heads * LANES)


def _rope_tables(ang, base):
    half = ang.shape[1]
    cos = jnp.cos(ang)
    sin = jnp.sin(ang)
    pad = ((0, 0), (base, LANES - base - 2 * half))
    cos_t = jnp.pad(jnp.concatenate([cos, cos], axis=1) - 1.0, pad) + 1.0
    sin_t = jnp.pad(jnp.concatenate([-sin, sin], axis=1), pad)
    return cos_t, sin_t


def _rope_angles(pos, dim):
    inv = ROPE_THETA ** (-jnp.arange(0, dim, 2, dtype=F32) / dim)
    return pos.astype(F32)[:, None] * inv[None, :]


def _lane_gain(g):
    return jnp.pad(g.astype(F32), (0, LANES - g.shape[0]))[None, :]


def _trunk(x, p):
    bsz, seq, _ = x.shape
    x2 = x.reshape(bsz * seq, D_MODEL)
    n_rows = seq // GRID_W
    row = jnp.repeat(jnp.arange(n_rows), GRID_W)
    col = jnp.tile(jnp.arange(GRID_W), n_rows)
    ang_axial = jnp.concatenate([_rope_angles(row, HEAD_DIM // 2), _rope_angles(col, HEAD_DIM // 2)], axis=-1)
    cos_ax, sin_ax = _rope_tables(ang_axial, 0)
    cos_1d, sin_1d = _rope_tables(_rope_angles(jnp.arange(seq), MLA_ROPE_DIM), _MLA_ROPE_BASE)
    depth = p["ev_norm_mix"].shape[0] + p["od_norm_mix"].shape[0]
    assert depth % 2 == 0
    a_grp = A_HEADS // A_KV_HEADS
    for layer in range(depth):
        i = layer // 2
        last_layer = layer == depth - 1
        if layer % 2 == 0:
            aq, ak, avt, bq, bk, bvt = _even_in(
                x2, bsz, seq, p["ev_norm_mix"][i][None, :], p["ev_w_in"][i], cos_ax, sin_ax,
                p["b_q_norm"][i], p["b_k_norm"][i])
            ya = _window_attn(aq, ak, avt, p["a_slope"], p["a_sink"][i], bsz, seq, A_KV_HEADS, a_grp)
            yb = _dense_attn(bq, bk, bvt, bsz, seq, B_KV_HEADS, B_HEADS // B_KV_HEADS, 1, 256)
            x2 = _out_proj(x2, [ya, yb], [p["ev_w_out_a"][i], p["ev_w_out_b"][i]])
            x2 = _ffn(x2, p["ev_norm_ffn"][i][None, :], p["ffn_w_gate"][i], p["ffn_w_up"][i], p["ffn_w_down"][i])
        else:
            q, k, vt = _odd_in(
                x2, bsz, seq, p["od_norm_mix"][i][None, :], p["od_w_in"][i], p["mla_q_norm"][i][None, :],
                p["mla_w_q_up"][i], p["mla_kv_norm"][i][None, :], p["mla_w_k_up"][i], p["mla_w_v_up"][i],
                cos_1d, sin_1d)
            yc = _dense_attn(q, k, vt, bsz, seq, MLA_HEADS, 1, 2, 1024)
            x2 = _out_proj(x2, [yc], [p["od_w_out"][i]])
            x2 = _moe(x2, p["od_norm_ffn"][i][None, :], p["router_hi"][i], p["router_lo"][i],
                      p["moe_w_gate"][i], p["moe_w_up"][i], p["moe_w_down"][i],
                      p["final_norm"], last_layer)
    return x2.reshape(bsz, seq, D_MODEL)


def _prepare(ev_norm_mix, ev_w_in, a_sink, b_q_norm, b_k_norm, ev_w_out, ev_norm_ffn,
             ffn_w_gate, ffn_w_up, ffn_w_down, od_norm_mix, od_w_in, mla_q_norm, mla_w_q_up,
             mla_kv_norm, mla_w_kv_up, od_w_out, od_norm_ffn, moe_router, moe_w_gate, moe_w_up,
             moe_w_down, final_norm):
    hd = HEAD_DIM
    a_grp = A_HEADS // A_KV_HEADS

    def even_w_in(w):
        sizes = [A_HEADS * hd, A_KV_HEADS * hd, A_KV_HEADS * hd, B_HEADS * hd, B_KV_HEADS * hd, B_KV_HEADS * hd]
        aq, ak, av, bq, bk, bv = jnp.split(w, list(np.cumsum(sizes)[:-1]), axis=-1)
        return jnp.concatenate([_pad_heads(aq, A_HEADS, hd), _pad_heads(ak, A_KV_HEADS, hd),
                                _pad_heads(bq, B_HEADS, hd), _pad_heads(bk, B_KV_HEADS, hd), av, bv],
                               axis=-1).astype(BF16)

    def odd_w_in(w):
        c = w[:, :MLA_Q_RANK + MLA_KV_RANK]
        kr = _pad_heads(w[:, MLA_Q_RANK + MLA_KV_RANK:], 1, MLA_ROPE_DIM, _MLA_ROPE_BASE)
        return jnp.concatenate([c, kr], axis=-1).astype(BF16)

    def q_up(w):
        return _pad_heads(w, MLA_HEADS, MLA_QK_DIM).astype(BF16)

    def kv_up(w):
        w = w.reshape(MLA_KV_RANK, MLA_HEADS, MLA_NOPE_DIM + MLA_V_DIM)
        wk = _pad_heads(w[:, :, :MLA_NOPE_DIM].reshape(MLA_KV_RANK, -1), MLA_HEADS, MLA_NOPE_DIM)
        wv = w[:, :, MLA_NOPE_DIM:].reshape(MLA_KV_RANK, -1)
        return wk.astype(BF16), wv.astype(BF16)

    slopes = jnp.asarray(2.0 ** (-8.0 * np.arange(1, A_HEADS + 1) / A_HEADS), dtype=F32)
    per_col = lambda v: jnp.repeat(v.astype(F32).reshape(A_KV_HEADS, a_grp), WIN_Q, axis=1)[:, None, :] * LOG2E
    router = jnp.pad(moe_router.astype(F32), ((0, 0), (0, 0), (0, LANES - N_EXPERTS)))
    router_hi = router.astype(BF16)
    kv = [kv_up(w) for w in mla_w_kv_up]
    return {
        "ev_norm_mix": ev_norm_mix, "ev_w_in": jnp.stack([even_w_in(w) for w in ev_w_in]),
        "a_slope": per_col(slopes), "a_sink": jnp.stack([per_col(s) for s in a_sink]),
        "b_q_norm": jnp.stack([_lane_gain(g) for g in b_q_norm]),
        "b_k_norm": jnp.stack([_lane_gain(g) for g in b_k_norm]),
        "ev_w_out_a": ev_w_out[:, :A_HEADS * hd].astype(BF16),
        "ev_w_out_b": ev_w_out[:, A_HEADS * hd:].astype(BF16),
        "ev_norm_ffn": ev_norm_ffn,
        "ffn_w_gate": ffn_w_gate.astype(BF16), "ffn_w_up": ffn_w_up.astype(BF16),
        "ffn_w_down": ffn_w_down.astype(BF16),
        "od_norm_mix": od_norm_mix, "od_w_in": jnp.stack([odd_w_in(w) for w in od_w_in]),
        "mla_q_norm": mla_q_norm, "mla_w_q_up": jnp.stack([q_up(w) for w in mla_w_q_up]),
        "mla_kv_norm": mla_kv_norm,
        "mla_w_k_up": jnp.stack([a for a, _ in kv]), "mla_w_v_up": jnp.stack([b for _, b in kv]),
        "od_w_out": od_w_out.astype(BF16), "od_norm_ffn": od_norm_ffn,
        "router_hi": router_hi, "router_lo": (router - router_hi.astype(F32)).astype(BF16),
        "moe_w_gate": moe_w_gate.astype(BF16), "moe_w_up": moe_w_up.astype(BF16),
        "moe_w_down": moe_w_down.astype(BF16),
        "final_norm": final_norm[None, :],
    }


def kernel(x_prompt, x_sample, ev_norm_mix, ev_w_in, a_sink, b_q_norm, b_k_norm, ev_w_out, ev_norm_ffn, ffn_w_gate, ffn_w_up, ffn_w_down, od_norm_mix, od_w_in, mla_q_norm, mla_w_q_up, mla_kv_norm, mla_w_kv_up, od_w_out, od_norm_ffn, moe_router, moe_w_gate, moe_w_up, moe_w_down, final_norm):
    p = _prepare(ev_norm_mix, ev_w_in, a_sink, b_q_norm, b_k_norm, ev_w_out, ev_norm_ffn,
                 ffn_w_gate, ffn_w_up, ffn_w_down, od_norm_mix, od_w_in, mla_q_norm, mla_w_q_up,
                 mla_kv_norm, mla_w_kv_up, od_w_out, od_norm_ffn, moe_router, moe_w_gate, moe_w_up,
                 moe_w_down, final_norm)
    return (_trunk(x_prompt, p), _trunk(x_sample, p))
```

```python
import functools
import math

import numpy as np
import jax
import jax.numpy as jnp
from jax import lax
from jax.experimental import pallas as pl
from jax.experimental.pallas import tpu as pltpu

F32 = jnp.float32
BF16 = jnp.bfloat16

D_MODEL = 1024
HEAD_DIM = 64
WINDOW = 128
GRID_W = 64
ROPE_THETA = 10000.0
NORM_EPS = 1e-6
A_HEADS, A_KV_HEADS = 8, 2
B_HEADS, B_KV_HEADS = 8, 2
MLA_HEADS = 8
MLA_Q_RANK, MLA_KV_RANK = 256, 128
MLA_NOPE_DIM, MLA_ROPE_DIM, MLA_V_DIM = 64, 32, 64
MLA_QK_DIM = MLA_NOPE_DIM + MLA_ROPE_DIM
N_EXPERTS = 8
TOP_K = 2
LANES = 128
LOG2E = math.log2(math.e)
VMEM_LIMIT = 48 * 1024 * 1024

TOK_TILE = 512
FFN_TOK_TILE = 1024
FFN_F_TILE = 512
KEY_CHUNK = 512
WIN_Q = 128
WIN_BLOCKS = 4
MOE_ROW_TILE = 448
EXPERT_F_STEPS = 7
DMA_UNROLL = 8


def _rms(x, g):
    return x * lax.rsqrt(jnp.mean(x * x, axis=-1, keepdims=True) + NORM_EPS) * g


def _swap_halves(x, half):
    lane = lax.broadcasted_iota(jnp.int32, x.shape, 1)
    return jnp.where(lane < half, pltpu.roll(x, LANES - half, 1), pltpu.roll(x, half, 1))


def _swap_halves_at(x, base, half):
    lane = lax.broadcasted_iota(jnp.int32, x.shape, 1)
    return jnp.where(lane < base + half, pltpu.roll(x, LANES - half, 1), pltpu.roll(x, half, 1))


def _params(sem):
    return pltpu.CompilerParams(dimension_semantics=sem, vmem_limit_bytes=VMEM_LIMIT)


_EV_AQ, _EV_AK, _EV_BQ, _EV_BK, _EV_V, _EV_END = 0, 1024, 1280, 2304, 2560, 2816


def _even_in_kernel(x_ref, g_ref, w_ref, cos_ref, sin_ref, qg_ref, kg_ref,
                    aq_ref, ak_ref, avt_ref, bq_ref, bk_ref, bvt_ref):
    xb = _rms(x_ref[...], g_ref[...]).astype(BF16)
    cos = cos_ref[...]
    sin = sin_ref[...]
    qscale = (HEAD_DIM ** -0.5) * LOG2E

    def norm_rope(blk, gain):
        ms = jnp.sum(blk * blk, axis=-1, keepdims=True) * (1.0 / HEAD_DIM)
        y = blk * lax.rsqrt(ms + NORM_EPS) * gain
        return y * cos + _swap_halves(y, HEAD_DIM // 2) * sin

    pa = jnp.dot(xb, w_ref[:, _EV_AQ:_EV_AK], preferred_element_type=F32)
    aq_ref[...] = (pa * qscale).astype(BF16)
    ak_ref[...] = jnp.dot(xb, w_ref[:, _EV_AK:_EV_BQ], preferred_element_type=F32).astype(BF16)
    pbq = jnp.dot(xb, w_ref[:, _EV_BQ:_EV_BK], preferred_element_type=F32)
    for h in range(B_HEADS):
        sl = slice(h * LANES, (h + 1) * LANES)
        bq_ref[:, sl] = (norm_rope(pbq[:, sl], qg_ref[...]) * qscale).astype(BF16)
    pbk = jnp.dot(xb, w_ref[:, _EV_BK:_EV_V], preferred_element_type=F32)
    lane = lax.broadcasted_iota(jnp.int32, (pbk.shape[0], LANES), 1)
    for h in range(B_KV_HEADS):
        sl = slice(h * LANES, (h + 1) * LANES)
        bk_ref[:, sl] = jnp.where(lane == REF_LANE, 1.0, norm_rope(pbk[:, sl], kg_ref[...])).astype(BF16)
    pv = jnp.dot(xb, w_ref[:, _EV_V:_EV_END], preferred_element_type=F32)
    avt_ref[0] = pv[:, :LANES].T.astype(BF16)
    bvt_ref[0] = pv[:, LANES:].T.astype(BF16)


def _even_in(x2, bsz, seq, g, w_exp, cos_t, sin_t, qg, kg):
    tokens = x2.shape[0]
    tt = min(TOK_TILE, seq)
    assert seq % tt == 0
    ns = seq // tt
    row = lambda i: (i, 0)
    const = lambda i: (0, 0)
    tab = lambda i: (i % ns, 0)
    vt = lambda i: (i // ns, 0, i % ns)
    return pl.pallas_call(
        _even_in_kernel,
        grid=(tokens // tt,),
        in_specs=[
            pl.BlockSpec((tt, D_MODEL), row),
            pl.BlockSpec((1, D_MODEL), const),
            pl.BlockSpec((D_MODEL, _EV_END), const),
            pl.BlockSpec((tt, LANES), tab),
            pl.BlockSpec((tt, LANES), tab),
            pl.BlockSpec((1, LANES), const),
            pl.BlockSpec((1, LANES), const),
        ],
        out_specs=[
            pl.BlockSpec((tt, A_HEADS * LANES), row),
            pl.BlockSpec((tt, A_KV_HEADS * LANES), row),
            pl.BlockSpec((1, A_KV_HEADS * HEAD_DIM, tt), vt),
            pl.BlockSpec((tt, B_HEADS * LANES), row),
            pl.BlockSpec((tt, B_KV_HEADS * LANES), row),
            pl.BlockSpec((1, B_KV_HEADS * HEAD_DIM, tt), vt),
        ],
        out_shape=[
            jax.ShapeDtypeStruct((tokens, A_HEADS * LANES), BF16),
            jax.ShapeDtypeStruct((tokens, A_KV_HEADS * LANES), BF16),
            jax.ShapeDtypeStruct((bsz, A_KV_HEADS * HEAD_DIM, seq), BF16),
            jax.ShapeDtypeStruct((tokens, B_HEADS * LANES), BF16),
            jax.ShapeDtypeStruct((tokens, B_KV_HEADS * LANES), BF16),
            jax.ShapeDtypeStruct((bsz, B_KV_HEADS * HEAD_DIM, seq), BF16),
        ],
        compiler_params=_params(("arbitrary",)),
        name="even_in_proj",
    )(x2, g, w_exp, cos_t, sin_t, qg, kg)


_MLA_ROPE_BASE = MLA_NOPE_DIM


def _odd_in_kernel(x_ref, g_ref, w_ref, qn_ref, wq_ref, kvn_ref, wk_ref, wv_ref,
                   cos_ref, sin_ref, q_ref, k_ref, vt_ref):
    xb = _rms(x_ref[...], g_ref[...]).astype(BF16)
    cos = cos_ref[...]
    sin = sin_ref[...]
    qscale = (MLA_QK_DIM ** -0.5) * LOG2E

    def rope(blk):
        return blk * cos + _swap_halves_at(blk, _MLA_ROPE_BASE, MLA_ROPE_DIM // 2) * sin

    proj = jnp.dot(xb, w_ref[...], preferred_element_type=F32)
    cq = _rms(proj[:, :MLA_Q_RANK], qn_ref[...]).astype(BF16)
    ckv = _rms(proj[:, MLA_Q_RANK:MLA_Q_RANK + MLA_KV_RANK], kvn_ref[...]).astype(BF16)
    kr = rope(proj[:, MLA_Q_RANK + MLA_KV_RANK:])
    q = jnp.dot(cq, wq_ref[...], preferred_element_type=F32)
    kn = jnp.dot(ckv, wk_ref[...], preferred_element_type=F32)
    lane = lax.broadcasted_iota(jnp.int32, (kr.shape[0], LANES), 1)
    for h in range(MLA_HEADS):
        sl = slice(h * LANES, (h + 1) * LANES)
        q_ref[:, sl] = (rope(q[:, sl]) * qscale).astype(BF16)
        k_ref[:, sl] = jnp.where(lane == REF_LANE, 1.0, kn[:, sl] + kr).astype(BF16)
    v = jnp.dot(ckv, wv_ref[...], preferred_element_type=F32)
    for c in range(MLA_HEADS * MLA_V_DIM // LANES):
        vt_ref[0, c * LANES:(c + 1) * LANES, :] = v[:, c * LANES:(c + 1) * LANES].T.astype(BF16)


def _odd_in(x2, bsz, seq, g, w_exp, qn, wq_exp, kvn, wk_exp, wv, cos_t, sin_t):
    tokens = x2.shape[0]
    tt = min(TOK_TILE, seq)
    assert seq % tt == 0
    ns = seq // tt
    row = lambda i: (i, 0)
    const = lambda i: (0, 0)
    tab = lambda i: (i % ns, 0)
    vt = lambda i: (i // ns, 0, i % ns)
    hl = MLA_HEADS * LANES
    hv = MLA_HEADS * MLA_V_DIM
    return pl.pallas_call(
        _odd_in_kernel,
        grid=(tokens // tt,),
        in_specs=[
            pl.BlockSpec((tt, D_MODEL), row),
            pl.BlockSpec((1, D_MODEL), const),
            pl.BlockSpec((D_MODEL, 4 * LANES), const),
            pl.BlockSpec((1, MLA_Q_RANK), const),
            pl.BlockSpec((MLA_Q_RANK, hl), const),
            pl.BlockSpec((1, MLA_KV_RANK), const),
            pl.BlockSpec((MLA_KV_RANK, hl), const),
            pl.BlockSpec((MLA_KV_RANK, hv), const),
            pl.BlockSpec((tt, LANES), tab),
            pl.BlockSpec((tt, LANES), tab),
        ],
        out_specs=[
            pl.BlockSpec((tt, hl), row),
            pl.BlockSpec((tt, hl), row),
            pl.BlockSpec((1, hv, tt), vt),
        ],
        out_shape=[
            jax.ShapeDtypeStruct((tokens, hl), BF16),
            jax.ShapeDtypeStruct((tokens, hl), BF16),
            jax.ShapeDtypeStruct((bsz, hv, seq), BF16),
        ],
        compiler_params=_params(("arbitrary",)),
        name="odd_in_proj",
    )(x2, g, w_exp, qn, wq_exp, kvn, wk_exp, wv, cos_t, sin_t)


def _pair_transpose(pieces):
    outs = []
    for a in range(0, len(pieces), 2):
        outs.append(jnp.concatenate([pieces[a], pieces[a + 1]], axis=0).T)
    return outs[0] if len(outs) == 1 else jnp.concatenate(outs, axis=1)


SUM_ROWS = 16
REF_LANE = LANES - 1
CHUNK_UNROLL = 4
JUMP_LIMIT = 64.0


def _bf16_round(x):
    return x.astype(BF16).astype(F32)


def _dense_attn_kernel(q_ref, k_ref, vt_ref, o_ref, qt_ref, acc_ref, ob_ref, *, kb, grp, tq, tk, seq):
    n = grp * tq
    dv = HEAD_DIM
    n_chunks = seq // tk
    ones = jnp.ones((SUM_ROWS, tk), BF16)
    row16 = lax.broadcasted_iota(jnp.int32, (16, n), 0)
    pieces = []
    for j in range(kb):
        qs = jnp.concatenate(
            [q_ref[:, (j * grp + g) * LANES:(j * grp + g + 1) * LANES] for g in range(grp)], axis=0)
        qt = qs.astype(F32).T.astype(BF16)
        qt_ref[0] = qt
        qt_ref[1] = qt

        def keys(c, j=j):
            off = pl.multiple_of(c * tk, tk)
            return k_ref[pl.ds(off, tk), j * LANES:(j + 1) * LANES]

        def values(c, j=j):
            off = pl.multiple_of(c * tk, tk)
            return jnp.concatenate([vt_ref[0, j * dv:(j + 1) * dv, pl.ds(off, tk)], ones], axis=0)

        def set_reference(buf, ref):
            qt_ref[buf, LANES - 16:, :] = jnp.where(row16 == 15, -ref, 0.0).astype(BF16)

        def chunk(c, buf, ref, ref_acc, run, jump_max):
            set_reference(buf, ref)
            t = jnp.dot(keys(c), qt_ref[buf], preferred_element_type=F32)
            jump = jnp.max(t, axis=0, keepdims=True)
            p = jnp.exp2(t.astype(BF16))
            pv = jnp.dot(values(c), p, preferred_element_type=F32)
            acc_ref[...] = jnp.exp2(ref_acc - ref) * acc_ref[...] + pv
            return jnp.maximum(run, ref + jump), jnp.maximum(jump_max, jump)

        run0 = jnp.max(jnp.dot(keys(0), qt, preferred_element_type=F32), axis=0, keepdims=True)
        ref0 = _bf16_round(run0)
        acc_ref[...] = jnp.zeros((dv + SUM_ROWS, n), F32)

        def group(ci, carry, chunk=chunk):
            ref_a, ref_b, ref_acc, run, jump_max = carry
            for u in range(CHUNK_UNROLL):
                run, jump_max = chunk(ci * CHUNK_UNROLL + u, u % 2, ref_a, ref_acc, run, jump_max)
                ref_a, ref_b, ref_acc = ref_b, _bf16_round(run), ref_a
            return ref_a, ref_b, ref_acc, run, jump_max

        init = (ref0, ref0, ref0, run0, jnp.zeros((1, n), F32))
        jump_max = lax.fori_loop(0, n_chunks // CHUNK_UNROLL, group, init)[4]
        acc = acc_ref[...]
        ob_ref[j] = acc[:dv] / acc[dv:dv + 1]

        @pl.when(jnp.max(jump_max) > JUMP_LIMIT)
        def _(j=j, qt=qt, keys=keys, values=values):
            acc_ref[...] = jnp.zeros((dv + SUM_ROWS, n), F32)

            def exact(c, m_old):
                s = jnp.dot(keys(c), qt, preferred_element_type=F32)
                m_new = jnp.maximum(m_old, jnp.max(s, axis=0, keepdims=True))
                p = jnp.exp2((s - m_new).astype(BF16))
                pv = jnp.dot(values(c), p, preferred_element_type=F32)
                acc_ref[...] = jnp.exp2(m_old - m_new) * acc_ref[...] + pv
                return m_new

            lax.fori_loop(0, n_chunks, exact, jnp.full((1, n), -jnp.inf, F32))
            acc = acc_ref[...]
            ob_ref[j] = acc[:dv] / acc[dv:dv + 1]

        o = ob_ref[j]
        for g in range(grp):
            pieces.append(o[:, g * tq:(g + 1) * tq])
    o_ref[...] = _pair_transpose(pieces).astype(o_ref.dtype)


def _dense_attn(q, k, vt, bsz, seq, n_kv, grp, kb, tq):
    tokens = q.shape[0]
    tq = min(tq, seq)
    tk = min(KEY_CHUNK, seq // CHUNK_UNROLL)
    n_chunks = seq // tk
    assert seq % tq == 0 and seq % tk == 0 and n_chunks % CHUNK_UNROLL == 0
    assert n_kv % kb == 0 and (kb * grp) % 2 == 0
    nq = seq // tq
    n = grp * tq
    kern = functools.partial(_dense_attn_kernel, kb=kb, grp=grp, tq=tq, tk=tk, seq=seq)
    return pl.pallas_call(
        kern,
        grid=(bsz, n_kv // kb, nq),
        in_specs=[
            pl.BlockSpec((tq, kb * grp * LANES), lambda b, h, i: (b * nq + i, h)),
            pl.BlockSpec((seq, kb * LANES), lambda b, h, i: (b, h)),
            pl.BlockSpec((1, kb * HEAD_DIM, seq), lambda b, h, i: (b, h, 0)),
        ],
        out_specs=pl.BlockSpec((tq, kb * grp * HEAD_DIM), lambda b, h, i: (b * nq + i, h)),
        out_shape=jax.ShapeDtypeStruct((tokens, n_kv * grp * HEAD_DIM), BF16),
        scratch_shapes=[
            pltpu.VMEM((2, LANES, n), BF16),
            pltpu.VMEM((HEAD_DIM + SUM_ROWS, n), F32),
            pltpu.VMEM((kb, HEAD_DIM, n), F32),
        ],
        compiler_params=_params(("arbitrary", "arbitrary", "arbitrary")),
        name="dense_attn",
    )(q, k, vt)


def _window_attn_kernel(q_ref, k_ref, vt_ref, slope_ref, sink_ref, o_ref, *, grp, seq):
    tq = WIN_Q
    span = 3 * WIN_Q
    n = grp * tq
    for u in range(WIN_BLOCKS):
        i = pl.program_id(2) * WIN_BLOCKS + u
        w0 = pl.multiple_of(jnp.clip((i - 1) * tq, 0, seq - span), tq)
        rows = slice(u * tq, (u + 1) * tq)
        qs = jnp.concatenate([q_ref[rows, g * LANES:(g + 1) * LANES] for g in range(grp)], axis=0)
        kc = k_ref[pl.ds(w0, span), :]
        s = lax.dot_general(kc, qs, (((1,), (1,)), ((), ())), preferred_element_type=F32)
        kpos = w0 + lax.broadcasted_iota(jnp.int32, (span, n), 0)
        qpos = i * tq + lax.broadcasted_iota(jnp.int32, (span, n), 1) % tq
        dist = jnp.abs(kpos - qpos)
        s = s - slope_ref[0] * dist.astype(F32)
        s = jnp.where(dist <= WINDOW, s, -jnp.inf)
        sink = sink_ref[0]
        m = jnp.maximum(jnp.max(s, axis=0, keepdims=True), sink)
        e = jnp.exp2(s - m)
        denom = jnp.sum(e, axis=0, keepdims=True) + jnp.exp2(sink - m)
        vc = vt_ref[0, :, pl.ds(w0, span)]
        o = jnp.dot(vc, e.astype(BF16), preferred_element_type=F32) / denom
        o_ref[rows, :] = _pair_transpose([o[:, g * tq:(g + 1) * tq] for g in range(grp)]).astype(o_ref.dtype)


def _window_attn(q, k, vt, slope_row, sink_row, bsz, seq, n_kv, grp):
    tokens = q.shape[0]
    tq = WIN_Q * WIN_BLOCKS
    assert seq % tq == 0 and seq >= 3 * WIN_Q
    nq = seq // tq
    n = grp * WIN_Q
    kern = functools.partial(_window_attn_kernel, grp=grp, seq=seq)
    return pl.pallas_call(
        kern,
        grid=(bsz, n_kv, nq),
        in_specs=[
            pl.BlockSpec((tq, grp * LANES), lambda b, h, i: (b * nq + i, h)),
            pl.BlockSpec((seq, LANES), lambda b, h, i: (b, h)),
            pl.BlockSpec((1, HEAD_DIM, seq), lambda b, h, i: (b, h, 0)),
            pl.BlockSpec((1, 1, n), lambda b, h, i: (h, 0, 0)),
            pl.BlockSpec((1, 1, n), lambda b, h, i: (h, 0, 0)),
        ],
        out_specs=pl.BlockSpec((tq, grp * HEAD_DIM), lambda b, h, i: (b * nq + i, h)),
        out_shape=jax.ShapeDtypeStruct((tokens, n_kv * grp * HEAD_DIM), BF16),
        compiler_params=_params(("arbitrary", "arbitrary", "arbitrary")),
        name="window_attn",
    )(q, k, vt, slope_row, sink_row)


def _out_proj_kernel(*refs, n_in):
    x_ref = refs[0]
    y_refs = refs[1:1 + n_in]
    w_refs = refs[1 + n_in:1 + 2 * n_in]
    o_ref = refs[1 + 2 * n_in]
    acc = x_ref[...]
    for y_ref, w_ref in zip(y_refs, w_refs):
        acc = acc + jnp.dot(y_ref[...], w_ref[...], preferred_element_type=F32)
    o_ref[...] = acc


def _out_proj(x2, ys, ws):
    tokens = x2.shape[0]
    tt = min(TOK_TILE, tokens)
    row = lambda i: (i, 0)
    const = lambda i: (0, 0)
    kern = functools.partial(_out_proj_kernel, n_in=len(ys))
    return pl.pallas_call(
        kern,
        grid=(tokens // tt,),
        in_specs=[pl.BlockSpec((tt, D_MODEL), row)]
        + [pl.BlockSpec((tt, y.shape[1]), row) for y in ys]
        + [pl.BlockSpec(w.shape, const) for w in ws],
        out_specs=pl.BlockSpec((tt, D_MODEL), row),
        out_shape=jax.ShapeDtypeStruct((tokens, D_MODEL), F32),
        compiler_params=_params(("arbitrary",)),
        name="out_proj",
    )(x2, *ys, *ws)


def _swiglu_partial(xb, wg, wu, wd):
    hg = jnp.dot(xb, wg, preferred_element_type=F32)
    hu = jnp.dot(xb, wu, preferred_element_type=F32)
    h = hg * jax.nn.sigmoid(hg) * hu
    return jnp.dot(h.astype(BF16), wd, preferred_element_type=F32)


def _ffn_kernel(x_ref, g_ref, wg_ref, wu_ref, wd_ref, o_ref, xn_ref, acc_ref):
    f = pl.program_id(1)

    @pl.when(f == 0)
    def _():
        xn_ref[...] = _rms(x_ref[...], g_ref[...]).astype(BF16)
        acc_ref[...] = jnp.zeros_like(acc_ref)

    acc_ref[...] += _swiglu_partial(xn_ref[...], wg_ref[...], wu_ref[...], wd_ref[...])

    @pl.when(f == pl.num_programs(1) - 1)
    def _():
        o_ref[...] = x_ref[...] + acc_ref[...]


def _ffn(x2, g, wg, wu, wd):
    tokens = x2.shape[0]
    ff = wg.shape[1]
    tt = min(FFN_TOK_TILE, tokens)
    fc = FFN_F_TILE
    assert tokens % tt == 0 and ff % fc == 0
    row = lambda i, f: (i, 0)
    return pl.pallas_call(
        _ffn_kernel,
        grid=(tokens // tt, ff // fc),
        in_specs=[
            pl.BlockSpec((tt, D_MODEL), row),
            pl.BlockSpec((1, D_MODEL), lambda i, f: (0, 0)),
            pl.BlockSpec((D_MODEL, fc), lambda i, f: (0, f)),
            pl.BlockSpec((D_MODEL, fc), lambda i, f: (0, f)),
            pl.BlockSpec((fc, D_MODEL), lambda i, f: (f, 0)),
        ],
        out_specs=pl.BlockSpec((tt, D_MODEL), row),
        out_shape=jax.ShapeDtypeStruct((tokens, D_MODEL), F32),
        scratch_shapes=[pltpu.VMEM((tt, D_MODEL), BF16), pltpu.VMEM((tt, D_MODEL), F32)],
        compiler_params=_params(("arbitrary", "arbitrary")),
        name="ffn",
    )(x2, g, wg, wu, wd)


def _router_kernel(x_ref, g_ref, rh_ref, rl_ref, xn_ref, ti_ref, tw_ref):
    tt = x_ref.shape[0]
    xn = _rms(x_ref[...], g_ref[...])
    xn_ref[...] = xn
    xh = xn.astype(BF16)
    xl = (xn - xh.astype(F32)).astype(BF16)
    logits = (jnp.dot(xh, rh_ref[...], preferred_element_type=F32)
              + jnp.dot(xh, rl_ref[...], preferred_element_type=F32)
              + jnp.dot(xl, rh_ref[...], preferred_element_type=F32))
    lane = lax.broadcasted_iota(jnp.int32, (tt, LANES), 1).astype(F32)
    lg = jnp.where(lane < N_EXPERTS, logits, -jnp.inf)
    m1 = jnp.max(lg, axis=1, keepdims=True)
    i1 = jnp.min(jnp.where(lg == m1, lane, float(LANES)), axis=1, keepdims=True)
    lg2 = jnp.where(lane == i1, -jnp.inf, lg)
    m2 = jnp.max(lg2, axis=1, keepdims=True)
    i2 = jnp.min(jnp.where(lg2 == m2, lane, float(LANES)), axis=1, keepdims=True)
    e2 = jnp.exp(m2 - m1)
    w1 = 1.0 / (1.0 + e2)
    w2 = e2 * w1
    ti_ref[...] = jnp.where(lane == 0.0, i1, i2).astype(jnp.int32)
    tw_ref[...] = jnp.where(lane == 0.0, w1, w2)


def _router(x2, g, r_hi, r_lo):
    tokens = x2.shape[0]
    tt = min(TOK_TILE, tokens)
    row = lambda i: (i, 0)
    const = lambda i: (0, 0)
    return pl.pallas_call(
        _router_kernel,
        grid=(tokens // tt,),
        in_specs=[
            pl.BlockSpec((tt, D_MODEL), row),
            pl.BlockSpec((1, D_MODEL), const),
            pl.BlockSpec((D_MODEL, LANES), const),
            pl.BlockSpec((D_MODEL, LANES), const),
        ],
        out_specs=[
            pl.BlockSpec((tt, D_MODEL), row),
            pl.BlockSpec((tt, LANES), row),
            pl.BlockSpec((tt, LANES), row),
        ],
        out_shape=[
            jax.ShapeDtypeStruct((tokens, D_MODEL), F32),
            jax.ShapeDtypeStruct((tokens, LANES), jnp.int32),
            jax.ShapeDtypeStruct((tokens, LANES), F32),
        ],
        compiler_params=_params(("arbitrary",)),
        name="moe_router",
    )(x2, g, r_hi, r_lo)


def _for_rows(n_rows, fn):
    def group(u, carry):
        for v in range(DMA_UNROLL):
            fn(u * DMA_UNROLL + v)
        return carry
    lax.fori_loop(0, n_rows // DMA_UNROLL, group, 0)


def _expert_ffn_kernel(be_ref, nu_ref, dst_ref, xn_hbm, wg_ref, wu_ref, wd_ref, y_hbm,
                       xg_ref, xb_ref, acc_ref, yb_ref, gsem, ssem, *, tokens):
    b = pl.program_id(0)
    f = pl.program_id(1)
    nf = pl.num_programs(1)
    nu = nu_ref[0]
    used = b < nu
    last = f == nf - 1
    slot = b % 2
    m = MOE_ROW_TILE
    per_step = m // EXPERT_F_STEPS
    nxt = jnp.minimum(b + 1, nu - 1)

    def gather(blk, r, sl):
        tok = lax.rem(dst_ref[blk * m + r], tokens)
        return pltpu.make_async_copy(xn_hbm.at[pl.ds(tok, 1)], xg_ref.at[sl, pl.ds(r, 1)], gsem.at[sl])

    def scatter(blk, r):
        return pltpu.make_async_copy(yb_ref.at[pl.ds(r, 1)], y_hbm.at[pl.ds(dst_ref[blk * m + r], 1)], ssem)

    @pl.when(jnp.logical_and(used, jnp.logical_and(f == 0, b == 0)))
    def _():
        _for_rows(m, lambda r: gather(0, r, 0).start())

    @pl.when(jnp.logical_and(used, f == 0))
    def _():
        _for_rows(m, lambda r: gather(b, r, slot).wait())
        xb_ref[...] = xg_ref[slot].astype(BF16)
        acc_ref[...] = jnp.zeros_like(acc_ref)

    def step(with_scatter):
        for v in range(per_step):
            r = f * per_step + v
            gather(nxt, r, 1 - slot).start()
            if with_scatter:
                scatter(b - 1, r).start()
        acc_ref[...] += _swiglu_partial(xb_ref[...], wg_ref[0], wu_ref[0], wd_ref[0])

    @pl.when(jnp.logical_and(used, b == 0))
    def _():
        step(False)

    @pl.when(jnp.logical_and(used, b > 0))
    def _():
        step(True)

    @pl.when(jnp.logical_and(used, jnp.logical_and(last, b > 0)))
    def _():
        _for_rows(m, lambda r: scatter(b - 1, r).wait())

    @pl.when(jnp.logical_and(used, last))
    def _():
        yb_ref[...] = acc_ref[...]

    @pl.when(jnp.logical_and(last, b == nu - 1))
    def _():
        _for_rows(m, lambda r: scatter(b, r).start())
        _for_rows(m, lambda r: gather(nxt, r, 1 - slot).wait())
        _for_rows(m, lambda r: scatter(b, r).wait())


def _expert_ffn(xn, dst, blk_expert, n_used, wg, wu, wd):
    tokens = xn.shape[0]
    rows = dst.shape[0]
    ff = wg.shape[2]
    m = MOE_ROW_TILE
    assert rows % m == 0 and ff % EXPERT_F_STEPS == 0 and m % EXPERT_F_STEPS == 0
    fc = ff // EXPERT_F_STEPS
    assert fc % LANES == 0
    return pl.pallas_call(
        functools.partial(_expert_ffn_kernel, tokens=tokens),
        grid_spec=pltpu.PrefetchScalarGridSpec(
            num_scalar_prefetch=3,
            grid=(rows // m, EXPERT_F_STEPS),
            in_specs=[
                pl.BlockSpec(memory_space=pl.ANY),
                pl.BlockSpec((1, D_MODEL, fc), lambda b, f, be, nu, ds: (be[b], 0, f)),
                pl.BlockSpec((1, D_MODEL, fc), lambda b, f, be, nu, ds: (be[b], 0, f)),
                pl.BlockSpec((1, fc, D_MODEL), lambda b, f, be, nu, ds: (be[b], f, 0)),
            ],
            out_specs=pl.BlockSpec(memory_space=pl.ANY),
            scratch_shapes=[
                pltpu.VMEM((2, m, D_MODEL), F32),
                pltpu.VMEM((m, D_MODEL), BF16),
                pltpu.VMEM((m, D_MODEL), F32),
                pltpu.VMEM((m, D_MODEL), F32),
                pltpu.SemaphoreType.DMA((2,)),
                pltpu.SemaphoreType.DMA,
            ],
        ),
        out_shape=jax.ShapeDtypeStruct((TOP_K * tokens + rows, D_MODEL), F32),
        compiler_params=_params(("arbitrary", "arbitrary")),
        name="expert_ffn",
    )(blk_expert, n_used, dst, xn, wg, wu, wd)


def _moe_combine_kernel(x_ref, y0_ref, y1_ref, tw_ref, gf_ref, o_ref, *, final_norm):
    tw = tw_ref[...]
    y = x_ref[...] + tw[:, 0:1] * y0_ref[...] + tw[:, 1:2] * y1_ref[...]
    if final_norm:
        y = _rms(y, gf_ref[...])
    o_ref[...] = y


def _moe_combine(x2, y, top_w, g_final, final_norm):
    tokens = x2.shape[0]
    tt = min(TOK_TILE, tokens)
    nt = tokens // tt
    row = lambda i: (i, 0)
    return pl.pallas_call(
        functools.partial(_moe_combine_kernel, final_norm=final_norm),
        grid=(nt,),
        in_specs=[
            pl.BlockSpec((tt, D_MODEL), row),
            pl.BlockSpec((tt, D_MODEL), row),
            pl.BlockSpec((tt, D_MODEL), lambda i: (nt + i, 0)),
            pl.BlockSpec((tt, LANES), row),
            pl.BlockSpec((1, D_MODEL), lambda i: (0, 0)),
        ],
        out_specs=pl.BlockSpec((tt, D_MODEL), row),
        out_shape=jax.ShapeDtypeStruct((tokens, D_MODEL), F32),
        compiler_params=_params(("arbitrary",)),
        name="moe_combine",
    )(x2, y, y, top_w, g_final)


def _route(top_i):
    m = MOE_ROW_TILE
    tokens = top_i.shape[0]
    n_assign = tokens * TOP_K
    rows = -(-(n_assign + N_EXPERTS * m) // m) * m
    e_flat = top_i.T.reshape(-1)
    counts = jnp.sum((e_flat[:, None] == jnp.arange(N_EXPERTS, dtype=jnp.int32)[None, :]).astype(jnp.int32), axis=0)
    padded = (counts + m - 1) // m * m
    ends = jnp.cumsum(padded)
    starts = ends - padded
    ustarts = jnp.cumsum(counts) - counts
    order = jnp.argsort(e_flat, stable=True).astype(jnp.int32)
    slot = jnp.arange(rows, dtype=jnp.int32)
    e_slot = jnp.minimum(jnp.sum((slot[:, None] >= ends[None, :]).astype(jnp.int32), axis=1), N_EXPERTS - 1)
    j = slot - starts[e_slot]
    valid = jnp.logical_and(j < counts[e_slot], slot < ends[-1])
    a = order[jnp.clip(ustarts[e_slot] + j, 0, n_assign - 1)]
    dst = jnp.where(valid, a, n_assign + slot).astype(jnp.int32)
    n_used = (ends[-1] // m).astype(jnp.int32)[None]
    return dst, e_slot[::m], n_used


def _moe(x2, g, r_hi, r_lo, wg, wu, wd, g_final, final_norm):
    xn, top_i, top_w = _router(x2, g, r_hi, r_lo)
    dst, blk_expert, n_used = _route(top_i[:, :TOP_K])
    y = _expert_ffn(xn, dst, blk_expert, n_used, wg, wu, wd)
    return _moe_combine(x2, y, top_w, g_final, final_norm)


def _pad_heads(w, n_heads, width, offset=0):
    r = w.shape[0]
    w = w.reshape(r, n_heads, width)
    w = jnp.pad(w, ((0, 0), (0, 0), (offset, LANES - width - offset)))
    return w.reshape(r, n_heads * LANES)


def _rope_tables(ang, base):
    half = ang.shape[1]
    cos = jnp.cos(ang)
    sin = jnp.sin(ang)
    pad = ((0, 0), (base, LANES - base - 2 * half))
    cos_t = jnp.pad(jnp.concatenate([cos, cos], axis=1) - 1.0, pad) + 1.0
    sin_t = jnp.pad(jnp.concatenate([-sin, sin], axis=1), pad)
    return cos_t, sin_t


def _rope_angles(pos, dim):
    inv = ROPE_THETA ** (-jnp.arange(0, dim, 2, dtype=F32) / dim)
    return pos.astype(F32)[:, None] * inv[None, :]


def _lane_gain(g):
    return jnp.pad(g.astype(F32), (0, LANES - g.shape[0]))[None, :]


def _trunk(x, p):
    bsz, seq, _ = x.shape
    x2 = x.reshape(bsz * seq, D_MODEL)
    n_rows = seq // GRID_W
    row = jnp.repeat(jnp.arange(n_rows), GRID_W)
    col = jnp.tile(jnp.arange(GRID_W), n_rows)
    ang_axial = jnp.concatenate([_rope_angles(row, HEAD_DIM // 2), _rope_angles(col, HEAD_DIM // 2)], axis=-1)
    cos_ax, sin_ax = _rope_tables(ang_axial, 0)
    cos_1d, sin_1d = _rope_tables(_rope_angles(jnp.arange(seq), MLA_ROPE_DIM), _MLA_ROPE_BASE)
    depth = p["ev_norm_mix"].shape[0] + p["od_norm_mix"].shape[0]
    assert depth % 2 == 0
    a_grp = A_HEADS // A_KV_HEADS
    for layer in range(depth):
        i = layer // 2
        last_layer = layer == depth - 1
        if layer % 2 == 0:
            aq, ak, avt, bq, bk, bvt = _even_in(
                x2, bsz, seq, p["ev_norm_mix"][i][None, :], p["ev_w_in"][i], cos_ax, sin_ax,
                p["b_q_norm"][i], p["b_k_norm"][i])
            ya = _window_attn(aq, ak, avt, p["a_slope"], p["a_sink"][i], bsz, seq, A_KV_HEADS, a_grp)
            yb = _dense_attn(bq, bk, bvt, bsz, seq, B_KV_HEADS, B_HEADS // B_KV_HEADS, 1, 256)
            x2 = _out_proj(x2, [ya, yb], [p["ev_w_out_a"][i], p["ev_w_out_b"][i]])
            x2 = _ffn(x2, p["ev_norm_ffn"][i][None, :], p["ffn_w_gate"][i], p["ffn_w_up"][i], p["ffn_w_down"][i])
        else:
            q, k, vt = _odd_in(
                x2, bsz, seq, p["od_norm_mix"][i][None, :], p["od_w_in"][i], p["mla_q_norm"][i][None, :],
                p["mla_w_q_up"][i], p["mla_kv_norm"][i][None, :], p["mla_w_k_up"][i], p["mla_w_v_up"][i],
                cos_1d, sin_1d)
            yc = _dense_attn(q, k, vt, bsz, seq, MLA_HEADS, 1, 2, 1024)
            x2 = _out_proj(x2, [yc], [p["od_w_out"][i]])
            x2 = _moe(x2, p["od_norm_ffn"][i][None, :], p["router_hi"][i], p["router_lo"][i],
                      p["moe_w_gate"][i], p["moe_w_up"][i], p["moe_w_down"][i],
                      p["final_norm"], last_layer)
    return x2.reshape(bsz, seq, D_MODEL)


def _prepare(ev_norm_mix, ev_w_in, a_sink, b_q_norm, b_k_norm, ev_w_out, ev_norm_ffn,
             ffn_w_gate, ffn_w_up, ffn_w_down, od_norm_mix, od_w_in, mla_q_norm, mla_w_q_up,
             mla_kv_norm, mla_w_kv_up, od_w_out, od_norm_ffn, moe_router, moe_w_gate, moe_w_up,
             moe_w_down, final_norm):
    hd = HEAD_DIM
    a_grp = A_HEADS // A_KV_HEADS

    def even_w_in(w):
        sizes = [A_HEADS * hd, A_KV_HEADS * hd, A_KV_HEADS * hd, B_HEADS * hd, B_KV_HEADS * hd, B_KV_HEADS * hd]
        aq, ak, av, bq, bk, bv = jnp.split(w, list(np.cumsum(sizes)[:-1]), axis=-1)
        return jnp.concatenate([_pad_heads(aq, A_HEADS, hd), _pad_heads(ak, A_KV_HEADS, hd),
                                _pad_heads(bq, B_HEADS, hd), _pad_heads(bk, B_KV_HEADS, hd), av, bv],
                               axis=-1).astype(BF16)

    def odd_w_in(w):
        c = w[:, :MLA_Q_RANK + MLA_KV_RANK]
        kr = _pad_heads(w[:, MLA_Q_RANK + MLA_KV_RANK:], 1, MLA_ROPE_DIM, _MLA_ROPE_BASE)
        return jnp.concatenate([c, kr], axis=-1).astype(BF16)

    def q_up(w):
        return _pad_heads(w, MLA_HEADS, MLA_QK_DIM).astype(BF16)

    def kv_up(w):
        w = w.reshape(MLA_KV_RANK, MLA_HEADS, MLA_NOPE_DIM + MLA_V_DIM)
        wk = _pad_heads(w[:, :, :MLA_NOPE_DIM].reshape(MLA_KV_RANK, -1), MLA_HEADS, MLA_NOPE_DIM)
        wv = w[:, :, MLA_NOPE_DIM:].reshape(MLA_KV_RANK, -1)
        return wk.astype(BF16), wv.astype(BF16)

    slopes = jnp.asarray(2.0 ** (-8.0 * np.arange(1, A_HEADS + 1) / A_HEADS), dtype=F32)
    per_col = lambda v: jnp.repeat(v.astype(F32).reshape(A_KV_HEADS, a_grp), WIN_Q, axis=1)[:, None, :] * LOG2E
    router = jnp.pad(moe_router.astype(F32), ((0, 0), (0, 0), (0, LANES - N_EXPERTS)))
    router_hi = router.astype(BF16)
    kv = [kv_up(w) for w in mla_w_kv_up]
    return {
        "ev_norm_mix": ev_norm_mix, "ev_w_in": jnp.stack([even_w_in(w) for w in ev_w_in]),
        "a_slope": per_col(slopes), "a_sink": jnp.stack([per_col(s) for s in a_sink]),
        "b_q_norm": jnp.stack([_lane_gain(g) for g in b_q_norm]),
        "b_k_norm": jnp.stack([_lane_gain(g) for g in b_k_norm]),
        "ev_w_out_a": ev_w_out[:, :A_HEADS * hd].astype(BF16),
        "ev_w_out_b": ev_w_out[:, A_HEADS * hd:].astype(BF16),
        "ev_norm_ffn": ev_norm_ffn,
        "ffn_w_gate": ffn_w_gate.astype(BF16), "ffn_w_up": ffn_w_up.astype(BF16),
        "ffn_w_down": ffn_w_down.astype(BF16),
        "od_norm_mix": od_norm_mix, "od_w_in": jnp.stack([odd_w_in(w) for w in od_w_in]),
        "mla_q_norm": mla_q_norm, "mla_w_q_up": jnp.stack([q_up(w) for w in mla_w_q_up]),
        "mla_kv_norm": mla_kv_norm,
        "mla_w_k_up": jnp.stack([a for a, _ in kv]), "mla_w_v_up": jnp.stack([b for _, b in kv]),
        "od_w_out": od_w_out.astype(BF16), "od_norm_ffn": od_norm_ffn,
        "router_hi": router_hi, "router_lo": (router - router_hi.astype(F32)).astype(BF16),
        "moe_w_gate": moe_w_gate.astype(BF16), "moe_w_up": moe_w_up.astype(BF16),
        "moe_w_down": moe_w_down.astype(BF16),
        "final_norm": final_norm[None, :],
    }


def kernel(x_prompt, x_sample, ev_norm_mix, ev_w_in, a_sink, b_q_norm, b_k_norm, ev_w_out, ev_norm_ffn, ffn_w_gate, ffn_w_up, ffn_w_down, od_norm_mix, od_w_in, mla_q_norm, mla_w_q_up, mla_kv_norm, mla_w_kv_up, od_w_out, od_norm_ffn, moe_router, moe_w_gate, moe_w_up, moe_w_down, final_norm):
    p = _prepare(ev_norm_mix, ev_w_in, a_sink, b_q_norm, b_k_norm, ev_w_out, ev_norm_ffn,
                 ffn_w_gate, ffn_w_up, ffn_w_down, od_norm_mix, od_w_in, mla_q_norm, mla_w_q_up,
                 mla_kv_norm, mla_w_kv_up, od_w_out, od_norm_ffn, moe_router, moe_w_gate, moe_w_up,
                 moe_w_down, final_norm)
    return (_trunk(x_prompt, p), _trunk(x_sample, p))
```

```python
import functools
import math

import numpy as np
import jax
import jax.numpy as jnp
from jax import lax
from jax.experimental import pallas as pl
from jax.experimental.pallas import tpu as pltpu

F32 = jnp.float32
BF16 = jnp.bfloat16

D_MODEL = 1024
HEAD_DIM = 64
WINDOW = 128
GRID_W = 64
ROPE_THETA = 10000.0
NORM_EPS = 1e-6
A_HEADS, A_KV_HEADS = 8, 2
B_HEADS, B_KV_HEADS = 8, 2
MLA_HEADS = 8
MLA_Q_RANK, MLA_KV_RANK = 256, 128
MLA_NOPE_DIM, MLA_ROPE_DIM, MLA_V_DIM = 64, 32, 64
MLA_QK_DIM = MLA_NOPE_DIM + MLA_ROPE_DIM
N_EXPERTS = 8
TOP_K = 2
LANES = 128
LOG2E = math.log2(math.e)
VMEM_LIMIT = 48 * 1024 * 1024

TOK_TILE = 512
FFN_TOK_TILE = 1024
FFN_F_TILE = 512
KEY_CHUNK = 512
WIN_Q = 128
WIN_BLOCKS = 4
MOE_ROW_TILE = 448
EXPERT_F_STEPS = 7
DMA_UNROLL = 8


def _rms(x, g):
    return x * lax.rsqrt(jnp.mean(x * x, axis=-1, keepdims=True) + NORM_EPS) * g


def _swap_halves(x, half):
    lane = lax.broadcasted_iota(jnp.int32, x.shape, 1)
    return jnp.where(lane < half, pltpu.roll(x, LANES - half, 1), pltpu.roll(x, half, 1))


def _swap_halves_at(x, base, half):
    lane = lax.broadcasted_iota(jnp.int32, x.shape, 1)
    return jnp.where(lane < base + half, pltpu.roll(x, LANES - half, 1), pltpu.roll(x, half, 1))


def _params(sem):
    return pltpu.CompilerParams(dimension_semantics=sem, vmem_limit_bytes=VMEM_LIMIT)


_EV_AQ, _EV_AK, _EV_BQ, _EV_BK, _EV_V, _EV_END = 0, 1024, 1280, 2304, 2560, 2816


def _even_in_kernel(x_ref, g_ref, w_ref, cos_ref, sin_ref, qg_ref, kg_ref,
                    aq_ref, ak_ref, avt_ref, bq_ref, bk_ref, bvt_ref):
    xb = _rms(x_ref[...], g_ref[...]).astype(BF16)
    cos = cos_ref[...]
    sin = sin_ref[...]
    qscale = (HEAD_DIM ** -0.5) * LOG2E

    def norm_rope(blk, gain):
        ms = jnp.sum(blk * blk, axis=-1, keepdims=True) * (1.0 / HEAD_DIM)
        y = blk * lax.rsqrt(ms + NORM_EPS) * gain
        return y * cos + _swap_halves(y, HEAD_DIM // 2) * sin

    pa = jnp.dot(xb, w_ref[:, _EV_AQ:_EV_AK], preferred_element_type=F32)
    aq_ref[...] = (pa * qscale).astype(BF16)
    ak_ref[...] = jnp.dot(xb, w_ref[:, _EV_AK:_EV_BQ], preferred_element_type=F32).astype(BF16)
    pbq = jnp.dot(xb, w_ref[:, _EV_BQ:_EV_BK], preferred_element_type=F32)
    for h in range(B_HEADS):
        sl = slice(h * LANES, (h + 1) * LANES)
        bq_ref[:, sl] = (norm_rope(pbq[:, sl], qg_ref[...]) * qscale).astype(BF16)
    pbk = jnp.dot(xb, w_ref[:, _EV_BK:_EV_V], preferred_element_type=F32)
    lane = lax.broadcasted_iota(jnp.int32, (pbk.shape[0], LANES), 1)
    for h in range(B_KV_HEADS):
        sl = slice(h * LANES, (h + 1) * LANES)
        bk_ref[:, sl] = jnp.where(lane == REF_LANE, 1.0, norm_rope(pbk[:, sl], kg_ref[...])).astype(BF16)
    pv = jnp.dot(xb, w_ref[:, _EV_V:_EV_END], preferred_element_type=F32)
    avt_ref[0] = pv[:, :LANES].T.astype(BF16)
    bvt_ref[0] = pv[:, LANES:].T.astype(BF16)


def _even_in(x2, bsz, seq, g, w_exp, cos_t, sin_t, qg, kg):
    tokens = x2.shape[0]
    tt = min(TOK_TILE, seq)
    assert seq % tt == 0
    ns = seq // tt
    row = lambda i: (i, 0)
    const = lambda i: (0, 0)
    tab = lambda i: (i % ns, 0)
    vt = lambda i: (i // ns, 0, i % ns)
    return pl.pallas_call(
        _even_in_kernel,
        grid=(tokens // tt,),
        in_specs=[
            pl.BlockSpec((tt, D_MODEL), row),
            pl.BlockSpec((1, D_MODEL), const),
            pl.BlockSpec((D_MODEL, _EV_END), const),
            pl.BlockSpec((tt, LANES), tab),
            pl.BlockSpec((tt, LANES), tab),
            pl.BlockSpec((1, LANES), const),
            pl.BlockSpec((1, LANES), const),
        ],
        out_specs=[
            pl.BlockSpec((tt, A_HEADS * LANES), row),
            pl.BlockSpec((tt, A_KV_HEADS * LANES), row),
            pl.BlockSpec((1, A_KV_HEADS * HEAD_DIM, tt), vt),
            pl.BlockSpec((tt, B_HEADS * LANES), row),
            pl.BlockSpec((tt, B_KV_HEADS * LANES), row),
            pl.BlockSpec((1, B_KV_HEADS * HEAD_DIM, tt), vt),
        ],
        out_shape=[
            jax.ShapeDtypeStruct((tokens, A_HEADS * LANES), BF16),
            jax.ShapeDtypeStruct((tokens, A_KV_HEADS * LANES), BF16),
            jax.ShapeDtypeStruct((bsz, A_KV_HEADS * HEAD_DIM, seq), BF16),
            jax.ShapeDtypeStruct((tokens, B_HEADS * LANES), BF16),
            jax.ShapeDtypeStruct((tokens, B_KV_HEADS * LANES), BF16),
            jax.ShapeDtypeStruct((bsz, B_KV_HEADS * HEAD_DIM, seq), BF16),
        ],
        compiler_params=_params(("arbitrary",)),
        name="even_in_proj",
    )(x2, g, w_exp, cos_t, sin_t, qg, kg)


_MLA_ROPE_BASE = MLA_NOPE_DIM


def _odd_in_kernel(x_ref, g_ref, w_ref, qn_ref, wq_ref, kvn_ref, wk_ref, wv_ref,
                   cos_ref, sin_ref, q_ref, k_ref, vt_ref):
    xb = _rms(x_ref[...], g_ref[...]).astype(BF16)
    cos = cos_ref[...]
    sin = sin_ref[...]
    qscale = (MLA_QK_DIM ** -0.5) * LOG2E

    def rope(blk):
        return blk * cos + _swap_halves_at(blk, _MLA_ROPE_BASE, MLA_ROPE_DIM // 2) * sin

    proj = jnp.dot(xb, w_ref[...], preferred_element_type=F32)
    cq = _rms(proj[:, :MLA_Q_RANK], qn_ref[...]).astype(BF16)
    ckv = _rms(proj[:, MLA_Q_RANK:MLA_Q_RANK + MLA_KV_RANK], kvn_ref[...]).astype(BF16)
    kr = rope(proj[:, MLA_Q_RANK + MLA_KV_RANK:])
    q = jnp.dot(cq, wq_ref[...], preferred_element_type=F32)
    kn = jnp.dot(ckv, wk_ref[...], preferred_element_type=F32)
    lane = lax.broadcasted_iota(jnp.int32, (kr.shape[0], LANES), 1)
    for h in range(MLA_HEADS):
        sl = slice(h * LANES, (h + 1) * LANES)
        q_ref[:, sl] = (rope(q[:, sl]) * qscale).astype(BF16)
        k_ref[:, sl] = jnp.where(lane == REF_LANE, 1.0, kn[:, sl] + kr).astype(BF16)
    v = jnp.dot(ckv, wv_ref[...], preferred_element_type=F32)
    for c in range(MLA_HEADS * MLA_V_DIM // LANES):
        vt_ref[0, c * LANES:(c + 1) * LANES, :] = v[:, c * LANES:(c + 1) * LANES].T.astype(BF16)


def _odd_in(x2, bsz, seq, g, w_exp, qn, wq_exp, kvn, wk_exp, wv, cos_t, sin_t):
    tokens = x2.shape[0]
    tt = min(TOK_TILE, seq)
    assert seq % tt == 0
    ns = seq // tt
    row = lambda i: (i, 0)
    const = lambda i: (0, 0)
    tab = lambda i: (i % ns, 0)
    vt = lambda i: (i // ns, 0, i % ns)
    hl = MLA_HEADS * LANES
    hv = MLA_HEADS * MLA_V_DIM
    return pl.pallas_call(
        _odd_in_kernel,
        grid=(tokens // tt,),
        in_specs=[
            pl.BlockSpec((tt, D_MODEL), row),
            pl.BlockSpec((1, D_MODEL), const),
            pl.BlockSpec((D_MODEL, 4 * LANES), const),
            pl.BlockSpec((1, MLA_Q_RANK), const),
            pl.BlockSpec((MLA_Q_RANK, hl), const),
            pl.BlockSpec((1, MLA_KV_RANK), const),
            pl.BlockSpec((MLA_KV_RANK, hl), const),
            pl.BlockSpec((MLA_KV_RANK, hv), const),
            pl.BlockSpec((tt, LANES), tab),
            pl.BlockSpec((tt, LANES), tab),
        ],
        out_specs=[
            pl.BlockSpec((tt, hl), row),
            pl.BlockSpec((tt, hl), row),
            pl.BlockSpec((1, hv, tt), vt),
        ],
        out_shape=[
            jax.ShapeDtypeStruct((tokens, hl), BF16),
            jax.ShapeDtypeStruct((tokens, hl), BF16),
            jax.ShapeDtypeStruct((bsz, hv, seq), BF16),
        ],
        compiler_params=_params(("arbitrary",)),
        name="odd_in_proj",
    )(x2, g, w_exp, qn, wq_exp, kvn, wk_exp, wv, cos_t, sin_t)


def _pair_transpose(pieces):
    outs = []
    for a in range(0, len(pieces), 2):
        outs.append(jnp.concatenate([pieces[a], pieces[a + 1]], axis=0).T)
    return outs[0] if len(outs) == 1 else jnp.concatenate(outs, axis=1)


SUM_ROWS = 16
REF_LANE = LANES - 1
MIN_CHUNK_UNROLL = 4
MAX_CHUNK_UNROLL = 8
SEED_KEYS = 128
JUMP_LIMIT = 64.0


def _bf16_round(x):
    return x.astype(BF16).astype(F32)


def _dense_attn_kernel(q_ref, k_ref, vt_ref, o_ref, qt_ref, acc_ref, ob_ref, *, kb, grp, tq, tk, seq, unroll):
    n = grp * tq
    dv = HEAD_DIM
    n_chunks = seq // tk
    ones = jnp.ones((SUM_ROWS, tk), BF16)
    row16 = lax.broadcasted_iota(jnp.int32, (16, n), 0)
    pieces = []
    for j in range(kb):
        qs = jnp.concatenate(
            [q_ref[:, (j * grp + g) * LANES:(j * grp + g + 1) * LANES] for g in range(grp)], axis=0)
        qt = qs.astype(F32).T.astype(BF16)
        qt_ref[0] = qt
        qt_ref[1] = qt

        def keys(c, j=j):
            off = pl.multiple_of(c * tk, tk)
            return k_ref[pl.ds(off, tk), j * LANES:(j + 1) * LANES]

        def values(c, j=j):
            off = pl.multiple_of(c * tk, tk)
            return jnp.concatenate([vt_ref[0, j * dv:(j + 1) * dv, pl.ds(off, tk)], ones], axis=0)

        def set_reference(buf, ref):
            qt_ref[buf, LANES - 16:, :] = jnp.where(row16 == 15, -ref, 0.0).astype(BF16)

        def chunk(c, buf, ref, ref_acc, run, jump_max):
            set_reference(buf, ref)
            t = jnp.dot(keys(c), qt_ref[buf], preferred_element_type=F32)
            jump = jnp.max(t, axis=0, keepdims=True)
            p = jnp.exp2(t.astype(BF16))
            pv = jnp.dot(values(c), p, preferred_element_type=F32)
            acc_ref[...] = jnp.exp2(ref_acc - ref) * acc_ref[...] + pv
            return jnp.maximum(run, ref + jump), jnp.maximum(jump_max, jump)

        first = k_ref[0:min(SEED_KEYS, tk), j * LANES:(j + 1) * LANES]
        run0 = jnp.max(jnp.dot(first, qt, preferred_element_type=F32), axis=0, keepdims=True)
        ref0 = _bf16_round(run0)
        acc_ref[...] = jnp.zeros((dv + SUM_ROWS, n), F32)

        def group(ci, carry, chunk=chunk):
            ref_a, ref_b, ref_acc, run, jump_max = carry
            for u in range(unroll):
                run, jump_max = chunk(ci * unroll + u, u % 2, ref_a, ref_acc, run, jump_max)
                ref_a, ref_b, ref_acc = ref_b, _bf16_round(run), ref_a
            return ref_a, ref_b, ref_acc, run, jump_max

        init = (ref0, ref0, ref0, run0, jnp.zeros((1, n), F32))
        jump_max = lax.fori_loop(0, n_chunks // unroll, group, init)[4]
        acc = acc_ref[...]
        ob_ref[j] = acc[:dv] / acc[dv:dv + 1]

        @pl.when(jnp.max(jump_max) > JUMP_LIMIT)
        def _(j=j, qt=qt, keys=keys, values=values):
            acc_ref[...] = jnp.zeros((dv + SUM_ROWS, n), F32)

            def exact(c, m_old):
                s = jnp.dot(keys(c), qt, preferred_element_type=F32)
                m_new = jnp.maximum(m_old, jnp.max(s, axis=0, keepdims=True))
                p = jnp.exp2((s - m_new).astype(BF16))
                pv = jnp.dot(values(c), p, preferred_element_type=F32)
                acc_ref[...] = jnp.exp2(m_old - m_new) * acc_ref[...] + pv
                return m_new

            lax.fori_loop(0, n_chunks, exact, jnp.full((1, n), -jnp.inf, F32))
            acc = acc_ref[...]
            ob_ref[j] = acc[:dv] / acc[dv:dv + 1]

        o = ob_ref[j]
        for g in range(grp):
            pieces.append(o[:, g * tq:(g + 1) * tq])
    o_ref[...] = _pair_transpose(pieces).astype(o_ref.dtype)


def _dense_attn(q, k, vt, bsz, seq, n_kv, grp, kb, tq):
    tokens = q.shape[0]
    tq = min(tq, seq)
    tk = min(KEY_CHUNK, seq // MIN_CHUNK_UNROLL)
    n_chunks = seq // tk
    unroll = MAX_CHUNK_UNROLL if n_chunks % MAX_CHUNK_UNROLL == 0 and n_chunks > MAX_CHUNK_UNROLL else MIN_CHUNK_UNROLL
    assert seq % tq == 0 and seq % tk == 0 and n_chunks % unroll == 0
    assert n_kv % kb == 0 and (kb * grp) % 2 == 0
    nq = seq // tq
    n = grp * tq
    kern = functools.partial(_dense_attn_kernel, kb=kb, grp=grp, tq=tq, tk=tk, seq=seq, unroll=unroll)
    return pl.pallas_call(
        kern,
        grid=(bsz, n_kv // kb, nq),
        in_specs=[
            pl.BlockSpec((tq, kb * grp * LANES), lambda b, h, i: (b * nq + i, h)),
            pl.BlockSpec((seq, kb * LANES), lambda b, h, i: (b, h)),
            pl.BlockSpec((1, kb * HEAD_DIM, seq), lambda b, h, i: (b, h, 0)),
        ],
        out_specs=pl.BlockSpec((tq, kb * grp * HEAD_DIM), lambda b, h, i: (b * nq + i, h)),
        out_shape=jax.ShapeDtypeStruct((tokens, n_kv * grp * HEAD_DIM), BF16),
        scratch_shapes=[
            pltpu.VMEM((2, LANES, n), BF16),
            pltpu.VMEM((HEAD_DIM + SUM_ROWS, n), F32),
            pltpu.VMEM((kb, HEAD_DIM, n), F32),
        ],
        compiler_params=_params(("arbitrary", "arbitrary", "arbitrary")),
        name="dense_attn",
    )(q, k, vt)


def _window_attn_kernel(q_ref, k_ref, vt_ref, slope_ref, sink_ref, o_ref, *, grp, seq):
    tq = WIN_Q
    span = 3 * WIN_Q
    n = grp * tq
    for u in range(WIN_BLOCKS):
        i = pl.program_id(2) * WIN_BLOCKS + u
        w0 = pl.multiple_of(jnp.clip((i - 1) * tq, 0, seq - span), tq)
        rows = slice(u * tq, (u + 1) * tq)
        qs = jnp.concatenate([q_ref[rows, g * LANES:(g + 1) * LANES] for g in range(grp)], axis=0)
        kc = k_ref[pl.ds(w0, span), :]
        s = lax.dot_general(kc, qs, (((1,), (1,)), ((), ())), preferred_element_type=F32)
        kpos = w0 + lax.broadcasted_iota(jnp.int32, (span, n), 0)
        qpos = i * tq + lax.broadcasted_iota(jnp.int32, (span, n), 1) % tq
        dist = jnp.abs(kpos - qpos)
        s = s - slope_ref[0] * dist.astype(F32)
        s = jnp.where(dist <= WINDOW, s, -jnp.inf)
        sink = sink_ref[0]
        m = jnp.maximum(jnp.max(s, axis=0, keepdims=True), sink)
        e = jnp.exp2(s - m)
        denom = jnp.sum(e, axis=0, keepdims=True) + jnp.exp2(sink - m)
        vc = vt_ref[0, :, pl.ds(w0, span)]
        o = jnp.dot(vc, e.astype(BF16), preferred_element_type=F32) / denom
        o_ref[rows, :] = _pair_transpose([o[:, g * tq:(g + 1) * tq] for g in range(grp)]).astype(o_ref.dtype)


def _window_attn(q, k, vt, slope_row, sink_row, bsz, seq, n_kv, grp):
    tokens = q.shape[0]
    tq = WIN_Q * WIN_BLOCKS
    assert seq % tq == 0 and seq >= 3 * WIN_Q
    nq = seq // tq
    n = grp * WIN_Q
    kern = functools.partial(_window_attn_kernel, grp=grp, seq=seq)
    return pl.pallas_call(
        kern,
        grid=(bsz, n_kv, nq),
        in_specs=[
            pl.BlockSpec((tq, grp * LANES), lambda b, h, i: (b * nq + i, h)),
            pl.BlockSpec((seq, LANES), lambda b, h, i: (b, h)),
            pl.BlockSpec((1, HEAD_DIM, seq), lambda b, h, i: (b, h, 0)),
            pl.BlockSpec((1, 1, n), lambda b, h, i: (h, 0, 0)),
            pl.BlockSpec((1, 1, n), lambda b, h, i: (h, 0, 0)),
        ],
        out_specs=pl.BlockSpec((tq, grp * HEAD_DIM), lambda b, h, i: (b * nq + i, h)),
        out_shape=jax.ShapeDtypeStruct((tokens, n_kv * grp * HEAD_DIM), BF16),
        compiler_params=_params(("arbitrary", "arbitrary", "arbitrary")),
        name="window_attn",
    )(q, k, vt, slope_row, sink_row)


def _out_proj_kernel(*refs, n_in):
    x_ref = refs[0]
    y_refs = refs[1:1 + n_in]
    w_refs = refs[1 + n_in:1 + 2 * n_in]
    o_ref = refs[1 + 2 * n_in]
    acc = x_ref[...]
    for y_ref, w_ref in zip(y_refs, w_refs):
        acc = acc + jnp.dot(y_ref[...], w_ref[...], preferred_element_type=F32)
    o_ref[...] = acc


def _out_proj(x2, ys, ws):
    tokens = x2.shape[0]
    tt = min(TOK_TILE, tokens)
    row = lambda i: (i, 0)
    const = lambda i: (0, 0)
    kern = functools.partial(_out_proj_kernel, n_in=len(ys))
    return pl.pallas_call(
        kern,
        grid=(tokens // tt,),
        in_specs=[pl.BlockSpec((tt, D_MODEL), row)]
        + [pl.BlockSpec((tt, y.shape[1]), row) for y in ys]
        + [pl.BlockSpec(w.shape, const) for w in ws],
        out_specs=pl.BlockSpec((tt, D_MODEL), row),
        out_shape=jax.ShapeDtypeStruct((tokens, D_MODEL), F32),
        compiler_params=_params(("arbitrary",)),
        name="out_proj",
    )(x2, *ys, *ws)


def _swiglu_partial(xb, wg, wu, wd):
    hg = jnp.dot(xb, wg, preferred_element_type=F32)
    hu = jnp.dot(xb, wu, preferred_element_type=F32)
    h = hg * jax.nn.sigmoid(hg) * hu
    return jnp.dot(h.astype(BF16), wd, preferred_element_type=F32)


def _ffn_kernel(x_ref, g_ref, wg_ref, wu_ref, wd_ref, o_ref, xn_ref, acc_ref):
    f = pl.program_id(1)

    @pl.when(f == 0)
    def _():
        xn_ref[...] = _rms(x_ref[...], g_ref[...]).astype(BF16)
        acc_ref[...] = jnp.zeros_like(acc_ref)

    acc_ref[...] += _swiglu_partial(xn_ref[...], wg_ref[...], wu_ref[...], wd_ref[...])

    @pl.when(f == pl.num_programs(1) - 1)
    def _():
        o_ref[...] = x_ref[...] + acc_ref[...]


def _ffn(x2, g, wg, wu, wd):
    tokens = x2.shape[0]
    ff = wg.shape[1]
    tt = min(FFN_TOK_TILE, tokens)
    fc = FFN_F_TILE
    assert tokens % tt == 0 and ff % fc == 0
    row = lambda i, f: (i, 0)
    return pl.pallas_call(
        _ffn_kernel,
        grid=(tokens // tt, ff // fc),
        in_specs=[
            pl.BlockSpec((tt, D_MODEL), row),
            pl.BlockSpec((1, D_MODEL), lambda i, f: (0, 0)),
            pl.BlockSpec((D_MODEL, fc), lambda i, f: (0, f)),
            pl.BlockSpec((D_MODEL, fc), lambda i, f: (0, f)),
            pl.BlockSpec((fc, D_MODEL), lambda i, f: (f, 0)),
        ],
        out_specs=pl.BlockSpec((tt, D_MODEL), row),
        out_shape=jax.ShapeDtypeStruct((tokens, D_MODEL), F32),
        scratch_shapes=[pltpu.VMEM((tt, D_MODEL), BF16), pltpu.VMEM((tt, D_MODEL), F32)],
        compiler_params=_params(("arbitrary", "arbitrary")),
        name="ffn",
    )(x2, g, wg, wu, wd)


def _router_kernel(x_ref, g_ref, rh_ref, rl_ref, xn_ref, ti_ref, tw_ref):
    tt = x_ref.shape[0]
    xn = _rms(x_ref[...], g_ref[...])
    xn_ref[...] = xn
    xh = xn.astype(BF16)
    xl = (xn - xh.astype(F32)).astype(BF16)
    logits = (jnp.dot(xh, rh_ref[...], preferred_element_type=F32)
              + jnp.dot(xh, rl_ref[...], preferred_element_type=F32)
              + jnp.dot(xl, rh_ref[...], preferred_element_type=F32))
    lane = lax.broadcasted_iota(jnp.int32, (tt, LANES), 1).astype(F32)
    lg = jnp.where(lane < N_EXPERTS, logits, -jnp.inf)
    m1 = jnp.max(lg, axis=1, keepdims=True)
    i1 = jnp.min(jnp.where(lg == m1, lane, float(LANES)), axis=1, keepdims=True)
    lg2 = jnp.where(lane == i1, -jnp.inf, lg)
    m2 = jnp.max(lg2, axis=1, keepdims=True)
    i2 = jnp.min(jnp.where(lg2 == m2, lane, float(LANES)), axis=1, keepdims=True)
    e2 = jnp.exp(m2 - m1)
    w1 = 1.0 / (1.0 + e2)
    w2 = e2 * w1
    ti_ref[...] = jnp.where(lane == 0.0, i1, i2).T[:8].astype(jnp.int32)
    tw_ref[...] = jnp.where(lane == 0.0, w1, w2)


def _router(x2, g, r_hi, r_lo):
    tokens = x2.shape[0]
    tt = min(TOK_TILE, tokens)
    row = lambda i: (i, 0)
    const = lambda i: (0, 0)
    return pl.pallas_call(
        _router_kernel,
        grid=(tokens // tt,),
        in_specs=[
            pl.BlockSpec((tt, D_MODEL), row),
            pl.BlockSpec((1, D_MODEL), const),
            pl.BlockSpec((D_MODEL, LANES), const),
            pl.BlockSpec((D_MODEL, LANES), const),
        ],
        out_specs=[
            pl.BlockSpec((tt, D_MODEL), row),
            pl.BlockSpec((8, tt), lambda i: (0, i)),
            pl.BlockSpec((tt, LANES), row),
        ],
        out_shape=[
            jax.ShapeDtypeStruct((tokens, D_MODEL), F32),
            jax.ShapeDtypeStruct((8, tokens), jnp.int32),
            jax.ShapeDtypeStruct((tokens, LANES), F32),
        ],
        compiler_params=_params(("arbitrary",)),
        name="moe_router",
    )(x2, g, r_hi, r_lo)


def _for_rows(n_rows, fn):
    def group(u, carry):
        for v in range(DMA_UNROLL):
            fn(u * DMA_UNROLL + v)
        return carry
    lax.fori_loop(0, n_rows // DMA_UNROLL, group, 0)


def _expert_ffn_kernel(be_ref, nu_ref, dst_ref, xn_hbm, wg_ref, wu_ref, wd_ref, y_hbm,
                       xg_ref, xb_ref, acc_ref, yb_ref, gsem, ssem, *, tokens):
    b = pl.program_id(0)
    f = pl.program_id(1)
    nf = pl.num_programs(1)
    nu = nu_ref[0]
    used = b < nu
    last = f == nf - 1
    slot = b % 2
    m = MOE_ROW_TILE
    per_step = m // EXPERT_F_STEPS
    nxt = jnp.minimum(b + 1, nu - 1)

    def gather(blk, r, sl):
        tok = lax.rem(dst_ref[blk * m + r], tokens)
        return pltpu.make_async_copy(xn_hbm.at[pl.ds(tok, 1)], xg_ref.at[sl, pl.ds(r, 1)], gsem.at[sl])

    def scatter(blk, r):
        return pltpu.make_async_copy(yb_ref.at[pl.ds(r, 1)], y_hbm.at[pl.ds(dst_ref[blk * m + r], 1)], ssem)

    @pl.when(jnp.logical_and(used, jnp.logical_and(f == 0, b == 0)))
    def _():
        _for_rows(m, lambda r: gather(0, r, 0).start())

    @pl.when(jnp.logical_and(used, f == 0))
    def _():
        _for_rows(m, lambda r: gather(b, r, slot).wait())
        xb_ref[...] = xg_ref[slot].astype(BF16)
        acc_ref[...] = jnp.zeros_like(acc_ref)

    def step(with_scatter):
        for v in range(per_step):
            r = f * per_step + v
            gather(nxt, r, 1 - slot).start()
            if with_scatter:
                scatter(b - 1, r).start()
        acc_ref[...] += _swiglu_partial(xb_ref[...], wg_ref[0], wu_ref[0], wd_ref[0])

    @pl.when(jnp.logical_and(used, b == 0))
    def _():
        step(False)

    @pl.when(jnp.logical_and(used, b > 0))
    def _():
        step(True)

    @pl.when(jnp.logical_and(used, jnp.logical_and(last, b > 0)))
    def _():
        _for_rows(m, lambda r: scatter(b - 1, r).wait())

    @pl.when(jnp.logical_and(used, last))
    def _():
        yb_ref[...] = acc_ref[...]

    @pl.when(jnp.logical_and(last, b == nu - 1))
    def _():
        _for_rows(m, lambda r: scatter(b, r).start())
        _for_rows(m, lambda r: gather(nxt, r, 1 - slot).wait())
        _for_rows(m, lambda r: scatter(b, r).wait())


def _expert_ffn(xn, dst, blk_expert, n_used, wg, wu, wd):
    tokens = xn.shape[0]
    rows = dst.shape[0]
    ff = wg.shape[2]
    m = MOE_ROW_TILE
    assert rows % m == 0 and ff % EXPERT_F_STEPS == 0 and m % EXPERT_F_STEPS == 0
    fc = ff // EXPERT_F_STEPS
    assert fc % LANES == 0
    return pl.pallas_call(
        functools.partial(_expert_ffn_kernel, tokens=tokens),
        grid_spec=pltpu.PrefetchScalarGridSpec(
            num_scalar_prefetch=3,
            grid=(rows // m, EXPERT_F_STEPS),
            in_specs=[
                pl.BlockSpec(memory_space=pl.ANY),
                pl.BlockSpec((1, D_MODEL, fc), lambda b, f, be, nu, ds: (be[b], 0, f)),
                pl.BlockSpec((1, D_MODEL, fc), lambda b, f, be, nu, ds: (be[b], 0, f)),
                pl.BlockSpec((1, fc, D_MODEL), lambda b, f, be, nu, ds: (be[b], f, 0)),
            ],
            out_specs=pl.BlockSpec(memory_space=pl.ANY),
            scratch_shapes=[
                pltpu.VMEM((2, m, D_MODEL), F32),
                pltpu.VMEM((m, D_MODEL), BF16),
                pltpu.VMEM((m, D_MODEL), F32),
                pltpu.VMEM((m, D_MODEL), F32),
                pltpu.SemaphoreType.DMA((2,)),
                pltpu.SemaphoreType.DMA,
            ],
        ),
        out_shape=jax.ShapeDtypeStruct((TOP_K * tokens + rows, D_MODEL), F32),
        compiler_params=_params(("arbitrary", "arbitrary")),
        name="expert_ffn",
    )(blk_expert, n_used, dst, xn, wg, wu, wd)


def _moe_combine_kernel(x_ref, y0_ref, y1_ref, tw_ref, gf_ref, o_ref, *, final_norm):
    tw = tw_ref[...]
    y = x_ref[...] + tw[:, 0:1] * y0_ref[...] + tw[:, 1:2] * y1_ref[...]
    if final_norm:
        y = _rms(y, gf_ref[...])
    o_ref[...] = y


def _moe_combine(x2, y, top_w, g_final, final_norm):
    tokens = x2.shape[0]
    tt = min(TOK_TILE, tokens)
    nt = tokens // tt
    row = lambda i: (i, 0)
    return pl.pallas_call(
        functools.partial(_moe_combine_kernel, final_norm=final_norm),
        grid=(nt,),
        in_specs=[
            pl.BlockSpec((tt, D_MODEL), row),
            pl.BlockSpec((tt, D_MODEL), row),
            pl.BlockSpec((tt, D_MODEL), lambda i: (nt + i, 0)),
            pl.BlockSpec((tt, LANES), row),
            pl.BlockSpec((1, D_MODEL), lambda i: (0, 0)),
        ],
        out_specs=pl.BlockSpec((tt, D_MODEL), row),
        out_shape=jax.ShapeDtypeStruct((tokens, D_MODEL), F32),
        compiler_params=_params(("arbitrary",)),
        name="moe_combine",
    )(x2, y, y, top_w, g_final)


def _route(top_i):
    m = MOE_ROW_TILE
    tokens = top_i.shape[1]
    n_assign = tokens * TOP_K
    rows = -(-(n_assign + N_EXPERTS * m) // m) * m
    e_flat = top_i.reshape(-1)
    counts = jnp.sum((e_flat[:, None] == jnp.arange(N_EXPERTS, dtype=jnp.int32)[None, :]).astype(jnp.int32), axis=0)
    padded = (counts + m - 1) // m * m
    ends = jnp.cumsum(padded)
    starts = ends - padded
    ustarts = jnp.cumsum(counts) - counts
    order = jnp.argsort(e_flat, stable=True).astype(jnp.int32)
    slot = jnp.arange(rows, dtype=jnp.int32)
    e_slot = jnp.minimum(jnp.sum((slot[:, None] >= ends[None, :]).astype(jnp.int32), axis=1), N_EXPERTS - 1)
    j = slot - starts[e_slot]
    valid = jnp.logical_and(j < counts[e_slot], slot < ends[-1])
    a = order[jnp.clip(ustarts[e_slot] + j, 0, n_assign - 1)]
    dst = jnp.where(valid, a, n_assign + slot).astype(jnp.int32)
    n_used = (ends[-1] // m).astype(jnp.int32)[None]
    return dst, e_slot[::m], n_used


def _moe(x2, g, r_hi, r_lo, wg, wu, wd, g_final, final_norm):
    xn, top_i, top_w = _router(x2, g, r_hi, r_lo)
    dst, blk_expert, n_used = _route(top_i[:TOP_K])
    y = _expert_ffn(xn, dst, blk_expert, n_used, wg, wu, wd)
    return _moe_combine(x2, y, top_w, g_final, final_norm)


def _pad_heads(w, n_heads, width, offset=0):
    r = w.shape[0]
    w = w.reshape(r, n_heads, width)
    w = jnp.pad(w, ((0, 0), (0, 0), (offset, LANES - width - offset)))
    return w.reshape(r, n_heads * LANES)


def _rope_tables(ang, base):
    half = ang.shape[1]
    cos = jnp.cos(ang)
    sin = jnp.sin(ang)
    pad = ((0, 0), (base, LANES - base - 2 * half))
    cos_t = jnp.pad(jnp.concatenate([cos, cos], axis=1) - 1.0, pad) + 1.0
    sin_t = jnp.pad(jnp.concatenate([-sin, sin], axis=1), pad)
    return cos_t, sin_t


def _rope_angles(pos, dim):
    inv = ROPE_THETA ** (-jnp.arange(0, dim, 2, dtype=F32) / dim)
    return pos.astype(F32)[:, None] * inv[None, :]


def _lane_gain(g):
    return jnp.pad(g.astype(F32), (0, LANES - g.shape[0]))[None, :]


def _trunk(x, p):
    bsz, seq, _ = x.shape
    x2 = x.reshape(bsz * seq, D_MODEL)
    n_rows = seq // GRID_W
    row = jnp.repeat(jnp.arange(n_rows), GRID_W)
    col = jnp.tile(jnp.arange(GRID_W), n_rows)
    ang_axial = jnp.concatenate([_rope_angles(row, HEAD_DIM // 2), _rope_angles(col, HEAD_DIM // 2)], axis=-1)
    cos_ax, sin_ax = _rope_tables(ang_axial, 0)
    cos_1d, sin_1d = _rope_tables(_rope_angles(jnp.arange(seq), MLA_ROPE_DIM), _MLA_ROPE_BASE)
    depth = p["ev_norm_mix"].shape[0] + p["od_norm_mix"].shape[0]
    assert depth % 2 == 0
    a_grp = A_HEADS // A_KV_HEADS
    for layer in range(depth):
        i = layer // 2
        last_layer = layer == depth - 1
        if layer % 2 == 0:
            aq, ak, avt, bq, bk, bvt = _even_in(
                x2, bsz, seq, p["ev_norm_mix"][i][None, :], p["ev_w_in"][i], cos_ax, sin_ax,
                p["b_q_norm"][i], p["b_k_norm"][i])
            ya = _window_attn(aq, ak, avt, p["a_slope"], p["a_sink"][i], bsz, seq, A_KV_HEADS, a_grp)
            yb = _dense_attn(bq, bk, bvt, bsz, seq, B_KV_HEADS, B_HEADS // B_KV_HEADS, 1, 256)
            x2 = _out_proj(x2, [ya, yb], [p["ev_w_out_a"][i], p["ev_w_out_b"][i]])
            x2 = _ffn(x2, p["ev_norm_ffn"][i][None, :], p["ffn_w_gate"][i], p["ffn_w_up"][i], p["ffn_w_down"][i])
        else:
            q, k, vt = _odd_in(
                x2, bsz, seq, p["od_norm_mix"][i][None, :], p["od_w_in"][i], p["mla_q_norm"][i][None, :],
                p["mla_w_q_up"][i], p["mla_kv_norm"][i][None, :], p["mla_w_k_up"][i], p["mla_w_v_up"][i],
                cos_1d, sin_1d)
            yc = _dense_attn(q, k, vt, bsz, seq, MLA_HEADS, 1, 2, 1024)
            x2 = _out_proj(x2, [yc], [p["od_w_out"][i]])
            x2 = _moe(x2, p["od_norm_ffn"][i][None, :], p["router_hi"][i], p["router_lo"][i],
                      p["moe_w_gate"][i], p["moe_w_up"][i], p["moe_w_down"][i],
                      p["final_norm"], last_layer)
    return x2.reshape(bsz, seq, D_MODEL)


def _prepare(ev_norm_mix, ev_w_in, a_sink, b_q_norm, b_k_norm, ev_w_out, ev_norm_ffn,
             ffn_w_gate, ffn_w_up, ffn_w_down, od_norm_mix, od_w_in, mla_q_norm, mla_w_q_up,
             mla_kv_norm, mla_w_kv_up, od_w_out, od_norm_ffn, moe_router, moe_w_gate, moe_w_up,
             moe_w_down, final_norm):
    hd = HEAD_DIM
    a_grp = A_HEADS // A_KV_HEADS

    def even_w_in(w):
        sizes = [A_HEADS * hd, A_KV_HEADS * hd, A_KV_HEADS * hd, B_HEADS * hd, B_KV_HEADS * hd, B_KV_HEADS * hd]
        aq, ak, av, bq, bk, bv = jnp.split(w, list(np.cumsum(sizes)[:-1]), axis=-1)
        return jnp.concatenate([_pad_heads(aq, A_HEADS, hd), _pad_heads(ak, A_KV_HEADS, hd),
                                _pad_heads(bq, B_HEADS, hd), _pad_heads(bk, B_KV_HEADS, hd), av, bv],
                               axis=-1).astype(BF16)

    def odd_w_in(w):
        c = w[:, :MLA_Q_RANK + MLA_KV_RANK]
        kr = _pad_heads(w[:, MLA_Q_RANK + MLA_KV_RANK:], 1, MLA_ROPE_DIM, _MLA_ROPE_BASE)
        return jnp.concatenate([c, kr], axis=-1).astype(BF16)

    def q_up(w):
        return _pad_heads(w, MLA_HEADS, MLA_QK_DIM).astype(BF16)

    def kv_up(w):
        w = w.reshape(MLA_KV_RANK, MLA_HEADS, MLA_NOPE_DIM + MLA_V_DIM)
        wk = _pad_heads(w[:, :, :MLA_NOPE_DIM].reshape(MLA_KV_RANK, -1), MLA_HEADS, MLA_NOPE_DIM)
        wv = w[:, :, MLA_NOPE_DIM:].reshape(MLA_KV_RANK, -1)
        return wk.astype(BF16), wv.astype(BF16)

    slopes = jnp.asarray(2.0 ** (-8.0 * np.arange(1, A_HEADS + 1) / A_HEADS), dtype=F32)
    per_col = lambda v: jnp.repeat(v.astype(F32).reshape(A_KV_HEADS, a_grp), WIN_Q, axis=1)[:, None, :] * LOG2E
    router = jnp.pad(moe_router.astype(F32), ((0, 0), (0, 0), (0, LANES - N_EXPERTS)))
    router_hi = router.astype(BF16)
    kv = [kv_up(w) for w in mla_w_kv_up]
    return {
        "ev_norm_mix": ev_norm_mix, "ev_w_in": jnp.stack([even_w_in(w) for w in ev_w_in]),
        "a_slope": per_col(slopes), "a_sink": jnp.stack([per_col(s) for s in a_sink]),
        "b_q_norm": jnp.stack([_lane_gain(g) for g in b_q_norm]),
        "b_k_norm": jnp.stack([_lane_gain(g) for g in b_k_norm]),
        "ev_w_out_a": ev_w_out[:, :A_HEADS * hd].astype(BF16),
        "ev_w_out_b": ev_w_out[:, A_HEADS * hd:].astype(BF16),
        "ev_norm_ffn": ev_norm_ffn,
        "ffn_w_gate": ffn_w_gate.astype(BF16), "ffn_w_up": ffn_w_up.astype(BF16),
        "ffn_w_down": ffn_w_down.astype(BF16),
        "od_norm_mix": od_norm_mix, "od_w_in": jnp.stack([odd_w_in(w) for w in od_w_in]),
        "mla_q_norm": mla_q_norm, "mla_w_q_up": jnp.stack([q_up(w) for w in mla_w_q_up]),
        "mla_kv_norm": mla_kv_norm,
        "mla_w_k_up": jnp.stack([a for a, _ in kv]), "mla_w_v_up": jnp.stack([b for _, b in kv]),
        "od_w_out": od_w_out.astype(BF16), "od_norm_ffn": od_norm_ffn,
        "router_hi": router_hi, "router_lo": (router - router_hi.astype(F32)).astype(BF16),
        "moe_w_gate": moe_w_gate.astype(BF16), "moe_w_up": moe_w_up.astype(BF16),
        "moe_w_down": moe_w_down.astype(BF16),
        "final_norm": final_norm[None, :],
    }


def kernel(x_prompt, x_sample, ev_norm_mix, ev_w_in, a_sink, b_q_norm, b_k_norm, ev_w_out, ev_norm_ffn, ffn_w_gate, ffn_w_up, ffn_w_down, od_norm_mix, od_w_in, mla_q_norm, mla_w_q_up, mla_kv_norm, mla_w_kv_up, od_w_out, od_norm_ffn, moe_router, moe_w_gate, moe_w_up, moe_w_down, final_norm):
    p = _prepare(ev_norm_mix, ev_w_in, a_sink, b_q_norm, b_k_norm, ev_w_out, ev_norm_ffn,
                 ffn_w_gate, ffn_w_up, ffn_w_down, od_norm_mix, od_w_in, mla_q_norm, mla_w_q_up,
                 mla_kv_norm, mla_w_kv_up, od_w_out, od_norm_ffn, moe_router, moe_w_gate, moe_w_up,
                 moe_w_down, final_norm)
    return (_trunk(x_prompt, p), _trunk(x_sample, p))
```

```python
import functools
import math

import numpy as np
import jax
import jax.numpy as jnp
from jax import lax
from jax.experimental import pallas as pl
from jax.experimental.pallas import tpu as pltpu

F32 = jnp.float32
BF16 = jnp.bfloat16

D_MODEL = 1024
HEAD_DIM = 64
WINDOW = 128
GRID_W = 64
ROPE_THETA = 10000.0
NORM_EPS = 1e-6
A_HEADS, A_KV_HEADS = 8, 2
B_HEADS, B_KV_HEADS = 8, 2
MLA_HEADS = 8
MLA_Q_RANK, MLA_KV_RANK = 256, 128
MLA_NOPE_DIM, MLA_ROPE_DIM, MLA_V_DIM = 64, 32, 64
MLA_QK_DIM = MLA_NOPE_DIM + MLA_ROPE_DIM
N_EXPERTS = 8
TOP_K = 2
LANES = 128
LOG2E = math.log2(math.e)
VMEM_LIMIT = 48 * 1024 * 1024

TOK_TILE = 512
FFN_TOK_TILE = 1024
FFN_F_TILE = 512
KEY_CHUNK = 512
WIN_Q = 128
WIN_BLOCKS = 4
MOE_ROW_TILE = 448
EXPERT_F_STEPS = 7
DMA_UNROLL = 8


def _rms(x, g):
    return x * lax.rsqrt(jnp.mean(x * x, axis=-1, keepdims=True) + NORM_EPS) * g


def _swap_halves(x, half):
    lane = lax.broadcasted_iota(jnp.int32, x.shape, 1)
    return jnp.where(lane < half, pltpu.roll(x, LANES - half, 1), pltpu.roll(x, half, 1))


def _swap_halves_at(x, base, half):
    lane = lax.broadcasted_iota(jnp.int32, x.shape, 1)
    return jnp.where(lane < base + half, pltpu.roll(x, LANES - half, 1), pltpu.roll(x, half, 1))


def _params(sem):
    return pltpu.CompilerParams(dimension_semantics=sem, vmem_limit_bytes=VMEM_LIMIT)


_EV_AQ, _EV_AK, _EV_BQ, _EV_BK, _EV_V, _EV_END = 0, 1024, 1280, 2304, 2560, 2816


def _even_in_kernel(x_ref, g_ref, w_ref, cos_ref, sin_ref, qg_ref, kg_ref,
                    aq_ref, ak_ref, avt_ref, bq_ref, bk_ref, bvt_ref):
    xb = _rms(x_ref[...], g_ref[...]).astype(BF16)
    cos = cos_ref[...]
    sin = sin_ref[...]
    qscale = (HEAD_DIM ** -0.5) * LOG2E

    def norm_rope(blk, gain):
        ms = jnp.sum(blk * blk, axis=-1, keepdims=True) * (1.0 / HEAD_DIM)
        y = blk * lax.rsqrt(ms + NORM_EPS) * gain
        return y * cos + _swap_halves(y, HEAD_DIM // 2) * sin

    pa = jnp.dot(xb, w_ref[:, _EV_AQ:_EV_AK], preferred_element_type=F32)
    aq_ref[...] = (pa * qscale).astype(BF16)
    ak_ref[...] = jnp.dot(xb, w_ref[:, _EV_AK:_EV_BQ], preferred_element_type=F32).astype(BF16)
    pbq = jnp.dot(xb, w_ref[:, _EV_BQ:_EV_BK], preferred_element_type=F32)
    for h in range(B_HEADS):
        sl = slice(h * LANES, (h + 1) * LANES)
        bq_ref[:, sl] = (norm_rope(pbq[:, sl], qg_ref[...]) * qscale).astype(BF16)
    pbk = jnp.dot(xb, w_ref[:, _EV_BK:_EV_V], preferred_element_type=F32)
    lane = lax.broadcasted_iota(jnp.int32, (pbk.shape[0], LANES), 1)
    for h in range(B_KV_HEADS):
        sl = slice(h * LANES, (h + 1) * LANES)
        bk_ref[:, sl] = jnp.where(lane == REF_LANE, 1.0, norm_rope(pbk[:, sl], kg_ref[...])).astype(BF16)
    pv = jnp.dot(xb, w_ref[:, _EV_V:_EV_END], preferred_element_type=F32)
    avt_ref[0] = pv[:, :LANES].T.astype(BF16)
    bvt_ref[0] = pv[:, LANES:].T.astype(BF16)


def _even_in(x2, bsz, seq, g, w_exp, cos_t, sin_t, qg, kg):
    tokens = x2.shape[0]
    tt = min(TOK_TILE, seq)
    assert seq % tt == 0
    ns = seq // tt
    row = lambda i: (i, 0)
    const = lambda i: (0, 0)
    tab = lambda i: (i % ns, 0)
    vt = lambda i: (i // ns, 0, i % ns)
    return pl.pallas_call(
        _even_in_kernel,
        grid=(tokens // tt,),
        in_specs=[
            pl.BlockSpec((tt, D_MODEL), row),
            pl.BlockSpec((1, D_MODEL), const),
            pl.BlockSpec((D_MODEL, _EV_END), const),
            pl.BlockSpec((tt, LANES), tab),
            pl.BlockSpec((tt, LANES), tab),
            pl.BlockSpec((1, LANES), const),
            pl.BlockSpec((1, LANES), const),
        ],
        out_specs=[
            pl.BlockSpec((tt, A_HEADS * LANES), row),
            pl.BlockSpec((tt, A_KV_HEADS * LANES), row),
            pl.BlockSpec((1, A_KV_HEADS * HEAD_DIM, tt), vt),
            pl.BlockSpec((tt, B_HEADS * LANES), row),
            pl.BlockSpec((tt, B_KV_HEADS * LANES), row),
            pl.BlockSpec((1, B_KV_HEADS * HEAD_DIM, tt), vt),
        ],
        out_shape=[
            jax.ShapeDtypeStruct((tokens, A_HEADS * LANES), BF16),
            jax.ShapeDtypeStruct((tokens, A_KV_HEADS * LANES), BF16),
            jax.ShapeDtypeStruct((bsz, A_KV_HEADS * HEAD_DIM, seq), BF16),
            jax.ShapeDtypeStruct((tokens, B_HEADS * LANES), BF16),
            jax.ShapeDtypeStruct((tokens, B_KV_HEADS * LANES), BF16),
            jax.ShapeDtypeStruct((bsz, B_KV_HEADS * HEAD_DIM, seq), BF16),
        ],
        compiler_params=_params(("arbitrary",)),
        name="even_in_proj",
    )(x2, g, w_exp, cos_t, sin_t, qg, kg)


_MLA_ROPE_BASE = MLA_NOPE_DIM


def _odd_in_kernel(x_ref, g_ref, w_ref, qn_ref, wq_ref, kvn_ref, wk_ref, wv_ref,
                   cos_ref, sin_ref, q_ref, k_ref, vt_ref):
    xb = _rms(x_ref[...], g_ref[...]).astype(BF16)
    cos = cos_ref[...]
    sin = sin_ref[...]
    qscale = (MLA_QK_DIM ** -0.5) * LOG2E

    def rope(blk):
        return blk * cos + _swap_halves_at(blk, _MLA_ROPE_BASE, MLA_ROPE_DIM // 2) * sin

    proj = jnp.dot(xb, w_ref[...], preferred_element_type=F32)
    cq = _rms(proj[:, :MLA_Q_RANK], qn_ref[...]).astype(BF16)
    ckv = _rms(proj[:, MLA_Q_RANK:MLA_Q_RANK + MLA_KV_RANK], kvn_ref[...]).astype(BF16)
    kr = rope(proj[:, MLA_Q_RANK + MLA_KV_RANK:])
    q = jnp.dot(cq, wq_ref[...], preferred_element_type=F32)
    kn = jnp.dot(ckv, wk_ref[...], preferred_element_type=F32)
    lane = lax.broadcasted_iota(jnp.int32, (kr.shape[0], LANES), 1)
    for h in range(MLA_HEADS):
        sl = slice(h * LANES, (h + 1) * LANES)
        q_ref[:, sl] = (rope(q[:, sl]) * qscale).astype(BF16)
        k_ref[:, sl] = jnp.where(lane == REF_LANE, 1.0, kn[:, sl] + kr).astype(BF16)
    v = jnp.dot(ckv, wv_ref[...], preferred_element_type=F32)
    for c in range(MLA_HEADS * MLA_V_DIM // LANES):
        vt_ref[0, c * LANES:(c + 1) * LANES, :] = v[:, c * LANES:(c + 1) * LANES].T.astype(BF16)


def _odd_in(x2, bsz, seq, g, w_exp, qn, wq_exp, kvn, wk_exp, wv, cos_t, sin_t):
    tokens = x2.shape[0]
    tt = min(TOK_TILE, seq)
    assert seq % tt == 0
    ns = seq // tt
    row = lambda i: (i, 0)
    const = lambda i: (0, 0)
    tab = lambda i: (i % ns, 0)
    vt = lambda i: (i // ns, 0, i % ns)
    hl = MLA_HEADS * LANES
    hv = MLA_HEADS * MLA_V_DIM
    return pl.pallas_call(
        _odd_in_kernel,
        grid=(tokens // tt,),
        in_specs=[
            pl.BlockSpec((tt, D_MODEL), row),
            pl.BlockSpec((1, D_MODEL), const),
            pl.BlockSpec((D_MODEL, 4 * LANES), const),
            pl.BlockSpec((1, MLA_Q_RANK), const),
            pl.BlockSpec((MLA_Q_RANK, hl), const),
            pl.BlockSpec((1, MLA_KV_RANK), const),
            pl.BlockSpec((MLA_KV_RANK, hl), const),
            pl.BlockSpec((MLA_KV_RANK, hv), const),
            pl.BlockSpec((tt, LANES), tab),
            pl.BlockSpec((tt, LANES), tab),
        ],
        out_specs=[
            pl.BlockSpec((tt, hl), row),
            pl.BlockSpec((tt, hl), row),
            pl.BlockSpec((1, hv, tt), vt),
        ],
        out_shape=[
            jax.ShapeDtypeStruct((tokens, hl), BF16),
            jax.ShapeDtypeStruct((tokens, hl), BF16),
            jax.ShapeDtypeStruct((bsz, hv, seq), BF16),
        ],
        compiler_params=_params(("arbitrary",)),
        name="odd_in_proj",
    )(x2, g, w_exp, qn, wq_exp, kvn, wk_exp, wv, cos_t, sin_t)


def _pair_transpose(pieces):
    outs = []
    for a in range(0, len(pieces), 2):
        outs.append(jnp.concatenate([pieces[a], pieces[a + 1]], axis=0).T)
    return outs[0] if len(outs) == 1 else jnp.concatenate(outs, axis=1)


SUM_ROWS = 16
REF_LANE = LANES - 1
MIN_CHUNK_UNROLL = 4
MAX_CHUNK_UNROLL = 8
SEED_KEYS = 128
JUMP_LIMIT = 64.0


def _bf16_round(x):
    return x.astype(BF16).astype(F32)


def _dense_attn_kernel(q_ref, k_ref, vt_ref, o_ref, qt_ref, acc_ref, ob_ref, *, kb, grp, tq, tk, seq, unroll):
    n = grp * tq
    dv = HEAD_DIM
    n_chunks = seq // tk
    ones = jnp.ones((SUM_ROWS, tk), BF16)
    row16 = lax.broadcasted_iota(jnp.int32, (16, n), 0)
    pieces = []
    for j in range(kb):
        qs = jnp.concatenate(
            [q_ref[:, (j * grp + g) * LANES:(j * grp + g + 1) * LANES] for g in range(grp)], axis=0)
        qt = qs.astype(F32).T.astype(BF16)
        qt_ref[0] = qt
        qt_ref[1] = qt

        def keys(c, j=j):
            off = pl.multiple_of(c * tk, tk)
            return k_ref[pl.ds(off, tk), j * LANES:(j + 1) * LANES]

        def values(c, j=j):
            off = pl.multiple_of(c * tk, tk)
            return jnp.concatenate([vt_ref[0, j * dv:(j + 1) * dv, pl.ds(off, tk)], ones], axis=0)

        def set_reference(buf, ref):
            qt_ref[buf, LANES - 16:, :] = jnp.where(row16 == 15, -ref, 0.0).astype(BF16)

        def chunk(c, buf, ref, ref_acc, run, jump_max):
            set_reference(buf, ref)
            t = jnp.dot(keys(c), qt_ref[buf], preferred_element_type=F32)
            jump = jnp.max(t, axis=0, keepdims=True)
            p = jnp.exp2(t.astype(BF16))
            pv = jnp.dot(values(c), p, preferred_element_type=F32)
            acc_ref[...] = jnp.exp2(ref_acc - ref) * acc_ref[...] + pv
            return jnp.maximum(run, ref + jump), jnp.maximum(jump_max, jump)

        first = k_ref[0:min(SEED_KEYS, tk), j * LANES:(j + 1) * LANES]
        run0 = jnp.max(jnp.dot(first, qt, preferred_element_type=F32), axis=0, keepdims=True)
        ref0 = _bf16_round(run0)
        acc_ref[...] = jnp.zeros((dv + SUM_ROWS, n), F32)

        def group(ci, carry, chunk=chunk):
            ref_a, ref_b, ref_acc, run, jump_max = carry
            for u in range(unroll):
                run, jump_max = chunk(ci * unroll + u, u % 2, ref_a, ref_acc, run, jump_max)
                ref_a, ref_b, ref_acc = ref_b, _bf16_round(run), ref_a
            return ref_a, ref_b, ref_acc, run, jump_max

        init = (ref0, ref0, ref0, run0, jnp.zeros((1, n), F32))
        jump_max = lax.fori_loop(0, n_chunks // unroll, group, init)[4]
        acc = acc_ref[...]
        ob_ref[j] = acc[:dv] / acc[dv:dv + 1]

        @pl.when(jnp.max(jump_max) > JUMP_LIMIT)
        def _(j=j, qt=qt, keys=keys, values=values):
            acc_ref[...] = jnp.zeros((dv + SUM_ROWS, n), F32)

            def exact(c, m_old):
                s = jnp.dot(keys(c), qt, preferred_element_type=F32)
                m_new = jnp.maximum(m_old, jnp.max(s, axis=0, keepdims=True))
                p = jnp.exp2((s - m_new).astype(BF16))
                pv = jnp.dot(values(c), p, preferred_element_type=F32)
                acc_ref[...] = jnp.exp2(m_old - m_new) * acc_ref[...] + pv
                return m_new

            lax.fori_loop(0, n_chunks, exact, jnp.full((1, n), -jnp.inf, F32))
            acc = acc_ref[...]
            ob_ref[j] = acc[:dv] / acc[dv:dv + 1]

        o = ob_ref[j]
        for g in range(grp):
            pieces.append(o[:, g * tq:(g + 1) * tq])
    o_ref[...] = _pair_transpose(pieces).astype(o_ref.dtype)


def _dense_attn(q, k, vt, bsz, seq, n_kv, grp, kb, tq):
    tokens = q.shape[0]
    tq = min(tq, seq)
    tk = min(KEY_CHUNK, seq // MIN_CHUNK_UNROLL)
    n_chunks = seq // tk
    unroll = MAX_CHUNK_UNROLL if n_chunks % MAX_CHUNK_UNROLL == 0 and n_chunks > MAX_CHUNK_UNROLL else MIN_CHUNK_UNROLL
    assert seq % tq == 0 and seq % tk == 0 and n_chunks % unroll == 0
    assert n_kv % kb == 0 and (kb * grp) % 2 == 0
    nq = seq // tq
    n = grp * tq
    kern = functools.partial(_dense_attn_kernel, kb=kb, grp=grp, tq=tq, tk=tk, seq=seq, unroll=unroll)
    return pl.pallas_call(
        kern,
        grid=(bsz, n_kv // kb, nq),
        in_specs=[
            pl.BlockSpec((tq, kb * grp * LANES), lambda b, h, i: (b * nq + i, h)),
            pl.BlockSpec((seq, kb * LANES), lambda b, h, i: (b, h)),
            pl.BlockSpec((1, kb * HEAD_DIM, seq), lambda b, h, i: (b, h, 0)),
        ],
        out_specs=pl.BlockSpec((tq, kb * grp * HEAD_DIM), lambda b, h, i: (b * nq + i, h)),
        out_shape=jax.ShapeDtypeStruct((tokens, n_kv * grp * HEAD_DIM), BF16),
        scratch_shapes=[
            pltpu.VMEM((2, LANES, n), BF16),
            pltpu.VMEM((HEAD_DIM + SUM_ROWS, n), F32),
            pltpu.VMEM((kb, HEAD_DIM, n), F32),
        ],
        compiler_params=_params(("arbitrary", "arbitrary", "arbitrary")),
        name="dense_attn",
    )(q, k, vt)


def _window_attn_kernel(q_ref, k_ref, vt_ref, slope_ref, sink_ref, o_ref, *, grp, seq):
    tq = WIN_Q
    span = 3 * WIN_Q
    n = grp * tq
    for u in range(WIN_BLOCKS):
        i = pl.program_id(2) * WIN_BLOCKS + u
        w0 = pl.multiple_of(jnp.clip((i - 1) * tq, 0, seq - span), tq)
        rows = slice(u * tq, (u + 1) * tq)
        qs = jnp.concatenate([q_ref[rows, g * LANES:(g + 1) * LANES] for g in range(grp)], axis=0)
        kc = k_ref[pl.ds(w0, span), :]
        s = lax.dot_general(kc, qs, (((1,), (1,)), ((), ())), preferred_element_type=F32)
        kpos = w0 + lax.broadcasted_iota(jnp.int32, (span, n), 0)
        qpos = i * tq + lax.broadcasted_iota(jnp.int32, (span, n), 1) % tq
        dist = jnp.abs(kpos - qpos)
        s = s - slope_ref[0] * dist.astype(F32)
        s = jnp.where(dist <= WINDOW, s, -jnp.inf)
        sink = sink_ref[0]
        m = jnp.maximum(jnp.max(s, axis=0, keepdims=True), sink)
        e = jnp.exp2(s - m)
        denom = jnp.sum(e, axis=0, keepdims=True) + jnp.exp2(sink - m)
        vc = vt_ref[0, :, pl.ds(w0, span)]
        o = jnp.dot(vc, e.astype(BF16), preferred_element_type=F32) / denom
        o_ref[rows, :] = _pair_transpose([o[:, g * tq:(g + 1) * tq] for g in range(grp)]).astype(o_ref.dtype)


def _window_attn(q, k, vt, slope_row, sink_row, bsz, seq, n_kv, grp):
    tokens = q.shape[0]
    tq = WIN_Q * WIN_BLOCKS
    assert seq % tq == 0 and seq >= 3 * WIN_Q
    nq = seq // tq
    n = grp * WIN_Q
    kern = functools.partial(_window_attn_kernel, grp=grp, seq=seq)
    return pl.pallas_call(
        kern,
        grid=(bsz, n_kv, nq),
        in_specs=[
            pl.BlockSpec((tq, grp * LANES), lambda b, h, i: (b * nq + i, h)),
            pl.BlockSpec((seq, LANES), lambda b, h, i: (b, h)),
            pl.BlockSpec((1, HEAD_DIM, seq), lambda b, h, i: (b, h, 0)),
            pl.BlockSpec((1, 1, n), lambda b, h, i: (h, 0, 0)),
            pl.BlockSpec((1, 1, n), lambda b, h, i: (h, 0, 0)),
        ],
        out_specs=pl.BlockSpec((tq, grp * HEAD_DIM), lambda b, h, i: (b * nq + i, h)),
        out_shape=jax.ShapeDtypeStruct((tokens, n_kv * grp * HEAD_DIM), BF16),
        compiler_params=_params(("arbitrary", "arbitrary", "arbitrary")),
        name="window_attn",
    )(q, k, vt, slope_row, sink_row)


def _out_proj_kernel(*refs, n_in):
    x_ref = refs[0]
    y_refs = refs[1:1 + n_in]
    w_refs = refs[1 + n_in:1 + 2 * n_in]
    o_ref = refs[1 + 2 * n_in]
    acc = x_ref[...]
    for y_ref, w_ref in zip(y_refs, w_refs):
        acc = acc + jnp.dot(y_ref[...], w_ref[...], preferred_element_type=F32)
    o_ref[...] = acc


def _out_proj(x2, ys, ws):
    tokens = x2.shape[0]
    tt = min(TOK_TILE, tokens)
    row = lambda i: (i, 0)
    const = lambda i: (0, 0)
    kern = functools.partial(_out_proj_kernel, n_in=len(ys))
    return pl.pallas_call(
        kern,
        grid=(tokens // tt,),
        in_specs=[pl.BlockSpec((tt, D_MODEL), row)]
        + [pl.BlockSpec((tt, y.shape[1]), row) for y in ys]
        + [pl.BlockSpec(w.shape, const) for w in ws],
        out_specs=pl.BlockSpec((tt, D_MODEL), row),
        out_shape=jax.ShapeDtypeStruct((tokens, D_MODEL), F32),
        compiler_params=_params(("arbitrary",)),
        name="out_proj",
    )(x2, *ys, *ws)


def _swiglu_partial(xb, wg, wu, wd):
    hg = jnp.dot(xb, wg, preferred_element_type=F32)
    hu = jnp.dot(xb, wu, preferred_element_type=F32)
    h = hg * jax.nn.sigmoid(hg) * hu
    return jnp.dot(h.astype(BF16), wd, preferred_element_type=F32)


def _ffn_kernel(x_ref, g_ref, wg_ref, wu_ref, wd_ref, o_ref, xn_ref, acc_ref):
    f = pl.program_id(1)

    @pl.when(f == 0)
    def _():
        xn_ref[...] = _rms(x_ref[...], g_ref[...]).astype(BF16)
        acc_ref[...] = jnp.zeros_like(acc_ref)

    acc_ref[...] += _swiglu_partial(xn_ref[...], wg_ref[...], wu_ref[...], wd_ref[...])

    @pl.when(f == pl.num_programs(1) - 1)
    def _():
        o_ref[...] = x_ref[...] + acc_ref[...]


def _ffn(x2, g, wg, wu, wd):
    tokens = x2.shape[0]
    ff = wg.shape[1]
    tt = min(FFN_TOK_TILE, tokens)
    fc = FFN_F_TILE
    assert tokens % tt == 0 and ff % fc == 0
    row = lambda i, f: (i, 0)
    return pl.pallas_call(
        _ffn_kernel,
        grid=(tokens // tt, ff // fc),
        in_specs=[
            pl.BlockSpec((tt, D_MODEL), row),
            pl.BlockSpec((1, D_MODEL), lambda i, f: (0, 0)),
            pl.BlockSpec((D_MODEL, fc), lambda i, f: (0, f)),
            pl.BlockSpec((D_MODEL, fc), lambda i, f: (0, f)),
            pl.BlockSpec((fc, D_MODEL), lambda i, f: (f, 0)),
        ],
        out_specs=pl.BlockSpec((tt, D_MODEL), row),
        out_shape=jax.ShapeDtypeStruct((tokens, D_MODEL), F32),
        scratch_shapes=[pltpu.VMEM((tt, D_MODEL), BF16), pltpu.VMEM((tt, D_MODEL), F32)],
        compiler_params=_params(("arbitrary", "arbitrary")),
        name="ffn",
    )(x2, g, wg, wu, wd)


def _router_kernel(x_ref, g_ref, rh_ref, rl_ref, xn_ref, ti_ref, tw_ref):
    tt = x_ref.shape[0]
    xn = _rms(x_ref[...], g_ref[...])
    xn_ref[...] = xn
    xh = xn.astype(BF16)
    xl = (xn - xh.astype(F32)).astype(BF16)
    logits = (jnp.dot(xh, rh_ref[...], preferred_element_type=F32)
              + jnp.dot(xh, rl_ref[...], preferred_element_type=F32)
              + jnp.dot(xl, rh_ref[...], preferred_element_type=F32))
    lane = lax.broadcasted_iota(jnp.int32, (tt, LANES), 1).astype(F32)
    lg = jnp.where(lane < N_EXPERTS, logits, -jnp.inf)
    m1 = jnp.max(lg, axis=1, keepdims=True)
    i1 = jnp.min(jnp.where(lg == m1, lane, float(LANES)), axis=1, keepdims=True)
    lg2 = jnp.where(lane == i1, -jnp.inf, lg)
    m2 = jnp.max(lg2, axis=1, keepdims=True)
    i2 = jnp.min(jnp.where(lg2 == m2, lane, float(LANES)), axis=1, keepdims=True)
    e2 = jnp.exp(m2 - m1)
    w1 = 1.0 / (1.0 + e2)
    w2 = e2 * w1
    ti_ref[...] = jnp.where(lane == 0.0, i1, i2).T[:8].astype(jnp.int32)
    tw_ref[...] = jnp.where(lane == 0.0, w1, w2)


def _router(x2, g, r_hi, r_lo):
    tokens = x2.shape[0]
    tt = min(TOK_TILE, tokens)
    row = lambda i: (i, 0)
    const = lambda i: (0, 0)
    return pl.pallas_call(
        _router_kernel,
        grid=(tokens // tt,),
        in_specs=[
            pl.BlockSpec((tt, D_MODEL), row),
            pl.BlockSpec((1, D_MODEL), const),
            pl.BlockSpec((D_MODEL, LANES), const),
            pl.BlockSpec((D_MODEL, LANES), const),
        ],
        out_specs=[
            pl.BlockSpec((tt, D_MODEL), row),
            pl.BlockSpec((8, tt), lambda i: (0, i)),
            pl.BlockSpec((tt, LANES), row),
        ],
        out_shape=[
            jax.ShapeDtypeStruct((tokens, D_MODEL), F32),
            jax.ShapeDtypeStruct((8, tokens), jnp.int32),
            jax.ShapeDtypeStruct((tokens, LANES), F32),
        ],
        compiler_params=_params(("arbitrary",)),
        name="moe_router",
    )(x2, g, r_hi, r_lo)


def _for_rows(n_rows, fn):
    def group(u, carry):
        for v in range(DMA_UNROLL):
            fn(u * DMA_UNROLL + v)
        return carry
    lax.fori_loop(0, n_rows // DMA_UNROLL, group, 0)


def _expert_ffn_kernel(be_ref, nu_ref, dst_ref, xn_hbm, wg_ref, wu_ref, wd_ref, y_hbm,
                       xg_ref, xb_ref, acc_ref, yb_ref, gsem, ssem, *, tokens):
    b = pl.program_id(0)
    f = pl.program_id(1)
    nf = pl.num_programs(1)
    nu = nu_ref[0]
    used = b < nu
    last = f == nf - 1
    slot = b % 2
    m = MOE_ROW_TILE
    per_step = m // EXPERT_F_STEPS
    nxt = jnp.minimum(b + 1, nu - 1)

    def gather(blk, r, sl):
        tok = lax.rem(dst_ref[blk * m + r], tokens)
        return pltpu.make_async_copy(xn_hbm.at[pl.ds(tok, 1)], xg_ref.at[sl, pl.ds(r, 1)], gsem.at[sl])

    def scatter(blk, r):
        return pltpu.make_async_copy(yb_ref.at[pl.ds(r, 1)], y_hbm.at[pl.ds(dst_ref[blk * m + r], 1)], ssem)

    @pl.when(jnp.logical_and(used, jnp.logical_and(f == 0, b == 0)))
    def _():
        _for_rows(m, lambda r: gather(0, r, 0).start())

    @pl.when(jnp.logical_and(used, f == 0))
    def _():
        _for_rows(m, lambda r: gather(b, r, slot).wait())
        xb_ref[...] = xg_ref[slot].astype(BF16)
        acc_ref[...] = jnp.zeros_like(acc_ref)

    def step(with_scatter):
        for v in range(per_step):
            r = f * per_step + v
            gather(nxt, r, 1 - slot).start()
            if with_scatter:
                scatter(b - 1, r).start()
        acc_ref[...] += _swiglu_partial(xb_ref[...], wg_ref[0], wu_ref[0], wd_ref[0])

    @pl.when(jnp.logical_and(used, b == 0))
    def _():
        step(False)

    @pl.when(jnp.logical_and(used, b > 0))
    def _():
        step(True)

    @pl.when(jnp.logical_and(used, jnp.logical_and(last, b > 0)))
    def _():
        _for_rows(m, lambda r: scatter(b - 1, r).wait())

    @pl.when(jnp.logical_and(used, last))
    def _():
        yb_ref[...] = acc_ref[...]

    @pl.when(jnp.logical_and(last, b == nu - 1))
    def _():
        _for_rows(m, lambda r: scatter(b, r).start())
        _for_rows(m, lambda r: gather(nxt, r, 1 - slot).wait())
        _for_rows(m, lambda r: scatter(b, r).wait())


def _expert_ffn(xn, dst, blk_expert, n_used, wg, wu, wd):
    tokens = xn.shape[0]
    rows = dst.shape[0]
    ff = wg.shape[2]
    m = MOE_ROW_TILE
    assert rows % m == 0 and ff % EXPERT_F_STEPS == 0 and m % EXPERT_F_STEPS == 0
    fc = ff // EXPERT_F_STEPS
    assert fc % LANES == 0
    return pl.pallas_call(
        functools.partial(_expert_ffn_kernel, tokens=tokens),
        grid_spec=pltpu.PrefetchScalarGridSpec(
            num_scalar_prefetch=3,
            grid=(rows // m, EXPERT_F_STEPS),
            in_specs=[
                pl.BlockSpec(memory_space=pl.ANY),
                pl.BlockSpec((1, D_MODEL, fc), lambda b, f, be, nu, ds: (be[b], 0, f)),
                pl.BlockSpec((1, D_MODEL, fc), lambda b, f, be, nu, ds: (be[b], 0, f)),
                pl.BlockSpec((1, fc, D_MODEL), lambda b, f, be, nu, ds: (be[b], f, 0)),
            ],
            out_specs=pl.BlockSpec(memory_space=pl.ANY),
            scratch_shapes=[
                pltpu.VMEM((2, m, D_MODEL), F32),
                pltpu.VMEM((m, D_MODEL), BF16),
                pltpu.VMEM((m, D_MODEL), F32),
                pltpu.VMEM((m, D_MODEL), F32),
                pltpu.SemaphoreType.DMA((2,)),
                pltpu.SemaphoreType.DMA,
            ],
        ),
        out_shape=jax.ShapeDtypeStruct((TOP_K * tokens + rows, D_MODEL), F32),
        compiler_params=_params(("arbitrary", "arbitrary")),
        name="expert_ffn",
    )(blk_expert, n_used, dst, xn, wg, wu, wd)


def _moe_combine_kernel(x_ref, y0_ref, y1_ref, tw_ref, gf_ref, o_ref, *, final_norm):
    tw = tw_ref[...]
    y = x_ref[...] + tw[:, 0:1] * y0_ref[...] + tw[:, 1:2] * y1_ref[...]
    if final_norm:
        y = _rms(y, gf_ref[...])
    o_ref[...] = y


def _moe_combine(x2, y, top_w, g_final, final_norm):
    tokens = x2.shape[0]
    tt = min(TOK_TILE, tokens)
    nt = tokens // tt
    row = lambda i: (i, 0)
    return pl.pallas_call(
        functools.partial(_moe_combine_kernel, final_norm=final_norm),
        grid=(nt,),
        in_specs=[
            pl.BlockSpec((tt, D_MODEL), row),
            pl.BlockSpec((tt, D_MODEL), row),
            pl.BlockSpec((tt, D_MODEL), lambda i: (nt + i, 0)),
            pl.BlockSpec((tt, LANES), row),
            pl.BlockSpec((1, D_MODEL), lambda i: (0, 0)),
        ],
        out_specs=pl.BlockSpec((tt, D_MODEL), row),
        out_shape=jax.ShapeDtypeStruct((tokens, D_MODEL), F32),
        compiler_params=_params(("arbitrary",)),
        name="moe_combine",
    )(x2, y, y, top_w, g_final)


def _route(top_i):
    m = MOE_ROW_TILE
    tokens = top_i.shape[1]
    n_assign = tokens * TOP_K
    rows = -(-(n_assign + N_EXPERTS * m) // m) * m
    e_flat = top_i.reshape(-1)
    counts = jnp.sum((e_flat[:, None] == jnp.arange(N_EXPERTS, dtype=jnp.int32)[None, :]).astype(jnp.int32), axis=0)
    padded = (counts + m - 1) // m * m
    ends = jnp.cumsum(padded)
    starts = ends - padded
    ustarts = jnp.cumsum(counts) - counts
    order = jnp.argsort(e_flat, stable=True).astype(jnp.int32)
    slot = jnp.arange(rows, dtype=jnp.int32)
    e_slot = jnp.minimum(jnp.sum((slot[:, None] >= ends[None, :]).astype(jnp.int32), axis=1), N_EXPERTS - 1)
    j = slot - starts[e_slot]
    valid = jnp.logical_and(j < counts[e_slot], slot < ends[-1])
    a = order[jnp.clip(ustarts[e_slot] + j, 0, n_assign - 1)]
    dst = jnp.where(valid, a, n_assign + slot).astype(jnp.int32)
    n_used = (ends[-1] // m).astype(jnp.int32)[None]
    return dst, e_slot[::m], n_used


def _moe(x2, g, r_hi, r_lo, wg, wu, wd, layer, g_final, final_norm):
    xn, top_i, top_w = _router(x2, g, r_hi, r_lo)
    dst, blk_expert, n_used = _route(top_i[:TOP_K])
    y = _expert_ffn(xn, dst, blk_expert + layer * N_EXPERTS, n_used, wg, wu, wd)
    return _moe_combine(x2, y, top_w, g_final, final_norm)


def _pad_heads(w, n_heads, width, offset=0):
    r = w.shape[0]
    w = w.reshape(r, n_heads, width)
    w = jnp.pad(w, ((0, 0), (0, 0), (offset, LANES - width - offset)))
    return w.reshape(r, n_heads * LANES)


def _rope_tables(ang, base):
    half = ang.shape[1]
    cos = jnp.cos(ang)
    sin = jnp.sin(ang)
    pad = ((0, 0), (base, LANES - base - 2 * half))
    cos_t = jnp.pad(jnp.concatenate([cos, cos], axis=1) - 1.0, pad) + 1.0
    sin_t = jnp.pad(jnp.concatenate([-sin, sin], axis=1), pad)
    return cos_t, sin_t


def _rope_angles(pos, dim):
    inv = ROPE_THETA ** (-jnp.arange(0, dim, 2, dtype=F32) / dim)
    return pos.astype(F32)[:, None] * inv[None, :]


def _lane_gain(g):
    return jnp.pad(g.astype(F32), (0, LANES - g.shape[0]))[None, :]


def _trunk(x, p):
    bsz, seq, _ = x.shape
    x2 = x.reshape(bsz * seq, D_MODEL)
    n_rows = seq // GRID_W
    row = jnp.repeat(jnp.arange(n_rows), GRID_W)
    col = jnp.tile(jnp.arange(GRID_W), n_rows)
    ang_axial = jnp.concatenate([_rope_angles(row, HEAD_DIM // 2), _rope_angles(col, HEAD_DIM // 2)], axis=-1)
    cos_ax, sin_ax = _rope_tables(ang_axial, 0)
    cos_1d, sin_1d = _rope_tables(_rope_angles(jnp.arange(seq), MLA_ROPE_DIM), _MLA_ROPE_BASE)
    depth = p["ev_norm_mix"].shape[0] + p["od_norm_mix"].shape[0]
    assert depth % 2 == 0
    a_grp = A_HEADS // A_KV_HEADS
    for layer in range(depth):
        i = layer // 2
        last_layer = layer == depth - 1
        if layer % 2 == 0:
            aq, ak, avt, bq, bk, bvt = _even_in(
                x2, bsz, seq, p["ev_norm_mix"][i][None, :], p["ev_w_in"][i], cos_ax, sin_ax,
                p["b_q_norm"][i], p["b_k_norm"][i])
            ya = _window_attn(aq, ak, avt, p["a_slope"], p["a_sink"][i], bsz, seq, A_KV_HEADS, a_grp)
            yb = _dense_attn(bq, bk, bvt, bsz, seq, B_KV_HEADS, B_HEADS // B_KV_HEADS, 1, 256)
            x2 = _out_proj(x2, [ya, yb], [p["ev_w_out_a"][i], p["ev_w_out_b"][i]])
            x2 = _ffn(x2, p["ev_norm_ffn"][i][None, :], p["ffn_w_gate"][i], p["ffn_w_up"][i], p["ffn_w_down"][i])
        else:
            q, k, vt = _odd_in(
                x2, bsz, seq, p["od_norm_mix"][i][None, :], p["od_w_in"][i], p["mla_q_norm"][i][None, :],
                p["mla_w_q_up"][i], p["mla_kv_norm"][i][None, :], p["mla_w_k_up"][i], p["mla_w_v_up"][i],
                cos_1d, sin_1d)
            yc = _dense_attn(q, k, vt, bsz, seq, MLA_HEADS, 1, 2, 1024)
            x2 = _out_proj(x2, [yc], [p["od_w_out"][i]])
            x2 = _moe(x2, p["od_norm_ffn"][i][None, :], p["router_hi"][i], p["router_lo"][i],
                      p["moe_w_gate"], p["moe_w_up"], p["moe_w_down"], i,
                      p["final_norm"], last_layer)
    return x2.reshape(bsz, seq, D_MODEL)


def _prepare(ev_norm_mix, ev_w_in, a_sink, b_q_norm, b_k_norm, ev_w_out, ev_norm_ffn,
             ffn_w_gate, ffn_w_up, ffn_w_down, od_norm_mix, od_w_in, mla_q_norm, mla_w_q_up,
             mla_kv_norm, mla_w_kv_up, od_w_out, od_norm_ffn, moe_router, moe_w_gate, moe_w_up,
             moe_w_down, final_norm):
    hd = HEAD_DIM
    a_grp = A_HEADS // A_KV_HEADS

    def even_w_in(w):
        sizes = [A_HEADS * hd, A_KV_HEADS * hd, A_KV_HEADS * hd, B_HEADS * hd, B_KV_HEADS * hd, B_KV_HEADS * hd]
        aq, ak, av, bq, bk, bv = jnp.split(w, list(np.cumsum(sizes)[:-1]), axis=-1)
        return jnp.concatenate([_pad_heads(aq, A_HEADS, hd), _pad_heads(ak, A_KV_HEADS, hd),
                                _pad_heads(bq, B_HEADS, hd), _pad_heads(bk, B_KV_HEADS, hd), av, bv],
                               axis=-1).astype(BF16)

    def odd_w_in(w):
        c = w[:, :MLA_Q_RANK + MLA_KV_RANK]
        kr = _pad_heads(w[:, MLA_Q_RANK + MLA_KV_RANK:], 1, MLA_ROPE_DIM, _MLA_ROPE_BASE)
        return jnp.concatenate([c, kr], axis=-1).astype(BF16)

    def q_up(w):
        return _pad_heads(w, MLA_HEADS, MLA_QK_DIM).astype(BF16)

    def kv_up(w):
        w = w.reshape(MLA_KV_RANK, MLA_HEADS, MLA_NOPE_DIM + MLA_V_DIM)
        wk = _pad_heads(w[:, :, :MLA_NOPE_DIM].reshape(MLA_KV_RANK, -1), MLA_HEADS, MLA_NOPE_DIM)
        wv = w[:, :, MLA_NOPE_DIM:].reshape(MLA_KV_RANK, -1)
        return wk.astype(BF16), wv.astype(BF16)

    slopes = jnp.asarray(2.0 ** (-8.0 * np.arange(1, A_HEADS + 1) / A_HEADS), dtype=F32)
    per_col = lambda v: jnp.repeat(v.astype(F32).reshape(A_KV_HEADS, a_grp), WIN_Q, axis=1)[:, None, :] * LOG2E
    router = jnp.pad(moe_router.astype(F32), ((0, 0), (0, 0), (0, LANES - N_EXPERTS)))
    router_hi = router.astype(BF16)
    kv = [kv_up(w) for w in mla_w_kv_up]
    return {
        "ev_norm_mix": ev_norm_mix, "ev_w_in": jnp.stack([even_w_in(w) for w in ev_w_in]),
        "a_slope": per_col(slopes), "a_sink": jnp.stack([per_col(s) for s in a_sink]),
        "b_q_norm": jnp.stack([_lane_gain(g) for g in b_q_norm]),
        "b_k_norm": jnp.stack([_lane_gain(g) for g in b_k_norm]),
        "ev_w_out_a": ev_w_out[:, :A_HEADS * hd].astype(BF16),
        "ev_w_out_b": ev_w_out[:, A_HEADS * hd:].astype(BF16),
        "ev_norm_ffn": ev_norm_ffn,
        "ffn_w_gate": ffn_w_gate.astype(BF16), "ffn_w_up": ffn_w_up.astype(BF16),
        "ffn_w_down": ffn_w_down.astype(BF16),
        "od_norm_mix": od_norm_mix, "od_w_in": jnp.stack([odd_w_in(w) for w in od_w_in]),
        "mla_q_norm": mla_q_norm, "mla_w_q_up": jnp.stack([q_up(w) for w in mla_w_q_up]),
        "mla_kv_norm": mla_kv_norm,
        "mla_w_k_up": jnp.stack([a for a, _ in kv]), "mla_w_v_up": jnp.stack([b for _, b in kv]),
        "od_w_out": od_w_out.astype(BF16), "od_norm_ffn": od_norm_ffn,
        "router_hi": router_hi, "router_lo": (router - router_hi.astype(F32)).astype(BF16),
        "moe_w_gate": moe_w_gate.astype(BF16).reshape((-1,) + moe_w_gate.shape[2:]),
        "moe_w_up": moe_w_up.astype(BF16).reshape((-1,) + moe_w_up.shape[2:]),
        "moe_w_down": moe_w_down.astype(BF16).reshape((-1,) + moe_w_down.shape[2:]),
        "final_norm": final_norm[None, :],
    }


def kernel(x_prompt, x_sample, ev_norm_mix, ev_w_in, a_sink, b_q_norm, b_k_norm, ev_w_out, ev_norm_ffn, ffn_w_gate, ffn_w_up, ffn_w_down, od_norm_mix, od_w_in, mla_q_norm, mla_w_q_up, mla_kv_norm, mla_w_kv_up, od_w_out, od_norm_ffn, moe_router, moe_w_gate, moe_w_up, moe_w_down, final_norm):
    p = _prepare(ev_norm_mix, ev_w_in, a_sink, b_q_norm, b_k_norm, ev_w_out, ev_norm_ffn,
                 ffn_w_gate, ffn_w_up, ffn_w_down, od_norm_mix, od_w_in, mla_q_norm, mla_w_q_up,
                 mla_kv_norm, mla_w_kv_up, od_w_out, od_norm_ffn, moe_router, moe_w_gate, moe_w_up,
                 moe_w_down, final_norm)
    return (_trunk(x_prompt, p), _trunk(x_sample, p))
```

```python
import functools
import math

import numpy as np
import jax
import jax.numpy as jnp
from jax import lax
from jax.experimental import pallas as pl
from jax.experimental.pallas import tpu as pltpu

F32 = jnp.float32
BF16 = jnp.bfloat16

D_MODEL = 1024
HEAD_DIM = 64
WINDOW = 128
GRID_W = 64
ROPE_THETA = 10000.0
NORM_EPS = 1e-6
A_HEADS, A_KV_HEADS = 8, 2
B_HEADS, B_KV_HEADS = 8, 2
MLA_HEADS = 8
MLA_Q_RANK, MLA_KV_RANK = 256, 128
MLA_NOPE_DIM, MLA_ROPE_DIM, MLA_V_DIM = 64, 32, 64
MLA_QK_DIM = MLA_NOPE_DIM + MLA_ROPE_DIM
N_EXPERTS = 8
TOP_K = 2
LANES = 128
LOG2E = math.log2(math.e)
VMEM_LIMIT = 48 * 1024 * 1024

TOK_TILE = 512
FFN_TOK_TILE = 1024
FFN_F_TILE = 512
KEY_CHUNK = 512
WIN_Q = 128
WIN_BLOCKS = 4
MOE_ROW_TILE = 896
EXPERT_F_STEPS = 7
DMA_UNROLL = 8


def _rms(x, g):
    return x * lax.rsqrt(jnp.mean(x * x, axis=-1, keepdims=True) + NORM_EPS) * g


def _swap_halves(x, half):
    lane = lax.broadcasted_iota(jnp.int32, x.shape, 1)
    return jnp.where(lane < half, pltpu.roll(x, LANES - half, 1), pltpu.roll(x, half, 1))


def _swap_halves_at(x, base, half):
    lane = lax.broadcasted_iota(jnp.int32, x.shape, 1)
    return jnp.where(lane < base + half, pltpu.roll(x, LANES - half, 1), pltpu.roll(x, half, 1))


def _params(sem):
    return pltpu.CompilerParams(dimension_semantics=sem, vmem_limit_bytes=VMEM_LIMIT)


_EV_AQ, _EV_AK, _EV_BQ, _EV_BK, _EV_V, _EV_END = 0, 1024, 1280, 2304, 2560, 2816


def _even_in_kernel(x_ref, g_ref, w_ref, cos_ref, sin_ref, qg_ref, kg_ref,
                    aq_ref, ak_ref, avt_ref, bq_ref, bk_ref, bvt_ref):
    xb = _rms(x_ref[...], g_ref[...]).astype(BF16)
    cos = cos_ref[...]
    sin = sin_ref[...]
    qscale = (HEAD_DIM ** -0.5) * LOG2E

    def norm_rope(blk, gain):
        ms = jnp.sum(blk * blk, axis=-1, keepdims=True) * (1.0 / HEAD_DIM)
        y = blk * lax.rsqrt(ms + NORM_EPS) * gain
        return y * cos + _swap_halves(y, HEAD_DIM // 2) * sin

    pa = jnp.dot(xb, w_ref[:, _EV_AQ:_EV_AK], preferred_element_type=F32)
    aq_ref[...] = (pa * qscale).astype(BF16)
    ak_ref[...] = jnp.dot(xb, w_ref[:, _EV_AK:_EV_BQ], preferred_element_type=F32).astype(BF16)
    pbq = jnp.dot(xb, w_ref[:, _EV_BQ:_EV_BK], preferred_element_type=F32)
    for h in range(B_HEADS):
        sl = slice(h * LANES, (h + 1) * LANES)
        bq_ref[:, sl] = (norm_rope(pbq[:, sl], qg_ref[...]) * qscale).astype(BF16)
    pbk = jnp.dot(xb, w_ref[:, _EV_BK:_EV_V], preferred_element_type=F32)
    lane = lax.broadcasted_iota(jnp.int32, (pbk.shape[0], LANES), 1)
    for h in range(B_KV_HEADS):
        sl = slice(h * LANES, (h + 1) * LANES)
        bk_ref[:, sl] = jnp.where(lane == REF_LANE, 1.0, norm_rope(pbk[:, sl], kg_ref[...])).astype(BF16)
    pv = jnp.dot(xb, w_ref[:, _EV_V:_EV_END], preferred_element_type=F32)
    avt_ref[0] = pv[:, :LANES].T.astype(BF16)
    bvt_ref[0] = pv[:, LANES:].T.astype(BF16)


def _even_in(x2, bsz, seq, g, w_exp, cos_t, sin_t, qg, kg):
    tokens = x2.shape[0]
    tt = min(TOK_TILE, seq)
    assert seq % tt == 0
    ns = seq // tt
    row = lambda i: (i, 0)
    const = lambda i: (0, 0)
    tab = lambda i: (i % ns, 0)
    vt = lambda i: (i // ns, 0, i % ns)
    return pl.pallas_call(
        _even_in_kernel,
        grid=(tokens // tt,),
        in_specs=[
            pl.BlockSpec((tt, D_MODEL), row),
            pl.BlockSpec((1, D_MODEL), const),
            pl.BlockSpec((D_MODEL, _EV_END), const),
            pl.BlockSpec((tt, LANES), tab),
            pl.BlockSpec((tt, LANES), tab),
            pl.BlockSpec((1, LANES), const),
            pl.BlockSpec((1, LANES), const),
        ],
        out_specs=[
            pl.BlockSpec((tt, A_HEADS * LANES), row),
            pl.BlockSpec((tt, A_KV_HEADS * LANES), row),
            pl.BlockSpec((1, A_KV_HEADS * HEAD_DIM, tt), vt),
            pl.BlockSpec((tt, B_HEADS * LANES), row),
            pl.BlockSpec((tt, B_KV_HEADS * LANES), row),
            pl.BlockSpec((1, B_KV_HEADS * HEAD_DIM, tt), vt),
        ],
        out_shape=[
            jax.ShapeDtypeStruct((tokens, A_HEADS * LANES), BF16),
            jax.ShapeDtypeStruct((tokens, A_KV_HEADS * LANES), BF16),
            jax.ShapeDtypeStruct((bsz, A_KV_HEADS * HEAD_DIM, seq), BF16),
            jax.ShapeDtypeStruct((tokens, B_HEADS * LANES), BF16),
            jax.ShapeDtypeStruct((tokens, B_KV_HEADS * LANES), BF16),
            jax.ShapeDtypeStruct((bsz, B_KV_HEADS * HEAD_DIM, seq), BF16),
        ],
        compiler_params=_params(("arbitrary",)),
        name="even_in_proj",
    )(x2, g, w_exp, cos_t, sin_t, qg, kg)


_MLA_ROPE_BASE = MLA_NOPE_DIM


def _odd_in_kernel(x_ref, g_ref, w_ref, qn_ref, wq_ref, kvn_ref, wk_ref, wv_ref,
                   cos_ref, sin_ref, q_ref, k_ref, vt_ref):
    xb = _rms(x_ref[...], g_ref[...]).astype(BF16)
    cos = cos_ref[...]
    sin = sin_ref[...]
    qscale = (MLA_QK_DIM ** -0.5) * LOG2E

    def rope(blk):
        return blk * cos + _swap_halves_at(blk, _MLA_ROPE_BASE, MLA_ROPE_DIM // 2) * sin

    proj = jnp.dot(xb, w_ref[...], preferred_element_type=F32)
    cq = _rms(proj[:, :MLA_Q_RANK], qn_ref[...]).astype(BF16)
    ckv = _rms(proj[:, MLA_Q_RANK:MLA_Q_RANK + MLA_KV_RANK], kvn_ref[...]).astype(BF16)
    kr = rope(proj[:, MLA_Q_RANK + MLA_KV_RANK:])
    q = jnp.dot(cq, wq_ref[...], preferred_element_type=F32)
    kn = jnp.dot(ckv, wk_ref[...], preferred_element_type=F32)
    lane = lax.broadcasted_iota(jnp.int32, (kr.shape[0], LANES), 1)
    for h in range(MLA_HEADS):
        sl = slice(h * LANES, (h + 1) * LANES)
        q_ref[:, sl] = (rope(q[:, sl]) * qscale).astype(BF16)
        k_ref[:, sl] = jnp.where(lane == REF_LANE, 1.0, kn[:, sl] + kr).astype(BF16)
    v = jnp.dot(ckv, wv_ref[...], preferred_element_type=F32)
    for c in range(MLA_HEADS * MLA_V_DIM // LANES):
        vt_ref[0, c * LANES:(c + 1) * LANES, :] = v[:, c * LANES:(c + 1) * LANES].T.astype(BF16)


def _odd_in(x2, bsz, seq, g, w_exp, qn, wq_exp, kvn, wk_exp, wv, cos_t, sin_t):
    tokens = x2.shape[0]
    tt = min(TOK_TILE, seq)
    assert seq % tt == 0
    ns = seq // tt
    row = lambda i: (i, 0)
    const = lambda i: (0, 0)
    tab = lambda i: (i % ns, 0)
    vt = lambda i: (i // ns, 0, i % ns)
    hl = MLA_HEADS * LANES
    hv = MLA_HEADS * MLA_V_DIM
    return pl.pallas_call(
        _odd_in_kernel,
        grid=(tokens // tt,),
        in_specs=[
            pl.BlockSpec((tt, D_MODEL), row),
            pl.BlockSpec((1, D_MODEL), const),
            pl.BlockSpec((D_MODEL, 4 * LANES), const),
            pl.BlockSpec((1, MLA_Q_RANK), const),
            pl.BlockSpec((MLA_Q_RANK, hl), const),
            pl.BlockSpec((1, MLA_KV_RANK), const),
            pl.BlockSpec((MLA_KV_RANK, hl), const),
            pl.BlockSpec((MLA_KV_RANK, hv), const),
            pl.BlockSpec((tt, LANES), tab),
            pl.BlockSpec((tt, LANES), tab),
        ],
        out_specs=[
            pl.BlockSpec((tt, hl), row),
            pl.BlockSpec((tt, hl), row),
            pl.BlockSpec((1, hv, tt), vt),
        ],
        out_shape=[
            jax.ShapeDtypeStruct((tokens, hl), BF16),
            jax.ShapeDtypeStruct((tokens, hl), BF16),
            jax.ShapeDtypeStruct((bsz, hv, seq), BF16),
        ],
        compiler_params=_params(("arbitrary",)),
        name="odd_in_proj",
    )(x2, g, w_exp, qn, wq_exp, kvn, wk_exp, wv, cos_t, sin_t)


def _pair_transpose(pieces):
    outs = []
    for a in range(0, len(pieces), 2):
        outs.append(jnp.concatenate([pieces[a], pieces[a + 1]], axis=0).T)
    return outs[0] if len(outs) == 1 else jnp.concatenate(outs, axis=1)


SUM_ROWS = 16
REF_LANE = LANES - 1
MIN_CHUNK_UNROLL = 4
MAX_CHUNK_UNROLL = 8
SEED_KEYS = 128
JUMP_LIMIT = 64.0


def _bf16_round(x):
    return x.astype(BF16).astype(F32)


def _dense_attn_kernel(q_ref, k_ref, vt_ref, o_ref, qt_ref, acc_ref, ob_ref, *, kb, grp, tq, tk, seq, unroll):
    n = grp * tq
    dv = HEAD_DIM
    n_chunks = seq // tk
    ones = jnp.ones((SUM_ROWS, tk), BF16)
    row16 = lax.broadcasted_iota(jnp.int32, (16, n), 0)
    pieces = []
    for j in range(kb):
        qs = jnp.concatenate(
            [q_ref[:, (j * grp + g) * LANES:(j * grp + g + 1) * LANES] for g in range(grp)], axis=0)
        qt = qs.astype(F32).T.astype(BF16)
        qt_ref[0] = qt
        qt_ref[1] = qt

        def keys(c, j=j):
            off = pl.multiple_of(c * tk, tk)
            return k_ref[pl.ds(off, tk), j * LANES:(j + 1) * LANES]

        def values(c, j=j):
            off = pl.multiple_of(c * tk, tk)
            return jnp.concatenate([vt_ref[0, j * dv:(j + 1) * dv, pl.ds(off, tk)], ones], axis=0)

        def set_reference(buf, ref):
            qt_ref[buf, LANES - 16:, :] = jnp.where(row16 == 15, -ref, 0.0).astype(BF16)

        def chunk(c, buf, ref, ref_acc, run, jump_max):
            set_reference(buf, ref)
            t = jnp.dot(keys(c), qt_ref[buf], preferred_element_type=F32)
            jump = jnp.max(t, axis=0, keepdims=True)
            p = jnp.exp2(t.astype(BF16))
            pv = jnp.dot(values(c), p, preferred_element_type=F32)
            acc_ref[...] = jnp.exp2(ref_acc - ref) * acc_ref[...] + pv
            return jnp.maximum(run, ref + jump), jnp.maximum(jump_max, jump)

        first = k_ref[0:min(SEED_KEYS, tk), j * LANES:(j + 1) * LANES]
        run0 = jnp.max(jnp.dot(first, qt, preferred_element_type=F32), axis=0, keepdims=True)
        ref0 = _bf16_round(run0)
        acc_ref[...] = jnp.zeros((dv + SUM_ROWS, n), F32)

        def group(ci, carry, chunk=chunk):
            ref_a, ref_b, ref_acc, run, jump_max = carry
            for u in range(unroll):
                run, jump_max = chunk(ci * unroll + u, u % 2, ref_a, ref_acc, run, jump_max)
                ref_a, ref_b, ref_acc = ref_b, _bf16_round(run), ref_a
            return ref_a, ref_b, ref_acc, run, jump_max

        init = (ref0, ref0, ref0, run0, jnp.zeros((1, n), F32))
        jump_max = lax.fori_loop(0, n_chunks // unroll, group, init)[4]
        acc = acc_ref[...]
        ob_ref[j] = acc[:dv] / acc[dv:dv + 1]

        @pl.when(jnp.max(jump_max) > JUMP_LIMIT)
        def _(j=j, qt=qt, keys=keys, values=values):
            acc_ref[...] = jnp.zeros((dv + SUM_ROWS, n), F32)

            def exact(c, m_old):
                s = jnp.dot(keys(c), qt, preferred_element_type=F32)
                m_new = jnp.maximum(m_old, jnp.max(s, axis=0, keepdims=True))
                p = jnp.exp2((s - m_new).astype(BF16))
                pv = jnp.dot(values(c), p, preferred_element_type=F32)
                acc_ref[...] = jnp.exp2(m_old - m_new) * acc_ref[...] + pv
                return m_new

            lax.fori_loop(0, n_chunks, exact, jnp.full((1, n), -jnp.inf, F32))
            acc = acc_ref[...]
            ob_ref[j] = acc[:dv] / acc[dv:dv + 1]

        o = ob_ref[j]
        for g in range(grp):
            pieces.append(o[:, g * tq:(g + 1) * tq])
    o_ref[...] = _pair_transpose(pieces).astype(o_ref.dtype)


def _dense_attn(q, k, vt, bsz, seq, n_kv, grp, kb, tq):
    tokens = q.shape[0]
    tq = min(tq, seq)
    tk = min(KEY_CHUNK, seq // MIN_CHUNK_UNROLL)
    n_chunks = seq // tk
    unroll = MAX_CHUNK_UNROLL if n_chunks % MAX_CHUNK_UNROLL == 0 and n_chunks > MAX_CHUNK_UNROLL else MIN_CHUNK_UNROLL
    assert seq % tq == 0 and seq % tk == 0 and n_chunks % unroll == 0
    assert n_kv % kb == 0 and (kb * grp) % 2 == 0
    nq = seq // tq
    n = grp * tq
    kern = functools.partial(_dense_attn_kernel, kb=kb, grp=grp, tq=tq, tk=tk, seq=seq, unroll=unroll)
    return pl.pallas_call(
        kern,
        grid=(bsz, n_kv // kb, nq),
        in_specs=[
            pl.BlockSpec((tq, kb * grp * LANES), lambda b, h, i: (b * nq + i, h)),
            pl.BlockSpec((seq, kb * LANES), lambda b, h, i: (b, h)),
            pl.BlockSpec((1, kb * HEAD_DIM, seq), lambda b, h, i: (b, h, 0)),
        ],
        out_specs=pl.BlockSpec((tq, kb * grp * HEAD_DIM), lambda b, h, i: (b * nq + i, h)),
        out_shape=jax.ShapeDtypeStruct((tokens, n_kv * grp * HEAD_DIM), BF16),
        scratch_shapes=[
            pltpu.VMEM((2, LANES, n), BF16),
            pltpu.VMEM((HEAD_DIM + SUM_ROWS, n), F32),
            pltpu.VMEM((kb, HEAD_DIM, n), F32),
        ],
        compiler_params=_params(("arbitrary", "arbitrary", "arbitrary")),
        name="dense_attn",
    )(q, k, vt)


def _window_attn_kernel(q_ref, k_ref, vt_ref, slope_ref, sink_ref, o_ref, *, grp, seq):
    tq = WIN_Q
    span = 3 * WIN_Q
    n = grp * tq
    for u in range(WIN_BLOCKS):
        i = pl.program_id(2) * WIN_BLOCKS + u
        w0 = pl.multiple_of(jnp.clip((i - 1) * tq, 0, seq - span), tq)
        rows = slice(u * tq, (u + 1) * tq)
        qs = jnp.concatenate([q_ref[rows, g * LANES:(g + 1) * LANES] for g in range(grp)], axis=0)
        kc = k_ref[pl.ds(w0, span), :]
        s = lax.dot_general(kc, qs, (((1,), (1,)), ((), ())), preferred_element_type=F32)
        kpos = w0 + lax.broadcasted_iota(jnp.int32, (span, n), 0)
        qpos = i * tq + lax.broadcasted_iota(jnp.int32, (span, n), 1) % tq
        dist = jnp.abs(kpos - qpos)
        s = s - slope_ref[0] * dist.astype(F32)
        s = jnp.where(dist <= WINDOW, s, -jnp.inf)
        sink = sink_ref[0]
        m = jnp.maximum(jnp.max(s, axis=0, keepdims=True), sink)
        e = jnp.exp2(s - m)
        denom = jnp.sum(e, axis=0, keepdims=True) + jnp.exp2(sink - m)
        vc = vt_ref[0, :, pl.ds(w0, span)]
        o = jnp.dot(vc, e.astype(BF16), preferred_element_type=F32) / denom
        o_ref[rows, :] = _pair_transpose([o[:, g * tq:(g + 1) * tq] for g in range(grp)]).astype(o_ref.dtype)


def _window_attn(q, k, vt, slope_row, sink_row, bsz, seq, n_kv, grp):
    tokens = q.shape[0]
    tq = WIN_Q * WIN_BLOCKS
    assert seq % tq == 0 and seq >= 3 * WIN_Q
    nq = seq // tq
    n = grp * WIN_Q
    kern = functools.partial(_window_attn_kernel, grp=grp, seq=seq)
    return pl.pallas_call(
        kern,
        grid=(bsz, n_kv, nq),
        in_specs=[
            pl.BlockSpec((tq, grp * LANES), lambda b, h, i: (b * nq + i, h)),
            pl.BlockSpec((seq, LANES), lambda b, h, i: (b, h)),
            pl.BlockSpec((1, HEAD_DIM, seq), lambda b, h, i: (b, h, 0)),
            pl.BlockSpec((1, 1, n), lambda b, h, i: (h, 0, 0)),
            pl.BlockSpec((1, 1, n), lambda b, h, i: (h, 0, 0)),
        ],
        out_specs=pl.BlockSpec((tq, grp * HEAD_DIM), lambda b, h, i: (b * nq + i, h)),
        out_shape=jax.ShapeDtypeStruct((tokens, n_kv * grp * HEAD_DIM), BF16),
        compiler_params=_params(("arbitrary", "arbitrary", "arbitrary")),
        name="window_attn",
    )(q, k, vt, slope_row, sink_row)


def _out_proj_kernel(*refs, n_in):
    x_ref = refs[0]
    y_refs = refs[1:1 + n_in]
    w_refs = refs[1 + n_in:1 + 2 * n_in]
    o_ref = refs[1 + 2 * n_in]
    acc = x_ref[...]
    for y_ref, w_ref in zip(y_refs, w_refs):
        acc = acc + jnp.dot(y_ref[...], w_ref[...], preferred_element_type=F32)
    o_ref[...] = acc


def _out_proj(x2, ys, ws):
    tokens = x2.shape[0]
    tt = min(TOK_TILE, tokens)
    row = lambda i: (i, 0)
    const = lambda i: (0, 0)
    kern = functools.partial(_out_proj_kernel, n_in=len(ys))
    return pl.pallas_call(
        kern,
        grid=(tokens // tt,),
        in_specs=[pl.BlockSpec((tt, D_MODEL), row)]
        + [pl.BlockSpec((tt, y.shape[1]), row) for y in ys]
        + [pl.BlockSpec(w.shape, const) for w in ws],
        out_specs=pl.BlockSpec((tt, D_MODEL), row),
        out_shape=jax.ShapeDtypeStruct((tokens, D_MODEL), F32),
        compiler_params=_params(("arbitrary",)),
        name="out_proj",
    )(x2, *ys, *ws)


def _swiglu_partial(xb, wg, wu, wd):
    hg = jnp.dot(xb, wg, preferred_element_type=F32)
    hu = jnp.dot(xb, wu, preferred_element_type=F32)
    h = hg * jax.nn.sigmoid(hg) * hu
    return jnp.dot(h.astype(BF16), wd, preferred_element_type=F32)


def _ffn_kernel(x_ref, g_ref, wg_ref, wu_ref, wd_ref, o_ref, xn_ref, acc_ref):
    f = pl.program_id(1)

    @pl.when(f == 0)
    def _():
        xn_ref[...] = _rms(x_ref[...], g_ref[...]).astype(BF16)
        acc_ref[...] = jnp.zeros_like(acc_ref)

    acc_ref[...] += _swiglu_partial(xn_ref[...], wg_ref[...], wu_ref[...], wd_ref[...])

    @pl.when(f == pl.num_programs(1) - 1)
    def _():
        o_ref[...] = x_ref[...] + acc_ref[...]


def _ffn(x2, g, wg, wu, wd):
    tokens = x2.shape[0]
    ff = wg.shape[1]
    tt = min(FFN_TOK_TILE, tokens)
    fc = FFN_F_TILE
    assert tokens % tt == 0 and ff % fc == 0
    row = lambda i, f: (i, 0)
    return pl.pallas_call(
        _ffn_kernel,
        grid=(tokens // tt, ff // fc),
        in_specs=[
            pl.BlockSpec((tt, D_MODEL), row),
            pl.BlockSpec((1, D_MODEL), lambda i, f: (0, 0)),
            pl.BlockSpec((D_MODEL, fc), lambda i, f: (0, f)),
            pl.BlockSpec((D_MODEL, fc), lambda i, f: (0, f)),
            pl.BlockSpec((fc, D_MODEL), lambda i, f: (f, 0)),
        ],
        out_specs=pl.BlockSpec((tt, D_MODEL), row),
        out_shape=jax.ShapeDtypeStruct((tokens, D_MODEL), F32),
        scratch_shapes=[pltpu.VMEM((tt, D_MODEL), BF16), pltpu.VMEM((tt, D_MODEL), F32)],
        compiler_params=_params(("arbitrary", "arbitrary")),
        name="ffn",
    )(x2, g, wg, wu, wd)


def _router_kernel(x_ref, g_ref, rh_ref, rl_ref, xn_ref, ti_ref, tw_ref):
    tt = x_ref.shape[0]
    xn = _rms(x_ref[...], g_ref[...])
    xn_ref[...] = xn
    xh = xn.astype(BF16)
    xl = (xn - xh.astype(F32)).astype(BF16)
    logits = (jnp.dot(xh, rh_ref[...], preferred_element_type=F32)
              + jnp.dot(xh, rl_ref[...], preferred_element_type=F32)
              + jnp.dot(xl, rh_ref[...], preferred_element_type=F32))
    lane = lax.broadcasted_iota(jnp.int32, (tt, LANES), 1).astype(F32)
    lg = jnp.where(lane < N_EXPERTS, logits, -jnp.inf)
    m1 = jnp.max(lg, axis=1, keepdims=True)
    i1 = jnp.min(jnp.where(lg == m1, lane, float(LANES)), axis=1, keepdims=True)
    lg2 = jnp.where(lane == i1, -jnp.inf, lg)
    m2 = jnp.max(lg2, axis=1, keepdims=True)
    i2 = jnp.min(jnp.where(lg2 == m2, lane, float(LANES)), axis=1, keepdims=True)
    e2 = jnp.exp(m2 - m1)
    w1 = 1.0 / (1.0 + e2)
    w2 = e2 * w1
    ti_ref[...] = jnp.where(lane == 0.0, i1, i2).T[:8].astype(jnp.int32)
    tw_ref[...] = jnp.where(lane == 0.0, w1, w2)


def _router(x2, g, r_hi, r_lo):
    tokens = x2.shape[0]
    tt = min(TOK_TILE, tokens)
    row = lambda i: (i, 0)
    const = lambda i: (0, 0)
    return pl.pallas_call(
        _router_kernel,
        grid=(tokens // tt,),
        in_specs=[
            pl.BlockSpec((tt, D_MODEL), row),
            pl.BlockSpec((1, D_MODEL), const),
            pl.BlockSpec((D_MODEL, LANES), const),
            pl.BlockSpec((D_MODEL, LANES), const),
        ],
        out_specs=[
            pl.BlockSpec((tt, D_MODEL), row),
            pl.BlockSpec((8, tt), lambda i: (0, i)),
            pl.BlockSpec((tt, LANES), row),
        ],
        out_shape=[
            jax.ShapeDtypeStruct((tokens, D_MODEL), F32),
            jax.ShapeDtypeStruct((8, tokens), jnp.int32),
            jax.ShapeDtypeStruct((tokens, LANES), F32),
        ],
        compiler_params=_params(("arbitrary",)),
        name="moe_router",
    )(x2, g, r_hi, r_lo)


def _for_rows(n_rows, fn):
    def group(u, carry):
        for v in range(DMA_UNROLL):
            fn(u * DMA_UNROLL + v)
        return carry
    lax.fori_loop(0, n_rows // DMA_UNROLL, group, 0)


def _expert_ffn_kernel(be_ref, nu_ref, dst_ref, xn_hbm, wg_ref, wu_ref, wd_ref, y_hbm,
                       xg_ref, xb_ref, acc_ref, yb_ref, gsem, ssem, *, tokens):
    b = pl.program_id(0)
    f = pl.program_id(1)
    nf = pl.num_programs(1)
    nu = nu_ref[0]
    used = b < nu
    last = f == nf - 1
    slot = b % 2
    m = MOE_ROW_TILE
    per_step = m // EXPERT_F_STEPS
    nxt = jnp.minimum(b + 1, nu - 1)

    def gather(blk, r, sl):
        tok = lax.rem(dst_ref[blk * m + r], tokens)
        return pltpu.make_async_copy(xn_hbm.at[pl.ds(tok, 1)], xg_ref.at[sl, pl.ds(r, 1)], gsem.at[sl])

    def scatter(blk, r):
        return pltpu.make_async_copy(yb_ref.at[pl.ds(r, 1)], y_hbm.at[pl.ds(dst_ref[blk * m + r], 1)], ssem)

    @pl.when(jnp.logical_and(used, jnp.logical_and(f == 0, b == 0)))
    def _():
        _for_rows(m, lambda r: gather(0, r, 0).start())

    @pl.when(jnp.logical_and(used, f == 0))
    def _():
        _for_rows(m, lambda r: gather(b, r, slot).wait())
        xb_ref[...] = xg_ref[slot].astype(BF16)
        acc_ref[...] = jnp.zeros_like(acc_ref)

    def step(with_scatter):
        for v in range(per_step):
            r = f * per_step + v
            gather(nxt, r, 1 - slot).start()
            if with_scatter:
                scatter(b - 1, r).start()
        acc_ref[...] += _swiglu_partial(xb_ref[...], wg_ref[0], wu_ref[0], wd_ref[0])

    @pl.when(jnp.logical_and(used, b == 0))
    def _():
        step(False)

    @pl.when(jnp.logical_and(used, b > 0))
    def _():
        step(True)

    @pl.when(jnp.logical_and(used, jnp.logical_and(last, b > 0)))
    def _():
        _for_rows(m, lambda r: scatter(b - 1, r).wait())

    @pl.when(jnp.logical_and(used, last))
    def _():
        yb_ref[...] = acc_ref[...]

    @pl.when(jnp.logical_and(last, b == nu - 1))
    def _():
        _for_rows(m, lambda r: scatter(b, r).start())
        _for_rows(m, lambda r: gather(nxt, r, 1 - slot).wait())
        _for_rows(m, lambda r: scatter(b, r).wait())


def _expert_ffn(xn, dst, blk_expert, n_used, wg, wu, wd):
    tokens = xn.shape[0]
    rows = dst.shape[0]
    ff = wg.shape[2]
    m = MOE_ROW_TILE
    assert rows % m == 0 and ff % EXPERT_F_STEPS == 0 and m % EXPERT_F_STEPS == 0
    fc = ff // EXPERT_F_STEPS
    assert fc % LANES == 0
    return pl.pallas_call(
        functools.partial(_expert_ffn_kernel, tokens=tokens),
        grid_spec=pltpu.PrefetchScalarGridSpec(
            num_scalar_prefetch=3,
            grid=(rows // m, EXPERT_F_STEPS),
            in_specs=[
                pl.BlockSpec(memory_space=pl.ANY),
                pl.BlockSpec((1, D_MODEL, fc), lambda b, f, be, nu, ds: (be[b], 0, f)),
                pl.BlockSpec((1, D_MODEL, fc), lambda b, f, be, nu, ds: (be[b], 0, f)),
                pl.BlockSpec((1, fc, D_MODEL), lambda b, f, be, nu, ds: (be[b], f, 0)),
            ],
            out_specs=pl.BlockSpec(memory_space=pl.ANY),
            scratch_shapes=[
                pltpu.VMEM((2, m, D_MODEL), F32),
                pltpu.VMEM((m, D_MODEL), BF16),
                pltpu.VMEM((m, D_MODEL), F32),
                pltpu.VMEM((m, D_MODEL), F32),
                pltpu.SemaphoreType.DMA((2,)),
                pltpu.SemaphoreType.DMA,
            ],
        ),
        out_shape=jax.ShapeDtypeStruct((TOP_K * tokens + rows, D_MODEL), F32),
        compiler_params=_params(("arbitrary", "arbitrary")),
        name="expert_ffn",
    )(blk_expert, n_used, dst, xn, wg, wu, wd)


def _moe_combine_kernel(x_ref, y0_ref, y1_ref, tw_ref, gf_ref, o_ref, *, final_norm):
    tw = tw_ref[...]
    y = x_ref[...] + tw[:, 0:1] * y0_ref[...] + tw[:, 1:2] * y1_ref[...]
    if final_norm:
        y = _rms(y, gf_ref[...])
    o_ref[...] = y


def _moe_combine(x2, y, top_w, g_final, final_norm):
    tokens = x2.shape[0]
    tt = min(TOK_TILE, tokens)
    nt = tokens // tt
    row = lambda i: (i, 0)
    return pl.pallas_call(
        functools.partial(_moe_combine_kernel, final_norm=final_norm),
        grid=(nt,),
        in_specs=[
            pl.BlockSpec((tt, D_MODEL), row),
            pl.BlockSpec((tt, D_MODEL), row),
            pl.BlockSpec((tt, D_MODEL), lambda i: (nt + i, 0)),
            pl.BlockSpec((tt, LANES), row),
            pl.BlockSpec((1, D_MODEL), lambda i: (0, 0)),
        ],
        out_specs=pl.BlockSpec((tt, D_MODEL), row),
        out_shape=jax.ShapeDtypeStruct((tokens, D_MODEL), F32),
        compiler_params=_params(("arbitrary",)),
        name="moe_combine",
    )(x2, y, y, top_w, g_final)


def _route(top_i):
    m = MOE_ROW_TILE
    tokens = top_i.shape[1]
    n_assign = tokens * TOP_K
    rows = -(-(n_assign + N_EXPERTS * m) // m) * m
    e_flat = top_i.reshape(-1)
    counts = jnp.sum((e_flat[:, None] == jnp.arange(N_EXPERTS, dtype=jnp.int32)[None, :]).astype(jnp.int32), axis=0)
    padded = (counts + m - 1) // m * m
    ends = jnp.cumsum(padded)
    starts = ends - padded
    ustarts = jnp.cumsum(counts) - counts
    order = jnp.argsort(e_flat, stable=True).astype(jnp.int32)
    slot = jnp.arange(rows, dtype=jnp.int32)
    e_slot = jnp.minimum(jnp.sum((slot[:, None] >= ends[None, :]).astype(jnp.int32), axis=1), N_EXPERTS - 1)
    j = slot - starts[e_slot]
    valid = jnp.logical_and(j < counts[e_slot], slot < ends[-1])
    a = order[jnp.clip(ustarts[e_slot] + j, 0, n_assign - 1)]
    dst = jnp.where(valid, a, n_assign + slot).astype(jnp.int32)
    n_used = (ends[-1] // m).astype(jnp.int32)[None]
    return dst, e_slot[::m], n_used


def _moe(x2, g, r_hi, r_lo, wg, wu, wd, layer, g_final, final_norm):
    xn, top_i, top_w = _router(x2, g, r_hi, r_lo)
    dst, blk_expert, n_used = _route(top_i[:TOP_K])
    y = _expert_ffn(xn, dst, blk_expert + layer * N_EXPERTS, n_used, wg, wu, wd)
    return _moe_combine(x2, y, top_w, g_final, final_norm)


def _pad_heads(w, n_heads, width, offset=0):
    r = w.shape[0]
    w = w.reshape(r, n_heads, width)
    w = jnp.pad(w, ((0, 0), (0, 0), (offset, LANES - width - offset)))
    return w.reshape(r, n_heads * LANES)


def _rope_tables(ang, base):
    half = ang.shape[1]
    cos = jnp.cos(ang)
    sin = jnp.sin(ang)
    pad = ((0, 0), (base, LANES - base - 2 * half))
    cos_t = jnp.pad(jnp.concatenate([cos, cos], axis=1) - 1.0, pad) + 1.0
    sin_t = jnp.pad(jnp.concatenate([-sin, sin], axis=1), pad)
    return cos_t, sin_t


def _rope_angles(pos, dim):
    inv = ROPE_THETA ** (-jnp.arange(0, dim, 2, dtype=F32) / dim)
    return pos.astype(F32)[:, None] * inv[None, :]


def _lane_gain(g):
    return jnp.pad(g.astype(F32), (0, LANES - g.shape[0]))[None, :]


def _trunk(x, p):
    bsz, seq, _ = x.shape
    x2 = x.reshape(bsz * seq, D_MODEL)
    n_rows = seq // GRID_W
    row = jnp.repeat(jnp.arange(n_rows), GRID_W)
    col = jnp.tile(jnp.arange(GRID_W), n_rows)
    ang_axial = jnp.concatenate([_rope_angles(row, HEAD_DIM // 2), _rope_angles(col, HEAD_DIM // 2)], axis=-1)
    cos_ax, sin_ax = _rope_tables(ang_axial, 0)
    cos_1d, sin_1d = _rope_tables(_rope_angles(jnp.arange(seq), MLA_ROPE_DIM), _MLA_ROPE_BASE)
    depth = p["ev_norm_mix"].shape[0] + p["od_norm_mix"].shape[0]
    assert depth % 2 == 0
    a_grp = A_HEADS // A_KV_HEADS
    for layer in range(depth):
        i = layer // 2
        last_layer = layer == depth - 1
        if layer % 2 == 0:
            aq, ak, avt, bq, bk, bvt = _even_in(
                x2, bsz, seq, p["ev_norm_mix"][i][None, :], p["ev_w_in"][i], cos_ax, sin_ax,
                p["b_q_norm"][i], p["b_k_norm"][i])
            ya = _window_attn(aq, ak, avt, p["a_slope"], p["a_sink"][i], bsz, seq, A_KV_HEADS, a_grp)
            yb = _dense_attn(bq, bk, bvt, bsz, seq, B_KV_HEADS, B_HEADS // B_KV_HEADS, 1, 256)
            x2 = _out_proj(x2, [ya, yb], [p["ev_w_out_a"][i], p["ev_w_out_b"][i]])
            x2 = _ffn(x2, p["ev_norm_ffn"][i][None, :], p["ffn_w_gate"][i], p["ffn_w_up"][i], p["ffn_w_down"][i])
        else:
            q, k, vt = _odd_in(
                x2, bsz, seq, p["od_norm_mix"][i][None, :], p["od_w_in"][i], p["mla_q_norm"][i][None, :],
                p["mla_w_q_up"][i], p["mla_kv_norm"][i][None, :], p["mla_w_k_up"][i], p["mla_w_v_up"][i],
                cos_1d, sin_1d)
            yc = _dense_attn(q, k, vt, bsz, seq, MLA_HEADS, 1, 2, 1024)
            x2 = _out_proj(x2, [yc], [p["od_w_out"][i]])
            x2 = _moe(x2, p["od_norm_ffn"][i][None, :], p["router_hi"][i], p["router_lo"][i],
                      p["moe_w_gate"], p["moe_w_up"], p["moe_w_down"], i,
                      p["final_norm"], last_layer)
    return x2.reshape(bsz, seq, D_MODEL)


def _prepare(ev_norm_mix, ev_w_in, a_sink, b_q_norm, b_k_norm, ev_w_out, ev_norm_ffn,
             ffn_w_gate, ffn_w_up, ffn_w_down, od_norm_mix, od_w_in, mla_q_norm, mla_w_q_up,
             mla_kv_norm, mla_w_kv_up, od_w_out, od_norm_ffn, moe_router, moe_w_gate, moe_w_up,
             moe_w_down, final_norm):
    hd = HEAD_DIM
    a_grp = A_HEADS // A_KV_HEADS

    def even_w_in(w):
        sizes = [A_HEADS * hd, A_KV_HEADS * hd, A_KV_HEADS * hd, B_HEADS * hd, B_KV_HEADS * hd, B_KV_HEADS * hd]
        aq, ak, av, bq, bk, bv = jnp.split(w, list(np.cumsum(sizes)[:-1]), axis=-1)
        return jnp.concatenate([_pad_heads(aq, A_HEADS, hd), _pad_heads(ak, A_KV_HEADS, hd),
                                _pad_heads(bq, B_HEADS, hd), _pad_heads(bk, B_KV_HEADS, hd), av, bv],
                               axis=-1).astype(BF16)

    def odd_w_in(w):
        c = w[:, :MLA_Q_RANK + MLA_KV_RANK]
        kr = _pad_heads(w[:, MLA_Q_RANK + MLA_KV_RANK:], 1, MLA_ROPE_DIM, _MLA_ROPE_BASE)
        return jnp.concatenate([c, kr], axis=-1).astype(BF16)

    def q_up(w):
        return _pad_heads(w, MLA_HEADS, MLA_QK_DIM).astype(BF16)

    def kv_up(w):
        w = w.reshape(MLA_KV_RANK, MLA_HEADS, MLA_NOPE_DIM + MLA_V_DIM)
        wk = _pad_heads(w[:, :, :MLA_NOPE_DIM].reshape(MLA_KV_RANK, -1), MLA_HEADS, MLA_NOPE_DIM)
        wv = w[:, :, MLA_NOPE_DIM:].reshape(MLA_KV_RANK, -1)
        return wk.astype(BF16), wv.astype(BF16)

    slopes = jnp.asarray(2.0 ** (-8.0 * np.arange(1, A_HEADS + 1) / A_HEADS), dtype=F32)
    per_col = lambda v: jnp.repeat(v.astype(F32).reshape(A_KV_HEADS, a_grp), WIN_Q, axis=1)[:, None, :] * LOG2E
    router = jnp.pad(moe_router.astype(F32), ((0, 0), (0, 0), (0, LANES - N_EXPERTS)))
    router_hi = router.astype(BF16)
    kv = [kv_up(w) for w in mla_w_kv_up]
    return {
        "ev_norm_mix": ev_norm_mix, "ev_w_in": jnp.stack([even_w_in(w) for w in ev_w_in]),
        "a_slope": per_col(slopes), "a_sink": jnp.stack([per_col(s) for s in a_sink]),
        "b_q_norm": jnp.stack([_lane_gain(g) for g in b_q_norm]),
        "b_k_norm": jnp.stack([_lane_gain(g) for g in b_k_norm]),
        "ev_w_out_a": ev_w_out[:, :A_HEADS * hd].astype(BF16),
        "ev_w_out_b": ev_w_out[:, A_HEADS * hd:].astype(BF16),
        "ev_norm_ffn": ev_norm_ffn,
        "ffn_w_gate": ffn_w_gate.astype(BF16), "ffn_w_up": ffn_w_up.astype(BF16),
        "ffn_w_down": ffn_w_down.astype(BF16),
        "od_norm_mix": od_norm_mix, "od_w_in": jnp.stack([odd_w_in(w) for w in od_w_in]),
        "mla_q_norm": mla_q_norm, "mla_w_q_up": jnp.stack([q_up(w) for w in mla_w_q_up]),
        "mla_kv_norm": mla_kv_norm,
        "mla_w_k_up": jnp.stack([a for a, _ in kv]), "mla_w_v_up": jnp.stack([b for _, b in kv]),
        "od_w_out": od_w_out.astype(BF16), "od_norm_ffn": od_norm_ffn,
        "router_hi": router_hi, "router_lo": (router - router_hi.astype(F32)).astype(BF16),
        "moe_w_gate": moe_w_gate.astype(BF16).reshape((-1,) + moe_w_gate.shape[2:]),
        "moe_w_up": moe_w_up.astype(BF16).reshape((-1,) + moe_w_up.shape[2:]),
        "moe_w_down": moe_w_down.astype(BF16).reshape((-1,) + moe_w_down.shape[2:]),
        "final_norm": final_norm[None, :],
    }


def kernel(x_prompt, x_sample, ev_norm_mix, ev_w_in, a_sink, b_q_norm, b_k_norm, ev_w_out, ev_norm_ffn, ffn_w_gate, ffn_w_up, ffn_w_down, od_norm_mix, od_w_in, mla_q_norm, mla_w_q_up, mla_kv_norm, mla_w_kv_up, od_w_out, od_norm_ffn, moe_router, moe_w_gate, moe_w_up, moe_w_down, final_norm):
    p = _prepare(ev_norm_mix, ev_w_in, a_sink, b_q_norm, b_k_norm, ev_w_out, ev_norm_ffn,
                 ffn_w_gate, ffn_w_up, ffn_w_down, od_norm_mix, od_w_in, mla_q_norm, mla_w_q_up,
                 mla_kv_norm, mla_w_kv_up, od_w_out, od_norm_ffn, moe_router, moe_w_gate, moe_w_up,
                 moe_w_down, final_norm)
    return (_trunk(x_prompt, p), _trunk(x_sample, p))
```

```python
import functools
import math

import numpy as np
import jax
import jax.numpy as jnp
from jax import lax
from jax.experimental import pallas as pl
from jax.experimental.pallas import tpu as pltpu

F32 = jnp.float32
BF16 = jnp.bfloat16

D_MODEL = 1024
HEAD_DIM = 64
WINDOW = 128
GRID_W = 64
ROPE_THETA = 10000.0
NORM_EPS = 1e-6
A_HEADS, A_KV_HEADS = 8, 2
B_HEADS, B_KV_HEADS = 8, 2
MLA_HEADS = 8
MLA_Q_RANK, MLA_KV_RANK = 256, 128
MLA_NOPE_DIM, MLA_ROPE_DIM, MLA_V_DIM = 64, 32, 64
MLA_QK_DIM = MLA_NOPE_DIM + MLA_ROPE_DIM
N_EXPERTS = 8
TOP_K = 2
LANES = 128
LOG2E = math.log2(math.e)
VMEM_LIMIT = 48 * 1024 * 1024

TOK_TILE = 512
FFN_TOK_TILE = 1024
FFN_F_TILE = 512
KEY_CHUNK = 512
WIN_Q = 128
WIN_BLOCKS = 4
MOE_ROW_TILE = 896
EXPERT_F_STEPS = 7
DMA_UNROLL = 8


def _rms(x, g):
    return x * lax.rsqrt(jnp.mean(x * x, axis=-1, keepdims=True) + NORM_EPS) * g


def _swap_halves(x, half):
    lane = lax.broadcasted_iota(jnp.int32, x.shape, 1)
    return jnp.where(lane < half, pltpu.roll(x, LANES - half, 1), pltpu.roll(x, half, 1))


def _swap_halves_at(x, base, half):
    lane = lax.broadcasted_iota(jnp.int32, x.shape, 1)
    return jnp.where(lane < base + half, pltpu.roll(x, LANES - half, 1), pltpu.roll(x, half, 1))


def _params(sem):
    return pltpu.CompilerParams(dimension_semantics=sem, vmem_limit_bytes=VMEM_LIMIT)


_EV_AQ, _EV_AK, _EV_BQ, _EV_BK, _EV_V, _EV_END = 0, 1024, 1280, 2304, 2560, 2816


def _even_in_kernel(x_ref, g_ref, w_ref, cos_ref, sin_ref, qg_ref, kg_ref,
                    aq_ref, ak_ref, avt_ref, bq_ref, bk_ref, bvt_ref):
    xb = _rms(x_ref[...], g_ref[...]).astype(BF16)
    cos = cos_ref[...]
    sin = sin_ref[...]
    qscale = (HEAD_DIM ** -0.5) * LOG2E

    def norm_rope(blk, gain):
        ms = jnp.sum(blk * blk, axis=-1, keepdims=True) * (1.0 / HEAD_DIM)
        y = blk * lax.rsqrt(ms + NORM_EPS) * gain
        return y * cos + _swap_halves(y, HEAD_DIM // 2) * sin

    pa = jnp.dot(xb, w_ref[:, _EV_AQ:_EV_AK], preferred_element_type=F32)
    aq_ref[...] = (pa * qscale).astype(BF16)
    ak_ref[...] = jnp.dot(xb, w_ref[:, _EV_AK:_EV_BQ], preferred_element_type=F32).astype(BF16)
    pbq = jnp.dot(xb, w_ref[:, _EV_BQ:_EV_BK], preferred_element_type=F32)
    for h in range(B_HEADS):
        sl = slice(h * LANES, (h + 1) * LANES)
        bq_ref[:, sl] = (norm_rope(pbq[:, sl], qg_ref[...]) * qscale).astype(BF16)
    pbk = jnp.dot(xb, w_ref[:, _EV_BK:_EV_V], preferred_element_type=F32)
    lane = lax.broadcasted_iota(jnp.int32, (pbk.shape[0], LANES), 1)
    for h in range(B_KV_HEADS):
        sl = slice(h * LANES, (h + 1) * LANES)
        bk_ref[:, sl] = jnp.where(lane == REF_LANE, 1.0, norm_rope(pbk[:, sl], kg_ref[...])).astype(BF16)
    pv = jnp.dot(xb, w_ref[:, _EV_V:_EV_END], preferred_element_type=F32)
    avt_ref[0] = pv[:, :LANES].T.astype(BF16)
    bvt_ref[0] = pv[:, LANES:].T.astype(BF16)


def _even_in(x2, bsz, seq, g, w_exp, cos_t, sin_t, qg, kg):
    tokens = x2.shape[0]
    tt = min(TOK_TILE, seq)
    assert seq % tt == 0
    ns = seq // tt
    row = lambda i: (i, 0)
    const = lambda i: (0, 0)
    tab = lambda i: (i % ns, 0)
    vt = lambda i: (i // ns, 0, i % ns)
    return pl.pallas_call(
        _even_in_kernel,
        grid=(tokens // tt,),
        in_specs=[
            pl.BlockSpec((tt, D_MODEL), row),
            pl.BlockSpec((1, D_MODEL), const),
            pl.BlockSpec((D_MODEL, _EV_END), const),
            pl.BlockSpec((tt, LANES), tab),
            pl.BlockSpec((tt, LANES), tab),
            pl.BlockSpec((1, LANES), const),
            pl.BlockSpec((1, LANES), const),
        ],
        out_specs=[
            pl.BlockSpec((tt, A_HEADS * LANES), row),
            pl.BlockSpec((tt, A_KV_HEADS * LANES), row),
            pl.BlockSpec((1, A_KV_HEADS * HEAD_DIM, tt), vt),
            pl.BlockSpec((tt, B_HEADS * LANES), row),
            pl.BlockSpec((tt, B_KV_HEADS * LANES), row),
            pl.BlockSpec((1, B_KV_HEADS * HEAD_DIM, tt), vt),
        ],
        out_shape=[
            jax.ShapeDtypeStruct((tokens, A_HEADS * LANES), BF16),
            jax.ShapeDtypeStruct((tokens, A_KV_HEADS * LANES), BF16),
            jax.ShapeDtypeStruct((bsz, A_KV_HEADS * HEAD_DIM, seq), BF16),
            jax.ShapeDtypeStruct((tokens, B_HEADS * LANES), BF16),
            jax.ShapeDtypeStruct((tokens, B_KV_HEADS * LANES), BF16),
            jax.ShapeDtypeStruct((bsz, B_KV_HEADS * HEAD_DIM, seq), BF16),
        ],
        compiler_params=_params(("arbitrary",)),
        name="even_in_proj",
    )(x2, g, w_exp, cos_t, sin_t, qg, kg)


_MLA_ROPE_BASE = MLA_NOPE_DIM


def _odd_in_kernel(x_ref, g_ref, w_ref, qn_ref, wq_ref, kvn_ref, wk_ref, wv_ref,
                   cos_ref, sin_ref, q_ref, k_ref, vt_ref):
    xb = _rms(x_ref[...], g_ref[...]).astype(BF16)
    cos = cos_ref[...]
    sin = sin_ref[...]
    qscale = (MLA_QK_DIM ** -0.5) * LOG2E

    def rope(blk):
        return blk * cos + _swap_halves_at(blk, _MLA_ROPE_BASE, MLA_ROPE_DIM // 2) * sin

    proj = jnp.dot(xb, w_ref[...], preferred_element_type=F32)
    cq = _rms(proj[:, :MLA_Q_RANK], qn_ref[...]).astype(BF16)
    ckv = _rms(proj[:, MLA_Q_RANK:MLA_Q_RANK + MLA_KV_RANK], kvn_ref[...]).astype(BF16)
    kr = rope(proj[:, MLA_Q_RANK + MLA_KV_RANK:])
    q = jnp.dot(cq, wq_ref[...], preferred_element_type=F32)
    kn = jnp.dot(ckv, wk_ref[...], preferred_element_type=F32)
    lane = lax.broadcasted_iota(jnp.int32, (kr.shape[0], LANES), 1)
    for h in range(MLA_HEADS):
        sl = slice(h * LANES, (h + 1) * LANES)
        q_ref[:, sl] = (rope(q[:, sl]) * qscale).astype(BF16)
        k_ref[:, sl] = jnp.where(lane == REF_LANE, 1.0, kn[:, sl] + kr).astype(BF16)
    v = jnp.dot(ckv, wv_ref[...], preferred_element_type=F32)
    for c in range(MLA_HEADS * MLA_V_DIM // LANES):
        vt_ref[0, c * LANES:(c + 1) * LANES, :] = v[:, c * LANES:(c + 1) * LANES].T.astype(BF16)


def _odd_in(x2, bsz, seq, g, w_exp, qn, wq_exp, kvn, wk_exp, wv, cos_t, sin_t):
    tokens = x2.shape[0]
    tt = min(TOK_TILE, seq)
    assert seq % tt == 0
    ns = seq // tt
    row = lambda i: (i, 0)
    const = lambda i: (0, 0)
    tab = lambda i: (i % ns, 0)
    vt = lambda i: (i // ns, 0, i % ns)
    hl = MLA_HEADS * LANES
    hv = MLA_HEADS * MLA_V_DIM
    return pl.pallas_call(
        _odd_in_kernel,
        grid=(tokens // tt,),
        in_specs=[
            pl.BlockSpec((tt, D_MODEL), row),
            pl.BlockSpec((1, D_MODEL), const),
            pl.BlockSpec((D_MODEL, 4 * LANES), const),
            pl.BlockSpec((1, MLA_Q_RANK), const),
            pl.BlockSpec((MLA_Q_RANK, hl), const),
            pl.BlockSpec((1, MLA_KV_RANK), const),
            pl.BlockSpec((MLA_KV_RANK, hl), const),
            pl.BlockSpec((MLA_KV_RANK, hv), const),
            pl.BlockSpec((tt, LANES), tab),
            pl.BlockSpec((tt, LANES), tab),
        ],
        out_specs=[
            pl.BlockSpec((tt, hl), row),
            pl.BlockSpec((tt, hl), row),
            pl.BlockSpec((1, hv, tt), vt),
        ],
        out_shape=[
            jax.ShapeDtypeStruct((tokens, hl), BF16),
            jax.ShapeDtypeStruct((tokens, hl), BF16),
            jax.ShapeDtypeStruct((bsz, hv, seq), BF16),
        ],
        compiler_params=_params(("arbitrary",)),
        name="odd_in_proj",
    )(x2, g, w_exp, qn, wq_exp, kvn, wk_exp, wv, cos_t, sin_t)


def _pair_transpose(pieces):
    outs = []
    for a in range(0, len(pieces), 2):
        outs.append(jnp.concatenate([pieces[a], pieces[a + 1]], axis=0).T)
    return outs[0] if len(outs) == 1 else jnp.concatenate(outs, axis=1)


SUM_ROWS = 16
REF_LANE = LANES - 1
MIN_CHUNK_UNROLL = 4
MAX_CHUNK_UNROLL = 8
SEED_KEYS = 128
JUMP_LIMIT = 64.0


def _bf16_round(x):
    return x.astype(BF16).astype(F32)


def _dense_attn_kernel(q_ref, k_ref, vt_ref, o_ref, qt_ref, acc_ref, ob_ref, *, kb, grp, tq, tk, seq, unroll):
    n = grp * tq
    dv = HEAD_DIM
    n_chunks = seq // tk
    ones = jnp.ones((SUM_ROWS, tk), BF16)
    row16 = lax.broadcasted_iota(jnp.int32, (16, n), 0)
    pieces = []
    for j in range(kb):
        qs = jnp.concatenate(
            [q_ref[:, (j * grp + g) * LANES:(j * grp + g + 1) * LANES] for g in range(grp)], axis=0)
        qt = qs.astype(F32).T.astype(BF16)
        qt_ref[0] = qt
        qt_ref[1] = qt

        def keys(c, j=j):
            off = pl.multiple_of(c * tk, tk)
            return k_ref[pl.ds(off, tk), j * LANES:(j + 1) * LANES]

        def values(c, j=j):
            off = pl.multiple_of(c * tk, tk)
            return jnp.concatenate([vt_ref[0, j * dv:(j + 1) * dv, pl.ds(off, tk)], ones], axis=0)

        def set_reference(buf, ref):
            qt_ref[buf, LANES - 16:, :] = jnp.where(row16 == 15, -ref, 0.0).astype(BF16)

        def chunk(c, buf, ref, ref_acc, run, jump_max):
            set_reference(buf, ref)
            t = jnp.dot(keys(c), qt_ref[buf], preferred_element_type=F32)
            jump = jnp.max(t, axis=0, keepdims=True)
            p = jnp.exp2(t.astype(BF16))
            pv = jnp.dot(values(c), p, preferred_element_type=F32)
            acc_ref[...] = jnp.exp2(ref_acc - ref) * acc_ref[...] + pv
            return jnp.maximum(run, ref + jump), jnp.maximum(jump_max, jump)

        first = k_ref[0:min(SEED_KEYS, tk), j * LANES:(j + 1) * LANES]
        run0 = jnp.max(jnp.dot(first, qt, preferred_element_type=F32), axis=0, keepdims=True)
        ref0 = _bf16_round(run0)
        acc_ref[...] = jnp.zeros((dv + SUM_ROWS, n), F32)

        def group(ci, carry, chunk=chunk):
            ref_a, ref_b, ref_acc, run, jump_max = carry
            for u in range(unroll):
                run, jump_max = chunk(ci * unroll + u, u % 2, ref_a, ref_acc, run, jump_max)
                ref_a, ref_b, ref_acc = ref_b, _bf16_round(run), ref_a
            return ref_a, ref_b, ref_acc, run, jump_max

        init = (ref0, ref0, ref0, run0, jnp.zeros((1, n), F32))
        jump_max = lax.fori_loop(0, n_chunks // unroll, group, init)[4]
        acc = acc_ref[...]
        ob_ref[j] = acc[:dv] / acc[dv:dv + 1]

        @pl.when(jnp.max(jump_max) > JUMP_LIMIT)
        def _(j=j, qt=qt, keys=keys, values=values):
            acc_ref[...] = jnp.zeros((dv + SUM_ROWS, n), F32)

            def exact(c, m_old):
                s = jnp.dot(keys(c), qt, preferred_element_type=F32)
                m_new = jnp.maximum(m_old, jnp.max(s, axis=0, keepdims=True))
                p = jnp.exp2((s - m_new).astype(BF16))
                pv = jnp.dot(values(c), p, preferred_element_type=F32)
                acc_ref[...] = jnp.exp2(m_old - m_new) * acc_ref[...] + pv
                return m_new

            lax.fori_loop(0, n_chunks, exact, jnp.full((1, n), -jnp.inf, F32))
            acc = acc_ref[...]
            ob_ref[j] = acc[:dv] / acc[dv:dv + 1]

        o = ob_ref[j]
        for g in range(grp):
            pieces.append(o[:, g * tq:(g + 1) * tq])
    o_ref[...] = _pair_transpose(pieces).astype(o_ref.dtype)


def _dense_attn(q, k, vt, bsz, seq, n_kv, grp, kb, tq):
    tokens = q.shape[0]
    tq = min(tq, seq)
    tk = min(KEY_CHUNK, seq // MIN_CHUNK_UNROLL)
    n_chunks = seq // tk
    unroll = MAX_CHUNK_UNROLL if n_chunks % MAX_CHUNK_UNROLL == 0 and n_chunks > MAX_CHUNK_UNROLL else MIN_CHUNK_UNROLL
    assert seq % tq == 0 and seq % tk == 0 and n_chunks % unroll == 0
    assert n_kv % kb == 0 and (kb * grp) % 2 == 0
    nq = seq // tq
    n = grp * tq
    kern = functools.partial(_dense_attn_kernel, kb=kb, grp=grp, tq=tq, tk=tk, seq=seq, unroll=unroll)
    return pl.pallas_call(
        kern,
        grid=(bsz, n_kv // kb, nq),
        in_specs=[
            pl.BlockSpec((tq, kb * grp * LANES), lambda b, h, i: (b * nq + i, h)),
            pl.BlockSpec((seq, kb * LANES), lambda b, h, i: (b, h)),
            pl.BlockSpec((1, kb * HEAD_DIM, seq), lambda b, h, i: (b, h, 0)),
        ],
        out_specs=pl.BlockSpec((tq, kb * grp * HEAD_DIM), lambda b, h, i: (b * nq + i, h)),
        out_shape=jax.ShapeDtypeStruct((tokens, n_kv * grp * HEAD_DIM), BF16),
        scratch_shapes=[
            pltpu.VMEM((2, LANES, n), BF16),
            pltpu.VMEM((HEAD_DIM + SUM_ROWS, n), F32),
            pltpu.VMEM((kb, HEAD_DIM, n), F32),
        ],
        compiler_params=_params(("arbitrary", "arbitrary", "arbitrary")),
        name="dense_attn",
    )(q, k, vt)


def _window_attn_kernel(q_ref, k_ref, vt_ref, slope_ref, sink_ref, o_ref, *, grp, seq):
    tq = WIN_Q
    span = 3 * WIN_Q
    n = grp * tq
    for u in range(WIN_BLOCKS):
        i = pl.program_id(2) * WIN_BLOCKS + u
        w0 = pl.multiple_of(jnp.clip((i - 1) * tq, 0, seq - span), tq)
        rows = slice(u * tq, (u + 1) * tq)
        qs = jnp.concatenate([q_ref[rows, g * LANES:(g + 1) * LANES] for g in range(grp)], axis=0)
        kc = k_ref[pl.ds(w0, span), :]
        s = lax.dot_general(kc, qs, (((1,), (1,)), ((), ())), preferred_element_type=F32)
        kpos = w0 + lax.broadcasted_iota(jnp.int32, (span, n), 0)
        qpos = i * tq + lax.broadcasted_iota(jnp.int32, (span, n), 1) % tq
        dist = jnp.abs(kpos - qpos)
        s = s - slope_ref[0] * dist.astype(F32)
        s = jnp.where(dist <= WINDOW, s, -jnp.inf)
        sink = sink_ref[0]
        m = jnp.maximum(jnp.max(s, axis=0, keepdims=True), sink)
        e = jnp.exp2(s - m)
        denom = jnp.sum(e, axis=0, keepdims=True) + jnp.exp2(sink - m)
        vc = vt_ref[0, :, pl.ds(w0, span)]
        o = jnp.dot(vc, e.astype(BF16), preferred_element_type=F32) / denom
        o_ref[rows, :] = _pair_transpose([o[:, g * tq:(g + 1) * tq] for g in range(grp)]).astype(o_ref.dtype)


def _window_attn(q, k, vt, slope_row, sink_row, bsz, seq, n_kv, grp):
    tokens = q.shape[0]
    tq = WIN_Q * WIN_BLOCKS
    assert seq % tq == 0 and seq >= 3 * WIN_Q
    nq = seq // tq
    n = grp * WIN_Q
    kern = functools.partial(_window_attn_kernel, grp=grp, seq=seq)
    return pl.pallas_call(
        kern,
        grid=(bsz, n_kv, nq),
        in_specs=[
            pl.BlockSpec((tq, grp * LANES), lambda b, h, i: (b * nq + i, h)),
            pl.BlockSpec((seq, LANES), lambda b, h, i: (b, h)),
            pl.BlockSpec((1, HEAD_DIM, seq), lambda b, h, i: (b, h, 0)),
            pl.BlockSpec((1, 1, n), lambda b, h, i: (h, 0, 0)),
            pl.BlockSpec((1, 1, n), lambda b, h, i: (h, 0, 0)),
        ],
        out_specs=pl.BlockSpec((tq, grp * HEAD_DIM), lambda b, h, i: (b * nq + i, h)),
        out_shape=jax.ShapeDtypeStruct((tokens, n_kv * grp * HEAD_DIM), BF16),
        compiler_params=_params(("arbitrary", "arbitrary", "arbitrary")),
        name="window_attn",
    )(q, k, vt, slope_row, sink_row)


def _out_proj_kernel(*refs, n_in):
    x_ref = refs[0]
    y_refs = refs[1:1 + n_in]
    w_refs = refs[1 + n_in:1 + 2 * n_in]
    o_ref = refs[1 + 2 * n_in]
    acc = x_ref[...]
    for y_ref, w_ref in zip(y_refs, w_refs):
        acc = acc + jnp.dot(y_ref[...], w_ref[...], preferred_element_type=F32)
    o_ref[...] = acc


def _out_proj(x2, ys, ws):
    tokens = x2.shape[0]
    tt = min(TOK_TILE, tokens)
    row = lambda i: (i, 0)
    const = lambda i: (0, 0)
    kern = functools.partial(_out_proj_kernel, n_in=len(ys))
    return pl.pallas_call(
        kern,
        grid=(tokens // tt,),
        in_specs=[pl.BlockSpec((tt, D_MODEL), row)]
        + [pl.BlockSpec((tt, y.shape[1]), row) for y in ys]
        + [pl.BlockSpec(w.shape, const) for w in ws],
        out_specs=pl.BlockSpec((tt, D_MODEL), row),
        out_shape=jax.ShapeDtypeStruct((tokens, D_MODEL), F32),
        compiler_params=_params(("arbitrary",)),
        name="out_proj",
    )(x2, *ys, *ws)


def _swiglu_partial(xb, wg, wu, wd):
    hg = jnp.dot(xb, wg, preferred_element_type=F32)
    hu = jnp.dot(xb, wu, preferred_element_type=F32)
    h = hg * jax.nn.sigmoid(hg) * hu
    return jnp.dot(h.astype(BF16), wd, preferred_element_type=F32)


def _ffn_kernel(x_ref, g_ref, wg_ref, wu_ref, wd_ref, o_ref, xn_ref, acc_ref):
    f = pl.program_id(1)

    @pl.when(f == 0)
    def _():
        xn_ref[...] = _rms(x_ref[...], g_ref[...]).astype(BF16)
        acc_ref[...] = jnp.zeros_like(acc_ref)

    acc_ref[...] += _swiglu_partial(xn_ref[...], wg_ref[...], wu_ref[...], wd_ref[...])

    @pl.when(f == pl.num_programs(1) - 1)
    def _():
        o_ref[...] = x_ref[...] + acc_ref[...]


def _ffn(x2, g, wg, wu, wd):
    tokens = x2.shape[0]
    ff = wg.shape[1]
    tt = min(FFN_TOK_TILE, tokens)
    fc = FFN_F_TILE
    assert tokens % tt == 0 and ff % fc == 0
    row = lambda i, f: (i, 0)
    return pl.pallas_call(
        _ffn_kernel,
        grid=(tokens // tt, ff // fc),
        in_specs=[
            pl.BlockSpec((tt, D_MODEL), row),
            pl.BlockSpec((1, D_MODEL), lambda i, f: (0, 0)),
            pl.BlockSpec((D_MODEL, fc), lambda i, f: (0, f)),
            pl.BlockSpec((D_MODEL, fc), lambda i, f: (0, f)),
            pl.BlockSpec((fc, D_MODEL), lambda i, f: (f, 0)),
        ],
        out_specs=pl.BlockSpec((tt, D_MODEL), row),
        out_shape=jax.ShapeDtypeStruct((tokens, D_MODEL), F32),
        scratch_shapes=[pltpu.VMEM((tt, D_MODEL), BF16), pltpu.VMEM((tt, D_MODEL), F32)],
        compiler_params=_params(("arbitrary", "arbitrary")),
        name="ffn",
    )(x2, g, wg, wu, wd)


def _router_kernel(x_ref, g_ref, rh_ref, rl_ref, xn_ref, ti_ref, tw_ref):
    tt = x_ref.shape[0]
    xn = _rms(x_ref[...], g_ref[...])
    xn_ref[...] = xn
    xh = xn.astype(BF16)
    xl = (xn - xh.astype(F32)).astype(BF16)
    logits = (jnp.dot(xh, rh_ref[...], preferred_element_type=F32)
              + jnp.dot(xh, rl_ref[...], preferred_element_type=F32)
              + jnp.dot(xl, rh_ref[...], preferred_element_type=F32))
    lane = lax.broadcasted_iota(jnp.int32, (tt, LANES), 1).astype(F32)
    lg = jnp.where(lane < N_EXPERTS, logits, -jnp.inf)
    m1 = jnp.max(lg, axis=1, keepdims=True)
    i1 = jnp.min(jnp.where(lg == m1, lane, float(LANES)), axis=1, keepdims=True)
    lg2 = jnp.where(lane == i1, -jnp.inf, lg)
    m2 = jnp.max(lg2, axis=1, keepdims=True)
    i2 = jnp.min(jnp.where(lg2 == m2, lane, float(LANES)), axis=1, keepdims=True)
    e2 = jnp.exp(m2 - m1)
    w1 = 1.0 / (1.0 + e2)
    w2 = e2 * w1
    ti_ref[...] = jnp.where(lane == 0.0, i1, i2).T[:8].astype(jnp.int32)
    tw_ref[...] = jnp.where(lane == 0.0, w1, w2)


def _router(x2, g, r_hi, r_lo):
    tokens = x2.shape[0]
    tt = min(TOK_TILE, tokens)
    row = lambda i: (i, 0)
    const = lambda i: (0, 0)
    return pl.pallas_call(
        _router_kernel,
        grid=(tokens // tt,),
        in_specs=[
            pl.BlockSpec((tt, D_MODEL), row),
            pl.BlockSpec((1, D_MODEL), const),
            pl.BlockSpec((D_MODEL, LANES), const),
            pl.BlockSpec((D_MODEL, LANES), const),
        ],
        out_specs=[
            pl.BlockSpec((tt, D_MODEL), row),
            pl.BlockSpec((8, tt), lambda i: (0, i)),
            pl.BlockSpec((tt, LANES), row),
        ],
        out_shape=[
            jax.ShapeDtypeStruct((tokens, D_MODEL), F32),
            jax.ShapeDtypeStruct((8, tokens), jnp.int32),
            jax.ShapeDtypeStruct((tokens, LANES), F32),
        ],
        compiler_params=_params(("arbitrary",)),
        name="moe_router",
    )(x2, g, r_hi, r_lo)


def _for_rows(n_rows, fn):
    def group(u, carry):
        for v in range(DMA_UNROLL):
            fn(u * DMA_UNROLL + v)
        return carry
    lax.fori_loop(0, n_rows // DMA_UNROLL, group, 0)


def _expert_ffn_kernel(be_ref, nu_ref, dst_ref, xn_hbm, wg_ref, wu_ref, wd_ref, y_hbm,
                       xg_ref, xb_ref, acc_ref, yb_ref, gsem, ssem, *, tokens):
    b = pl.program_id(0)
    f = pl.program_id(1)
    nf = pl.num_programs(1)
    nu = nu_ref[0]
    used = b < nu
    last = f == nf - 1
    slot = b % 2
    m = MOE_ROW_TILE
    per_step = m // EXPERT_F_STEPS
    nxt = jnp.minimum(b + 1, nu - 1)

    def gather(blk, r, sl, fs=None):
        fs, v = (r // per_step, r % per_step) if fs is None else (fs, r)
        tok = lax.rem(dst_ref[blk * m + fs * per_step + v], tokens)
        return pltpu.make_async_copy(xn_hbm.at[pl.ds(tok, 1)], xg_ref.at[sl, fs, pl.ds(v, 1)], gsem.at[sl])

    def scatter(blk, r, fs=None):
        fs, v = (r // per_step, r % per_step) if fs is None else (fs, r)
        row = dst_ref[blk * m + fs * per_step + v]
        return pltpu.make_async_copy(yb_ref.at[fs, pl.ds(v, 1)], y_hbm.at[pl.ds(row, 1)], ssem)

    @pl.when(jnp.logical_and(used, jnp.logical_and(f == 0, b == 0)))
    def _():
        _for_rows(m, lambda r: gather(0, r, 0).start())

    @pl.when(jnp.logical_and(used, f == 0))
    def _():
        _for_rows(m, lambda r: gather(b, r, slot).wait())
        xb_ref[...] = xg_ref[slot].reshape(m, D_MODEL).astype(BF16)
        acc_ref[...] = jnp.zeros_like(acc_ref)

    def step(with_scatter):
        for v in range(per_step):
            gather(nxt, v, 1 - slot, f).start()
            if with_scatter:
                scatter(b - 1, v, f).start()
        acc_ref[...] += _swiglu_partial(xb_ref[...], wg_ref[0], wu_ref[0], wd_ref[0])

    @pl.when(jnp.logical_and(used, b == 0))
    def _():
        step(False)

    @pl.when(jnp.logical_and(used, b > 0))
    def _():
        step(True)

    @pl.when(jnp.logical_and(used, jnp.logical_and(last, b > 0)))
    def _():
        _for_rows(m, lambda r: scatter(b - 1, r).wait())

    @pl.when(jnp.logical_and(used, last))
    def _():
        yb_ref[...] = acc_ref[...].reshape(EXPERT_F_STEPS, per_step, D_MODEL)

    @pl.when(jnp.logical_and(last, b == nu - 1))
    def _():
        _for_rows(m, lambda r: scatter(b, r).start())
        _for_rows(m, lambda r: gather(nxt, r, 1 - slot).wait())
        _for_rows(m, lambda r: scatter(b, r).wait())


def _expert_ffn(xn, dst, blk_expert, n_used, wg, wu, wd):
    tokens = xn.shape[0]
    rows = dst.shape[0]
    ff = wg.shape[2]
    m = MOE_ROW_TILE
    assert rows % m == 0 and ff % EXPERT_F_STEPS == 0 and m % EXPERT_F_STEPS == 0
    fc = ff // EXPERT_F_STEPS
    assert fc % LANES == 0
    return pl.pallas_call(
        functools.partial(_expert_ffn_kernel, tokens=tokens),
        grid_spec=pltpu.PrefetchScalarGridSpec(
            num_scalar_prefetch=3,
            grid=(rows // m, EXPERT_F_STEPS),
            in_specs=[
                pl.BlockSpec(memory_space=pl.ANY),
                pl.BlockSpec((1, D_MODEL, fc), lambda b, f, be, nu, ds: (be[b], 0, f)),
                pl.BlockSpec((1, D_MODEL, fc), lambda b, f, be, nu, ds: (be[b], 0, f)),
                pl.BlockSpec((1, fc, D_MODEL), lambda b, f, be, nu, ds: (be[b], f, 0)),
            ],
            out_specs=pl.BlockSpec(memory_space=pl.ANY),
            scratch_shapes=[
                pltpu.VMEM((2, EXPERT_F_STEPS, m // EXPERT_F_STEPS, D_MODEL), F32),
                pltpu.VMEM((m, D_MODEL), BF16),
                pltpu.VMEM((m, D_MODEL), F32),
                pltpu.VMEM((EXPERT_F_STEPS, m // EXPERT_F_STEPS, D_MODEL), F32),
                pltpu.SemaphoreType.DMA((2,)),
                pltpu.SemaphoreType.DMA,
            ],
        ),
        out_shape=jax.ShapeDtypeStruct((TOP_K * tokens + rows, D_MODEL), F32),
        compiler_params=_params(("arbitrary", "arbitrary")),
        name="expert_ffn",
    )(blk_expert, n_used, dst, xn, wg, wu, wd)


def _moe_combine_kernel(x_ref, y0_ref, y1_ref, tw_ref, gf_ref, o_ref, *, final_norm):
    tw = tw_ref[...]
    y = x_ref[...] + tw[:, 0:1] * y0_ref[...] + tw[:, 1:2] * y1_ref[...]
    if final_norm:
        y = _rms(y, gf_ref[...])
    o_ref[...] = y


def _moe_combine(x2, y, top_w, g_final, final_norm):
    tokens = x2.shape[0]
    tt = min(TOK_TILE, tokens)
    nt = tokens // tt
    row = lambda i: (i, 0)
    return pl.pallas_call(
        functools.partial(_moe_combine_kernel, final_norm=final_norm),
        grid=(nt,),
        in_specs=[
            pl.BlockSpec((tt, D_MODEL), row),
            pl.BlockSpec((tt, D_MODEL), row),
            pl.BlockSpec((tt, D_MODEL), lambda i: (nt + i, 0)),
            pl.BlockSpec((tt, LANES), row),
            pl.BlockSpec((1, D_MODEL), lambda i: (0, 0)),
        ],
        out_specs=pl.BlockSpec((tt, D_MODEL), row),
        out_shape=jax.ShapeDtypeStruct((tokens, D_MODEL), F32),
        compiler_params=_params(("arbitrary",)),
        name="moe_combine",
    )(x2, y, y, top_w, g_final)


def _route(top_i):
    m = MOE_ROW_TILE
    tokens = top_i.shape[1]
    n_assign = tokens * TOP_K
    rows = -(-(n_assign + N_EXPERTS * m) // m) * m
    e_flat = top_i.reshape(-1)
    counts = jnp.sum((e_flat[:, None] == jnp.arange(N_EXPERTS, dtype=jnp.int32)[None, :]).astype(jnp.int32), axis=0)
    padded = (counts + m - 1) // m * m
    ends = jnp.cumsum(padded)
    starts = ends - padded
    ustarts = jnp.cumsum(counts) - counts
    order = jnp.argsort(e_flat, stable=True).astype(jnp.int32)
    slot = jnp.arange(rows, dtype=jnp.int32)
    e_slot = jnp.minimum(jnp.sum((slot[:, None] >= ends[None, :]).astype(jnp.int32), axis=1), N_EXPERTS - 1)
    j = slot - starts[e_slot]
    valid = jnp.logical_and(j < counts[e_slot], slot < ends[-1])
    a = order[jnp.clip(ustarts[e_slot] + j, 0, n_assign - 1)]
    dst = jnp.where(valid, a, n_assign + slot).astype(jnp.int32)
    n_used = (ends[-1] // m).astype(jnp.int32)[None]
    return dst, e_slot[::m], n_used


def _moe(x2, g, r_hi, r_lo, wg, wu, wd, layer, g_final, final_norm):
    xn, top_i, top_w = _router(x2, g, r_hi, r_lo)
    dst, blk_expert, n_used = _route(top_i[:TOP_K])
    y = _expert_ffn(xn, dst, blk_expert + layer * N_EXPERTS, n_used, wg, wu, wd)
    return _moe_combine(x2, y, top_w, g_final, final_norm)


def _pad_heads(w, n_heads, width, offset=0):
    r = w.shape[0]
    w = w.reshape(r, n_heads, width)
    w = jnp.pad(w, ((0, 0), (0, 0), (offset, LANES - width - offset)))
    return w.reshape(r, n_heads * LANES)


def _rope_tables(ang, base):
    half = ang.shape[1]
    cos = jnp.cos(ang)
    sin = jnp.sin(ang)
    pad = ((0, 0), (base, LANES - base - 2 * half))
    cos_t = jnp.pad(jnp.concatenate([cos, cos], axis=1) - 1.0, pad) + 1.0
    sin_t = jnp.pad(jnp.concatenate([-sin, sin], axis=1), pad)
    return cos_t, sin_t


def _rope_angles(pos, dim):
    inv = ROPE_THETA ** (-jnp.arange(0, dim, 2, dtype=F32) / dim)
    return pos.astype(F32)[:, None] * inv[None, :]


def _lane_gain(g):
    return jnp.pad(g.astype(F32), (0, LANES - g.shape[0]))[None, :]


def _trunk(x, p):
    bsz, seq, _ = x.shape
    x2 = x.reshape(bsz * seq, D_MODEL)
    n_rows = seq // GRID_W
    row = jnp.repeat(jnp.arange(n_rows), GRID_W)
    col = jnp.tile(jnp.arange(GRID_W), n_rows)
    ang_axial = jnp.concatenate([_rope_angles(row, HEAD_DIM // 2), _rope_angles(col, HEAD_DIM // 2)], axis=-1)
    cos_ax, sin_ax = _rope_tables(ang_axial, 0)
    cos_1d, sin_1d = _rope_tables(_rope_angles(jnp.arange(seq), MLA_ROPE_DIM), _MLA_ROPE_BASE)
    depth = p["ev_norm_mix"].shape[0] + p["od_norm_mix"].shape[0]
    assert depth % 2 == 0
    a_grp = A_HEADS // A_KV_HEADS
    for layer in range(depth):
        i = layer // 2
        last_layer = layer == depth - 1
        if layer % 2 == 0:
            aq, ak, avt, bq, bk, bvt = _even_in(
                x2, bsz, seq, p["ev_norm_mix"][i][None, :], p["ev_w_in"][i], cos_ax, sin_ax,
                p["b_q_norm"][i], p["b_k_norm"][i])
            ya = _window_attn(aq, ak, avt, p["a_slope"], p["a_sink"][i], bsz, seq, A_KV_HEADS, a_grp)
            yb = _dense_attn(bq, bk, bvt, bsz, seq, B_KV_HEADS, B_HEADS // B_KV_HEADS, 1, 256)
            x2 = _out_proj(x2, [ya, yb], [p["ev_w_out_a"][i], p["ev_w_out_b"][i]])
            x2 = _ffn(x2, p["ev_norm_ffn"][i][None, :], p["ffn_w_gate"][i], p["ffn_w_up"][i], p["ffn_w_down"][i])
        else:
            q, k, vt = _odd_in(
                x2, bsz, seq, p["od_norm_mix"][i][None, :], p["od_w_in"][i], p["mla_q_norm"][i][None, :],
                p["mla_w_q_up"][i], p["mla_kv_norm"][i][None, :], p["mla_w_k_up"][i], p["mla_w_v_up"][i],
                cos_1d, sin_1d)
            yc = _dense_attn(q, k, vt, bsz, seq, MLA_HEADS, 1, 2, 1024)
            x2 = _out_proj(x2, [yc], [p["od_w_out"][i]])
            x2 = _moe(x2, p["od_norm_ffn"][i][None, :], p["router_hi"][i], p["router_lo"][i],
                      p["moe_w_gate"], p["moe_w_up"], p["moe_w_down"], i,
                      p["final_norm"], last_layer)
    return x2.reshape(bsz, seq, D_MODEL)


def _prepare(ev_norm_mix, ev_w_in, a_sink, b_q_norm, b_k_norm, ev_w_out, ev_norm_ffn,
             ffn_w_gate, ffn_w_up, ffn_w_down, od_norm_mix, od_w_in, mla_q_norm, mla_w_q_up,
             mla_kv_norm, mla_w_kv_up, od_w_out, od_norm_ffn, moe_router, moe_w_gate, moe_w_up,
             moe_w_down, final_norm):
    hd = HEAD_DIM
    a_grp = A_HEADS // A_KV_HEADS

    def even_w_in(w):
        sizes = [A_HEADS * hd, A_KV_HEADS * hd, A_KV_HEADS * hd, B_HEADS * hd, B_KV_HEADS * hd, B_KV_HEADS * hd]
        aq, ak, av, bq, bk, bv = jnp.split(w, list(np.cumsum(sizes)[:-1]), axis=-1)
        return jnp.concatenate([_pad_heads(aq, A_HEADS, hd), _pad_heads(ak, A_KV_HEADS, hd),
                                _pad_heads(bq, B_HEADS, hd), _pad_heads(bk, B_KV_HEADS, hd), av, bv],
                               axis=-1).astype(BF16)

    def odd_w_in(w):
        c = w[:, :MLA_Q_RANK + MLA_KV_RANK]
        kr = _pad_heads(w[:, MLA_Q_RANK + MLA_KV_RANK:], 1, MLA_ROPE_DIM, _MLA_ROPE_BASE)
        return jnp.concatenate([c, kr], axis=-1).astype(BF16)

    def q_up(w):
        return _pad_heads(w, MLA_HEADS, MLA_QK_DIM).astype(BF16)

    def kv_up(w):
        w = w.reshape(MLA_KV_RANK, MLA_HEADS, MLA_NOPE_DIM + MLA_V_DIM)
        wk = _pad_heads(w[:, :, :MLA_NOPE_DIM].reshape(MLA_KV_RANK, -1), MLA_HEADS, MLA_NOPE_DIM)
        wv = w[:, :, MLA_NOPE_DIM:].reshape(MLA_KV_RANK, -1)
        return wk.astype(BF16), wv.astype(BF16)

    slopes = jnp.asarray(2.0 ** (-8.0 * np.arange(1, A_HEADS + 1) / A_HEADS), dtype=F32)
    per_col = lambda v: jnp.repeat(v.astype(F32).reshape(A_KV_HEADS, a_grp), WIN_Q, axis=1)[:, None, :] * LOG2E
    router = jnp.pad(moe_router.astype(F32), ((0, 0), (0, 0), (0, LANES - N_EXPERTS)))
    router_hi = router.astype(BF16)
    kv = [kv_up(w) for w in mla_w_kv_up]
    return {
        "ev_norm_mix": ev_norm_mix, "ev_w_in": jnp.stack([even_w_in(w) for w in ev_w_in]),
        "a_slope": per_col(slopes), "a_sink": jnp.stack([per_col(s) for s in a_sink]),
        "b_q_norm": jnp.stack([_lane_gain(g) for g in b_q_norm]),
        "b_k_norm": jnp.stack([_lane_gain(g) for g in b_k_norm]),
        "ev_w_out_a": ev_w_out[:, :A_HEADS * hd].astype(BF16),
        "ev_w_out_b": ev_w_out[:, A_HEADS * hd:].astype(BF16),
        "ev_norm_ffn": ev_norm_ffn,
        "ffn_w_gate": ffn_w_gate.astype(BF16), "ffn_w_up": ffn_w_up.astype(BF16),
        "ffn_w_down": ffn_w_down.astype(BF16),
        "od_norm_mix": od_norm_mix, "od_w_in": jnp.stack([odd_w_in(w) for w in od_w_in]),
        "mla_q_norm": mla_q_norm, "mla_w_q_up": jnp.stack([q_up(w) for w in mla_w_q_up]),
        "mla_kv_norm": mla_kv_norm,
        "mla_w_k_up": jnp.stack([a for a, _ in kv]), "mla_w_v_up": jnp.stack([b for _, b in kv]),
        "od_w_out": od_w_out.astype(BF16), "od_norm_ffn": od_norm_ffn,
        "router_hi": router_hi, "router_lo": (router - router_hi.astype(F32)).astype(BF16),
        "moe_w_gate": moe_w_gate.astype(BF16).reshape((-1,) + moe_w_gate.shape[2:]),
        "moe_w_up": moe_w_up.astype(BF16).reshape((-1,) + moe_w_up.shape[2:]),
        "moe_w_down": moe_w_down.astype(BF16).reshape((-1,) + moe_w_down.shape[2:]),
        "final_norm": final_norm[None, :],
    }


def kernel(x_prompt, x_sample, ev_norm_mix, ev_w_in, a_sink, b_q_norm, b_k_norm, ev_w_out, ev_norm_ffn, ffn_w_gate, ffn_w_up, ffn_w_down, od_norm_mix, od_w_in, mla_q_norm, mla_w_q_up, mla_kv_norm, mla_w_kv_up, od_w_out, od_norm_ffn, moe_router, moe_w_gate, moe_w_up, moe_w_down, final_norm):
    p = _prepare(ev_norm_mix, ev_w_in, a_sink, b_q_norm, b_k_norm, ev_w_out, ev_norm_ffn,
                 ffn_w_gate, ffn_w_up, ffn_w_down, od_norm_mix, od_w_in, mla_q_norm, mla_w_q_up,
                 mla_kv_norm, mla_w_kv_up, od_w_out, od_norm_ffn, moe_router, moe_w_gate, moe_w_up,
                 moe_w_down, final_norm)
    return (_trunk(x_prompt, p), _trunk(x_sample, p))
```
